```python
import jax, jax.numpy as jnp
from jax import lax
import numpy as np

D_MODEL = 2048
BATCH = 4
SEQ = 4096
DEPTH = 4

CTX_LEN = 256
GRID_W = 64
HEAD_DIM = D_MODEL // 16
ATTN_HEADS = 8
ATTN_KV_HEADS = 2
ATTN_GROUP = ATTN_HEADS // ATTN_KV_HEADS
FOURIER_GROUPS = 4
RET_HEADS = 4
ATTN_WIDTH = ATTN_HEADS * HEAD_DIM
KV_WIDTH = ATTN_KV_HEADS * HEAD_DIM
FOURIER_WIDTH = FOURIER_GROUPS * HEAD_DIM
RET_WIDTH = RET_HEADS * HEAD_DIM
MIX_WIDTH = ATTN_WIDTH + FOURIER_WIDTH + RET_WIDTH
SPLIT_POINTS = tuple(int(s) for s in np.cumsum(
    [ATTN_WIDTH, KV_WIDTH, KV_WIDTH, FOURIER_WIDTH, RET_WIDTH, RET_WIDTH, RET_WIDTH, RET_WIDTH]))
IN_WIDTH = ATTN_WIDTH + 2 * KV_WIDTH + FOURIER_WIDTH + 5 * RET_WIDTH
Q_BLOCK = 128
RET_CHUNK = 128
N_EXPERTS = 16
EXPERT_FF = D_MODEL // 2
CAPACITY_FACTOR = 2
ROPE_BASE = 10000.0
EPS = 1e-6

kernel_name = "hymba_style_attn_fourier_retention_ecmoe"


def rmsnorm(x, g):
    xf = x.astype(jnp.float32)
    y = xf * lax.rsqrt(jnp.mean(xf * xf, axis=-1, keepdims=True) + EPS)
    return (y * g.astype(jnp.float32)).astype(x.dtype)


def head_norm(o):
    of = o.astype(jnp.float32)
    mu = jnp.mean(of, axis=-1, keepdims=True)
    var = jnp.mean(jnp.square(of - mu), axis=-1, keepdims=True)
    return ((of - mu) * lax.rsqrt(var + EPS)).astype(o.dtype)


def adaln(cond, w, b):
    m = jax.nn.silu(cond) @ w + b
    m = m.reshape(m.shape[:-1] + (6, D_MODEL))
    return [m[..., i, :][..., None, :] for i in range(6)]


def modulate(h, shift, scale):
    return h * (1 + scale) + shift


def _rope(x, pos):
    half = x.shape[-1] // 2
    freqs = ROPE_BASE ** (-jnp.arange(half, dtype=jnp.float32) / half)
    ang = pos.astype(jnp.float32)[:, None] * freqs[None, :]
    cos = jnp.cos(ang)[:, None, :].astype(x.dtype)
    sin = jnp.sin(ang)[:, None, :].astype(x.dtype)
    x1, x2 = x[..., :half], x[..., half:]
    return jnp.concatenate([x1 * cos - x2 * sin, x1 * sin + x2 * cos], axis=-1)


def axial_rope(x):
    L, d = x.shape[1], x.shape[-1]
    n_rows = L // GRID_W
    rows = jnp.repeat(jnp.arange(n_rows), GRID_W)
    cols = jnp.tile(jnp.arange(GRID_W), n_rows)
    half = d // 2
    return jnp.concatenate([_rope(x[..., :half], rows), _rope(x[..., half:], cols)], axis=-1)


def _attend(q, k, v):
    B, Lq, H, dh = q.shape
    qg = q.reshape(B, Lq, ATTN_KV_HEADS, ATTN_GROUP, dh)
    s = jnp.einsum('bqkgd,bskd->bkgqs', qg, k).astype(jnp.float32) * (dh ** -0.5)
    p = jax.nn.softmax(s, axis=-1).astype(v.dtype)
    o = jnp.einsum('bkgqs,bskd->bqkgd', p, v)
    return o.reshape(B, Lq, H * dh)


def attend_blocks(q, k, v):
    B, S, H, dh = q.shape
    nb = S // Q_BLOCK
    qb = q.reshape(B, nb, Q_BLOCK, H, dh).transpose(1, 0, 2, 3, 4)
    o = lax.map(lambda qblk: _attend(qblk, k, v), qb)
    return o.transpose(1, 0, 2, 3).reshape(B, S, H * dh)


def fourier_mix(f):
    B, L, _ = f.shape
    fg = f.reshape(B, L, FOURIER_GROUPS, HEAD_DIM).astype(jnp.float32)
    y = jnp.fft.fft2(fg, axes=(1, 3), norm='ortho').real
    return y.reshape(B, L, FOURIER_WIDTH).astype(f.dtype)


def retention_chunkwise(q, k, v, log_decay, state0):
    B, L, H, dk = q.shape
    dt = q.dtype
    n = L // RET_CHUNK
    lg = log_decay.astype(jnp.float32)
    i = jnp.arange(RET_CHUNK, dtype=jnp.float32)
    diff = i[:, None] - i[None, :]
    intra = jnp.where(diff[None] >= 0,
                      jnp.exp(lg[:, None, None] * jnp.maximum(diff, 0.0)[None]), 0.0).astype(dt)
    q_dec = jnp.exp(lg[None, :] * (i[:, None] + 1.0)).astype(dt)
    k_dec = jnp.exp(lg[None, :] * (RET_CHUNK - 1.0 - i)[:, None]).astype(dt)
    c_dec = jnp.exp(lg * RET_CHUNK).astype(dt)

    def chunks(a):
        return a.reshape(B, n, RET_CHUNK, H, a.shape[-1]).transpose(1, 0, 2, 3, 4)

    def step(state, inp):
        qc, kc, vc = inp
        sc = jnp.einsum('bihd,bjhd->bhij', qc, kc) * intra[None]
        o = (jnp.einsum('bhij,bjhe->bihe', sc, vc)
             + jnp.einsum('bihd,bhde->bihe', qc, state) * q_dec[None, :, :, None])
        state = (state * c_dec[None, :, None, None]
                 + jnp.einsum('bjhd,bjhe->bhde', kc * k_dec[None, :, :, None], vc))
        return state, o

    state, o = lax.scan(step, state0, (chunks(q), chunks(k), chunks(v)))
    return o.transpose(1, 0, 2, 3, 4).reshape(B, L, H, v.shape[-1]), state


def retention_direction(q_c, k_c, v_c, q_l, k_l, v_l, log_decay):
    B, Lc, H, dk = q_c.shape
    Ls = q_l.shape[1]
    pos_c = jnp.arange(Lc)
    pos_l = Lc + jnp.arange(Ls)
    state0 = jnp.zeros((B, H, dk, v_c.shape[-1]), q_c.dtype)
    o_c, state = retention_chunkwise(_rope(q_c, pos_c), _rope(k_c, pos_c), v_c, log_decay, state0)
    o_l, _ = retention_chunkwise(_rope(q_l, pos_l), _rope(k_l, pos_l), v_l, log_decay, state)
    return o_c, o_l


def ret_combine(o_f, o_b, g_f, g_b):
    B, L = g_f.shape[:2]
    gn = lambda o: head_norm(o).reshape(B, L, RET_WIDTH)
    return jax.nn.silu(g_f) * gn(o_f) + jax.nn.silu(g_b) * gn(o_b)


def in_project(h, w_in_l, q_gain, k_gain):
    B, L, _ = h.shape
    p = jnp.einsum('bld,de->ble', h, w_in_l)
    q, k, v, f, rq, rk, rv, gf, gb = jnp.split(p, SPLIT_POINTS, axis=-1)
    heads = lambda a, nh: a.reshape(B, L, nh, HEAD_DIM)
    q = rmsnorm(heads(q, ATTN_HEADS), q_gain)
    k = rmsnorm(heads(k, ATTN_KV_HEADS), k_gain)
    v = heads(v, ATTN_KV_HEADS)
    rq = heads(rq, RET_HEADS)
    rk = heads(rk, RET_HEADS) * (HEAD_DIM ** -0.5)
    rv = heads(rv, RET_HEADS)
    return q, k, v, f, rq, rk, rv, gf, gb


def ec_moe(h, w_router_l, w_gate_l, w_up_l, w_down_l):
    B, N, D = h.shape
    cap = CAPACITY_FACTOR * N // N_EXPERTS
    logits = jnp.einsum('bnd,de->bne', h, w_router_l).astype(jnp.float32)
    aff = jax.nn.softmax(logits, axis=-1)
    gate, idx = lax.top_k(aff.transpose(0, 2, 1), cap)
    xg = jax.vmap(lambda hb, ib: hb[ib])(h, idx)
    a = jnp.einsum('becd,edf->becf', xg, w_gate_l)
    u = jnp.einsum('becd,edf->becf', xg, w_up_l)
    y = jnp.einsum('becf,efd->becd', jax.nn.silu(a) * u, w_down_l)
    y = y * gate[..., None].astype(y.dtype)
    return jax.vmap(lambda ib, yb: jnp.zeros((N, D), yb.dtype).at[ib.reshape(-1)].add(
        yb.reshape(-1, D)))(idx, y)


def layer(x, xc, c, c_ctx, w_ada_l, b_ada_l, g_mix, g_ffn, w_in_l, q_gain, k_gain,
          log_decay_l, w_out_l, w_router_l, w_gate_l, w_up_l, w_down_l, ctx_out):
    sh1, sc1, gt1, sh2, sc2, gt2 = adaln(c, w_ada_l, b_ada_l)
    sh1c, sc1c, gt1c, sh2c, sc2c, gt2c = adaln(c_ctx, w_ada_l, b_ada_l)

    h = modulate(rmsnorm(x, g_mix), sh1, sc1)
    hc = modulate(rmsnorm(xc, g_mix), sh1c, sc1c)
    q, k, v, f, rq, rk, rv, gf, gb = in_project(h, w_in_l, q_gain, k_gain)
    qc, kc, vc, fc, rqc, rkc, rvc, gfc, gbc = in_project(hc, w_in_l, q_gain, k_gain)

    q, k = axial_rope(q), axial_rope(k)
    a = attend_blocks(q, jnp.concatenate([kc, k], axis=1), jnp.concatenate([vc, v], axis=1))

    ocf, olf = retention_direction(rqc, rkc, rvc, rq, rk, rv, log_decay_l[0])
    flip = lambda t: jnp.flip(t, axis=1)
    ocb, olb = retention_direction(flip(rqc), flip(rkc), flip(rvc), flip(rq), flip(rk), flip(rv),
                                   log_decay_l[1])
    ocb, olb = flip(ocb), flip(olb)
    r = ret_combine(olf, olb, gf, gb)

    y = jnp.concatenate([a, fourier_mix(f), r], axis=-1) @ w_out_l
    x = x + gt1 * y
    x = x + gt2 * ec_moe(modulate(rmsnorm(x, g_ffn), sh2, sc2), w_router_l, w_gate_l, w_up_l, w_down_l)

    if ctx_out:
        ac = _attend(qc, kc, vc)
        rc = ret_combine(ocf, ocb, gfc, gbc)
        yc = jnp.concatenate([ac, fourier_mix(fc), rc], axis=-1) @ w_out_l
        xc = xc + gt1c * yc
        xc = xc + gt2c * ec_moe(modulate(rmsnorm(xc, g_ffn), sh2c, sc2c),
                                w_router_l, w_gate_l, w_up_l, w_down_l)
    return x, xc


def setup_inputs(seed: int = 0) -> dict:
    key = jax.random.key(seed)
    ks = jax.random.split(key, 17)
    f32 = jnp.float32
    nrm = lambda k, shape, s: jax.random.normal(k, shape, f32) * s
    base_decay = jnp.log(1.0 - 2.0 ** (-5.0 - jnp.arange(RET_HEADS, dtype=f32)))
    return {
        "x": nrm(ks[0], (BATCH, SEQ, D_MODEL), 1.0),
        "c": nrm(ks[1], (BATCH, D_MODEL), 1.0),
        "ctx": nrm(ks[2], (BATCH, CTX_LEN, D_MODEL), 1.0),
        "c_ctx": nrm(ks[3], (D_MODEL,), 1.0),
        "w_ada": nrm(ks[4], (DEPTH, D_MODEL, 6 * D_MODEL), 0.5 * D_MODEL ** -0.5),
        "b_ada": nrm(ks[5], (DEPTH, 6 * D_MODEL), 0.02),
        "norm_mix": 1.0 + nrm(ks[6], (DEPTH, D_MODEL), 0.02),
        "norm_ffn": 1.0 + nrm(ks[7], (DEPTH, D_MODEL), 0.02),
        "w_in": nrm(ks[8], (DEPTH, D_MODEL, IN_WIDTH), D_MODEL ** -0.5),
        "q_norm": 1.0 + nrm(ks[9], (DEPTH, HEAD_DIM), 0.02),
        "k_norm": 1.0 + nrm(ks[10], (DEPTH, HEAD_DIM), 0.02),
        "ret_log_decay": base_decay * (1.0 + nrm(ks[11], (DEPTH, 2, RET_HEADS), 0.05)),
        "w_out": nrm(ks[12], (DEPTH, MIX_WIDTH, D_MODEL), MIX_WIDTH ** -0.5),
        "w_router": nrm(ks[13], (DEPTH, D_MODEL, N_EXPERTS), D_MODEL ** -0.5),
        "w_gate": nrm(ks[14], (DEPTH, N_EXPERTS, D_MODEL, EXPERT_FF), D_MODEL ** -0.5),
        "w_up": nrm(ks[15], (DEPTH, N_EXPERTS, D_MODEL, EXPERT_FF), D_MODEL ** -0.5),
        "w_down": nrm(ks[16], (DEPTH, N_EXPERTS, EXPERT_FF, D_MODEL), EXPERT_FF ** -0.5),
    }


def reference(x, c, ctx, c_ctx, w_ada, b_ada, norm_mix, norm_ffn, w_in, q_norm, k_norm,
              ret_log_decay, w_out, w_router, w_gate, w_up, w_down):
    xc = ctx
    for l in range(DEPTH):
        x, xc = layer(x, xc, c, c_ctx, w_ada[l], b_ada[l], norm_mix[l], norm_ffn[l], w_in[l],
                      q_norm[l], k_norm[l], ret_log_decay[l], w_out[l], w_router[l],
                      w_gate[l], w_up[l], w_down[l], l < DEPTH - 1)
    return x
```

```python
import functools
import math

import jax
import jax.numpy as jnp
from jax import lax
from jax.experimental import pallas as pl
from jax.experimental.pallas import tpu as pltpu

HEAD_DIM = 128
ATTN_HEADS = 8
ATTN_KV_HEADS = 2
ATTN_GROUP = ATTN_HEADS // ATTN_KV_HEADS
FOURIER_GROUPS = 4
RET_HEADS = 4
RET_CHUNK = 128
GRID_W = 64
ROPE_BASE = 10000.0
EPS = 1e-6
CAPACITY_FACTOR = 2

ATTN_WIDTH = ATTN_HEADS * HEAD_DIM
KV_WIDTH = ATTN_KV_HEADS * HEAD_DIM
FOURIER_WIDTH = FOURIER_GROUPS * HEAD_DIM
RET_WIDTH = RET_HEADS * HEAD_DIM

ROW_BLK = 256
LANES = 128
MIB = 1024 * 1024

F32 = jnp.float32
BF16 = jnp.bfloat16


def _cparams(sem, vmem_mib):
    return pltpu.CompilerParams(dimension_semantics=sem, vmem_limit_bytes=vmem_mib * MIB)


def _largest_tile(n, cap, mult=8):
    best = mult
    for t in range(mult, min(n, cap) + 1, mult):
        if n % t == 0:
            best = t
    return best


def _split(a):
    hi = a.astype(BF16)
    lo = (a - hi.astype(F32)).astype(BF16)
    return hi, lo


def _dot(a, b):
    return jnp.dot(a, b, preferred_element_type=F32)


def _dot_nt(a, b):
    return lax.dot_general(a, b, (((1,), (1,)), ((), ())), preferred_element_type=F32)


def _dot_tn(a, b):
    return lax.dot_general(a, b, (((0,), (0,)), ((), ())), preferred_element_type=F32)


def _silu(a):
    return a / (1.0 + jnp.exp(-a))


def _adaln_kernel(c_ref, w_ref, b_ref, o_ref):
    s = _silu(c_ref[...])
    sh, sl = _split(s)
    wh, wl = _split(w_ref[0])
    o_ref[0] = _dot(sh, wh) + (_dot(sh, wl) + _dot(sl, wh)) + b_ref[0]


def _adaln(cond8, w_ada, b_ada):
    depth, d, n = w_ada.shape
    tn = _largest_tile(n, 768, LANES)
    return pl.pallas_call(
        _adaln_kernel,
        grid=(depth, n // tn),
        in_specs=[
            pl.BlockSpec((8, d), lambda l, j: (0, 0)),
            pl.BlockSpec((1, d, tn), lambda l, j: (l, 0, j)),
            pl.BlockSpec((1, 1, tn), lambda l, j: (l, 0, j)),
        ],
        out_specs=pl.BlockSpec((1, 8, tn), lambda l, j: (l, 0, j)),
        out_shape=jax.ShapeDtypeStruct((depth, 8, n), F32),
        compiler_params=_cparams(("parallel", "parallel"), 48),
        name="adaln",
    )(cond8, w_ada, b_ada.reshape(depth, 1, n))


def _rms_mod(x, g, shift, scale):
    y = x * lax.rsqrt(jnp.mean(x * x, axis=-1, keepdims=True) + EPS)
    return (y * g) * (1.0 + scale) + shift


def _norm_kernel(*refs, has_delta, want_h, n_lat_blk):
    refs = list(refs)
    x_ref = refs.pop(0)
    x = x_ref[0]
    if has_delta:
        dl_ref, dc_ref, pm_ref = refs.pop(0), refs.pop(0), refs.pop(0)
        is_ctx = pl.program_id(1) >= n_lat_blk
        delta = jnp.where(is_ctx, dc_ref[...], dl_ref[0])
        x = x + pm_ref[0, 5:6, :] * delta
    if want_h:
        m_ref, g_ref = refs.pop(0), refs.pop(0)
    if has_delta:
        xo_ref = refs.pop(0)
        xo_ref[0] = x
    if want_h:
        h_ref = refs.pop(0)
        h_ref[0] = _rms_mod(x, g_ref[...], m_ref[0, 0:1, :], m_ref[0, 1:2, :]).astype(BF16)


def _norm(x, mods, g, n_lat_blk, delta_lat=None, delta_ctx=None, prev_mods=None,
          want_h=True, lat_only=False):
    bsz, l, d = x.shape
    n_blk = n_lat_blk if lat_only else l // ROW_BLK
    has_delta = delta_lat is not None

    def mod_row(b, t):
        return (jnp.where(t >= n_lat_blk, bsz, b), 0, 0)

    xspec = pl.BlockSpec((1, ROW_BLK, d), lambda b, t: (b, t, 0))
    in_specs, args = [xspec], [x]
    if has_delta:
        in_specs += [
            pl.BlockSpec((1, ROW_BLK, d), lambda b, t: (b, jnp.minimum(t, n_lat_blk - 1), 0)),
            pl.BlockSpec((ROW_BLK, d), lambda b, t: (b, 0)),
            pl.BlockSpec((1, 6, d), mod_row),
        ]
        args += [delta_lat, delta_ctx, prev_mods]
    if want_h:
        in_specs += [pl.BlockSpec((1, 6, d), mod_row), pl.BlockSpec((1, d), lambda b, t: (0, 0))]
        args += [mods, g.reshape(1, d)]
    out_specs, out_shape = [], []
    rows = n_blk * ROW_BLK
    if has_delta:
        out_specs.append(xspec)
        out_shape.append(jax.ShapeDtypeStruct((bsz, rows, d), F32))
    if want_h:
        out_specs.append(xspec)
        out_shape.append(jax.ShapeDtypeStruct((bsz, rows, d), BF16))
    outs = pl.pallas_call(
        functools.partial(_norm_kernel, has_delta=has_delta, want_h=want_h, n_lat_blk=n_lat_blk),
        grid=(bsz, n_blk),
        in_specs=in_specs,
        out_specs=out_specs,
        out_shape=out_shape,
        compiler_params=_cparams(("parallel", "parallel"), 32),
        name="norm",
    )(*args)
    outs = list(outs)
    x_new = outs.pop(0) if has_delta else x
    h = outs.pop(0) if want_h else None
    return x_new, h


def _rope(z, tab_ref, k, half):
    return (z * tab_ref[k] + pltpu.roll(z, LANES - half, 1) * tab_ref[k + 1]
            + pltpu.roll(z, half, 1) * tab_ref[k + 2])


def _proj_kernel(*refs, flavor, n_heads, scale, half, split_at):
    h_ref, w_ref = refs[0], refs[1]
    acc = _dot(h_ref[...], w_ref[...])
    if flavor == "plain":
        refs[2][...] = acc.astype(BF16)
    elif flavor == "split":
        refs[2][...] = acc[:, :split_at].astype(BF16)
        refs[3][...] = acc[:, split_at:].astype(BF16)
    elif flavor == "silu":
        refs[2][...] = _silu(acc).astype(BF16)
    elif flavor == "qk":
        gain_ref, tab_ref, o_ref = refs[2], refs[3], refs[4]
        for hd in range(n_heads):
            sl = slice(hd * HEAD_DIM, (hd + 1) * HEAD_DIM)
            z = acc[:, sl]
            z = (z * lax.rsqrt(jnp.mean(z * z, axis=-1, keepdims=True) + EPS)) * gain_ref[...]
            o_ref[:, sl] = (_rope(z, tab_ref, 0, half) * scale).astype(BF16)
    elif flavor == "ret":
        tab_ref, o_ref = refs[2], refs[3]
        width = n_heads * HEAD_DIM
        for hd in range(n_heads):
            sl = slice(hd * HEAD_DIM, (hd + 1) * HEAD_DIM)
            z = acc[:, sl] * scale
            o_ref[:, sl] = _rope(z, tab_ref, 0, half).astype(BF16)
            o_ref[:, width + hd * HEAD_DIM: width + (hd + 1) * HEAD_DIM] = (
                _rope(z, tab_ref, 3, half).astype(BF16))
    else:
        raise ValueError(flavor)


def _proj(h2d, w, seq_len, flavor, *, gain=None, tab=None, n_heads=0, scale=1.0, half=0,
          split_at=0):
    t_rows, d = h2d.shape
    n = w.shape[1]
    tm = _largest_tile(seq_len, 1088)
    per_seq = seq_len // tm
    in_specs = [pl.BlockSpec((tm, d), lambda i: (i, 0)), pl.BlockSpec((d, n), lambda i: (0, 0))]
    args = [h2d, w]
    if flavor == "qk":
        in_specs.append(pl.BlockSpec((1, HEAD_DIM), lambda i: (0, 0)))
        args.append(gain.reshape(1, HEAD_DIM))
    if flavor in ("qk", "ret"):
        ntab = tab.shape[0]
        in_specs.append(pl.BlockSpec((ntab, tm, HEAD_DIM), lambda i: (0, i % per_seq, 0)))
        args.append(tab)
    if flavor == "split":
        widths = [split_at, n - split_at]
    elif flavor == "ret":
        widths = [2 * n]
    else:
        widths = [n]
    out_specs = [pl.BlockSpec((tm, wd), lambda i: (i, 0)) for wd in widths]
    out_shape = [jax.ShapeDtypeStruct((t_rows, wd), BF16) for wd in widths]
    outs = pl.pallas_call(
        functools.partial(_proj_kernel, flavor=flavor, n_heads=n_heads, scale=scale, half=half,
                          split_at=split_at),
        grid=(t_rows // tm,),
        in_specs=in_specs,
        out_specs=out_specs,
        out_shape=out_shape,
        compiler_params=_cparams(("parallel",), 48),
        name="proj_" + flavor,
    )(*args)
    return outs if len(outs) > 1 else outs[0]


def _attn_kernel(q_ref, k_ref, v_ref, o_ref, *, tk, n_lat_qblk, n_lat_chunks, n_ctx_chunks):
    is_ctx = pl.program_id(2) >= n_lat_qblk
    c_lo = jnp.where(is_ctx, n_lat_chunks, 0)
    c_hi = n_lat_chunks + n_ctx_chunks
    tq = q_ref.shape[0]
    for hd in range(ATTN_GROUP):
        sl = slice(hd * HEAD_DIM, (hd + 1) * HEAD_DIM)
        q = q_ref[:, sl]

        def body(c, carry, q=q):
            m, l, acc = carry
            off = pl.multiple_of(c * tk, tk)
            k = k_ref[pl.ds(off, tk), :]
            v = v_ref[pl.ds(off, tk), :]
            s = _dot_nt(q, k)
            m_new = jnp.maximum(m, jnp.max(s, axis=-1, keepdims=True))
            alpha = jnp.exp2(m - m_new)
            p = jnp.exp2(s - m_new)
            l = alpha * l + jnp.sum(p, axis=-1, keepdims=True)
            acc = alpha * acc + _dot(p.astype(BF16), v)
            return m_new, l, acc

        init = (jnp.full((tq, 1), -jnp.inf, F32), jnp.zeros((tq, 1), F32),
                jnp.zeros((tq, HEAD_DIM), F32))
        _, l, acc = lax.fori_loop(c_lo, c_hi, body, init)
        o_ref[:, sl] = (acc / l).astype(BF16)


def _attention(q, k, v, bsz, seq_len, n_lat):
    tq = ROW_BLK
    tk = ROW_BLK
    per_seq = seq_len // tq
    gw = ATTN_GROUP * HEAD_DIM
    return pl.pallas_call(
        functools.partial(_attn_kernel, tk=tk, n_lat_qblk=n_lat // tq, n_lat_chunks=n_lat // tk,
                          n_ctx_chunks=(seq_len - n_lat) // tk),
        grid=(bsz, ATTN_KV_HEADS, per_seq),
        in_specs=[
            pl.BlockSpec((tq, gw), lambda b, g, i: (b * per_seq + i, g)),
            pl.BlockSpec((seq_len, HEAD_DIM), lambda b, g, i: (b, g)),
            pl.BlockSpec((seq_len, HEAD_DIM), lambda b, g, i: (b, g)),
        ],
        out_specs=pl.BlockSpec((tq, gw), lambda b, g, i: (b * per_seq + i, g)),
        out_shape=jax.ShapeDtypeStruct(q.shape, BF16),
        compiler_params=_cparams(("parallel", "parallel", "parallel"), 32),
        name="attention",
    )(q, k, v)


def _fourier_a_kernel(f_ref, wc_ref, o_ref):
    g = _dot(f_ref[...], wc_ref[...])
    wdt = f_ref.shape[1]
    o_ref[0, 0] = g[:, :wdt].astype(BF16)
    o_ref[0, 1] = g[:, wdt:].astype(BF16)


def _fourier_b_kernel(m_ref, g_ref, o_ref):
    o_ref[0] = _dot(m_ref[...], g_ref[0]).astype(BF16)


def _fourier(f2d, bsz, seq_len, row_off, n, wc, dft):
    per_seq = seq_len // ROW_BLK
    off_blk = row_off // ROW_BLK
    nb = n // ROW_BLK
    width = f2d.shape[1]
    g = pl.pallas_call(
        _fourier_a_kernel,
        grid=(bsz, nb),
        in_specs=[
            pl.BlockSpec((ROW_BLK, width), lambda b, t: (b * per_seq + off_blk + t, 0)),
            pl.BlockSpec((width, 2 * width), lambda b, t: (0, 0)),
        ],
        out_specs=pl.BlockSpec((1, 2, ROW_BLK, width), lambda b, t: (b, 0, t, 0)),
        out_shape=jax.ShapeDtypeStruct((bsz, 2, n, width), BF16),
        compiler_params=_cparams(("parallel", "parallel"), 32),
        name="fourier_chan",
    )(f2d, wc)
    g = g.reshape(bsz, 2 * n, width)
    return pl.pallas_call(
        _fourier_b_kernel,
        grid=(nb, bsz),
        in_specs=[
            pl.BlockSpec((ROW_BLK, 2 * n), lambda i, b: (i, 0)),
            pl.BlockSpec((1, 2 * n, width), lambda i, b: (b, 0, 0)),
        ],
        out_specs=pl.BlockSpec((1, ROW_BLK, width), lambda i, b: (b, i, 0)),
        out_shape=jax.ShapeDtypeStruct((bsz, n, width), BF16),
        compiler_params=_cparams(("parallel", "parallel"), 48),
        name="fourier_pos",
    )(dft, g)


def _dft_mats(n):
    i = jnp.arange(n, dtype=jnp.int32)
    prod = (i[:, None] * i[None, :]) % n
    ang = prod.astype(F32) * (2.0 * math.pi / n)
    s = n ** -0.5
    return jnp.concatenate([jnp.cos(ang) * s, jnp.sin(ang) * s], axis=1).astype(BF16)


def _chan_mats():
    i = jnp.arange(HEAD_DIM, dtype=jnp.int32)
    ang = ((i[:, None] * i[None, :]) % HEAD_DIM).astype(F32) * (2.0 * math.pi / HEAD_DIM)
    s = HEAD_DIM ** -0.5
    eye = jnp.eye(FOURIER_GROUPS, dtype=F32)
    c = jnp.kron(eye, jnp.cos(ang) * s)
    sn = jnp.kron(eye, -jnp.sin(ang) * s)
    return jnp.concatenate([c, sn], axis=1).astype(BF16)


def _ret_kernel(ld_ref, q_ref, k_ref, v_ref, o_ref, state_ref, *, n_lat_chunks, n_ctx_chunks):
    h = pl.program_id(1)
    d = pl.program_id(2)
    fwd = d == 0
    c = RET_CHUNK
    lg = ld_ref[d, h]
    ii = lax.broadcasted_iota(jnp.int32, (c, c), 0).astype(F32)
    jj = lax.broadcasted_iota(jnp.int32, (c, c), 1).astype(F32)
    diff = jnp.where(fwd, ii - jj, jj - ii)
    intra = jnp.where(diff >= 0, jnp.exp(lg * jnp.maximum(diff, 0.0)), 0.0)
    ri = lax.broadcasted_iota(jnp.int32, (c, 1), 0).astype(F32)
    q_dec = jnp.exp(lg * jnp.where(fwd, ri + 1.0, c - ri))
    k_dec = jnp.exp(lg * jnp.where(fwd, c - 1.0 - ri, ri))
    c_dec = jnp.exp(jnp.full((1, HEAD_DIM), lg * c, F32))
    state_ref[...] = jnp.zeros_like(state_ref)
    n_all = n_lat_chunks + n_ctx_chunks

    def body(s, _):
        in_ctx = s < n_ctx_chunks
        ctx_idx = jnp.where(fwd, s, n_ctx_chunks - 1 - s)
        lat_idx = jnp.where(fwd, s - n_ctx_chunks, n_all - 1 - s)
        chunk = jnp.where(in_ctx, n_lat_chunks + ctx_idx, lat_idx)
        off = pl.multiple_of(chunk * c, c)
        q = q_ref[pl.ds(off, c), :]
        k = k_ref[pl.ds(off, c), :]
        v = v_ref[pl.ds(off, c), :]
        st = state_ref[...]
        sc = _dot_nt(q, k) * intra
        o = _dot(sc.astype(BF16), v) + _dot(q, st.astype(BF16)) * q_dec
        kd = (k.astype(F32) * k_dec).astype(BF16)
        state_ref[...] = st * c_dec + _dot_tn(kd, v)
        mu = jnp.mean(o, axis=-1, keepdims=True)
        var = jnp.mean(jnp.square(o - mu), axis=-1, keepdims=True)
        o_ref[pl.ds(off, c), :] = (o - mu) * lax.rsqrt(var + EPS)
        return 0

    lax.fori_loop(0, n_all, body, 0)


def _retention(rq, rk, rv, log_decay, bsz, seq_len, n_lat):
    return pl.pallas_call(
        functools.partial(_ret_kernel, n_lat_chunks=n_lat // RET_CHUNK,
                          n_ctx_chunks=(seq_len - n_lat) // RET_CHUNK),
        grid=(bsz, RET_HEADS, 2),
        in_specs=[
            pl.BlockSpec(memory_space=pltpu.SMEM),
            pl.BlockSpec((seq_len, HEAD_DIM), lambda b, h, d: (b, d * RET_HEADS + h)),
            pl.BlockSpec((seq_len, HEAD_DIM), lambda b, h, d: (b, d * RET_HEADS + h)),
            pl.BlockSpec((seq_len, HEAD_DIM), lambda b, h, d: (b, h)),
        ],
        out_specs=pl.BlockSpec((seq_len, HEAD_DIM), lambda b, h, d: (b, d * RET_HEADS + h)),
        out_shape=jax.ShapeDtypeStruct((bsz * seq_len, 2 * RET_WIDTH), F32),
        scratch_shapes=[pltpu.VMEM((HEAD_DIM, HEAD_DIM), F32)],
        compiler_params=_cparams(("parallel", "parallel", "parallel"), 32),
        name="retention",
    )(log_decay, rq, rk, rv)


def _outproj_kernel(a_ref, fl_ref, fc_ref, on_ref, sg_ref, w_ref, x_ref, m_ref, g_ref, wr_ref,
                    xo_ref, h_ref, aff_ref, *, n_lat_blk):
    is_ctx = pl.program_id(1) >= n_lat_blk
    fm = jnp.where(is_ctx, fc_ref[0], fl_ref[0])
    rw = RET_WIDTH
    r = (sg_ref[:, :rw].astype(F32) * on_ref[:, :rw]
         + sg_ref[:, rw:].astype(F32) * on_ref[:, rw:]).astype(BF16)
    a0, f0 = ATTN_WIDTH, ATTN_WIDTH + FOURIER_WIDTH
    y = (_dot(a_ref[...], w_ref[:a0, :]) + _dot(fm, w_ref[a0:f0, :])) + _dot(r, w_ref[f0:, :])
    x = x_ref[0] + m_ref[0, 2:3, :] * y
    xo_ref[0] = x
    h = _rms_mod(x, g_ref[...], m_ref[0, 3:4, :], m_ref[0, 4:5, :])
    h_ref[0] = h.astype(BF16)
    hh, hl = _split(h)
    wh, wl = _split(wr_ref[...])
    lt = _dot_nt(wh, hh) + (_dot_nt(wh, hl) + _dot_nt(wl, hh))
    e = jnp.exp(lt - jnp.max(lt, axis=0, keepdims=True))
    aff_ref[...] = e / jnp.sum(e, axis=0, keepdims=True)


def _outproj(a, fm_lat, fm_ctx, on, sg, w_out, x, mods, g, wr_t, n_lat_blk):
    bsz, l, d = x.shape
    per_seq = l // ROW_BLK
    n_exp = wr_t.shape[0]
    fw = fm_lat.shape[-1]

    def flat(b, t):
        return (b * per_seq + t, 0)

    return pl.pallas_call(
        functools.partial(_outproj_kernel, n_lat_blk=n_lat_blk),
        grid=(bsz, per_seq),
        in_specs=[
            pl.BlockSpec((ROW_BLK, a.shape[1]), flat),
            pl.BlockSpec((1, ROW_BLK, fw), lambda b, t: (b, jnp.minimum(t, n_lat_blk - 1), 0)),
            pl.BlockSpec((1, ROW_BLK, fw), lambda b, t: (b, 0, 0)),
            pl.BlockSpec((ROW_BLK, on.shape[1]), flat),
            pl.BlockSpec((ROW_BLK, sg.shape[1]), flat),
            pl.BlockSpec(w_out.shape, lambda b, t: (0, 0)),
            pl.BlockSpec((1, ROW_BLK, d), lambda b, t: (b, t, 0)),
            pl.BlockSpec((1, 6, d), lambda b, t: (jnp.where(t >= n_lat_blk, bsz, b), 0, 0)),
            pl.BlockSpec((1, d), lambda b, t: (0, 0)),
            pl.BlockSpec(wr_t.shape, lambda b, t: (0, 0)),
        ],
        out_specs=[
            pl.BlockSpec((1, ROW_BLK, d), lambda b, t: (b, t, 0)),
            pl.BlockSpec((1, ROW_BLK, d), lambda b, t: (b, t, 0)),
            pl.BlockSpec((n_exp, ROW_BLK), lambda b, t: (0, b * per_seq + t)),
        ],
        out_shape=[
            jax.ShapeDtypeStruct((bsz, l, d), F32),
            jax.ShapeDtypeStruct((bsz, l, d), BF16),
            jax.ShapeDtypeStruct((n_exp, bsz * l), F32),
        ],
        compiler_params=_cparams(("parallel", "parallel"), 48),
        name="outproj",
    )(a, fm_lat, fm_ctx, on, sg, w_out, x, mods, g.reshape(1, d), wr_t)


def _cumsum_lanes(m, out_ref, fin):
    n_exp, n = m.shape
    tri = (lax.broadcasted_iota(jnp.int32, (LANES, LANES), 0)
           <= lax.broadcasted_iota(jnp.int32, (LANES, LANES), 1)).astype(BF16)
    run = jnp.zeros((n_exp, 1), F32)
    befores = []
    for k in range(n // LANES):
        befores.append(run)
        sl = slice(k * LANES, (k + 1) * LANES)
        cnt = _dot(m[:, sl].astype(BF16), tri) + run
        out_ref[0, :, sl] = fin(cnt, sl)
        run = cnt[:, LANES - 1:LANES]
    befores.append(run)
    return befores


def _select(seg, cap, slot_off, sel_ref, tmp_ref):
    bits = pltpu.bitcast(seg, jnp.int32)
    n_exp = seg.shape[0]

    def body(it, t):
        tt = t | lax.shift_left(jnp.int32(1), 30 - it)
        cnt = jnp.sum(jnp.where(bits >= tt, 1.0, 0.0), axis=1, keepdims=True)
        return jnp.where(cnt >= cap, tt, t)

    t = lax.fori_loop(0, 31, body, jnp.zeros((n_exp, 1), jnp.int32))
    gt = bits > t
    eq = bits == t
    need = cap - jnp.sum(jnp.where(gt, 1.0, 0.0), axis=1, keepdims=True)
    eqf = jnp.where(eq, 1.0, 0.0)
    _cumsum_lanes(eqf, tmp_ref, lambda cnt, sl: cnt)
    take = eq & (tmp_ref[0] - eqf < need)
    mask = gt | take
    maskf = jnp.where(mask, 1.0, 0.0)
    return _cumsum_lanes(
        maskf, sel_ref,
        lambda cnt, sl: jnp.where(maskf[:, sl] > 0.5, cnt - 1.0 + slot_off, -1.0))


def _routing_kernel(aff_ref, sel_l, gat_l, sel_c, gat_c, st_l, tmp_l, tmp_c, *, n_lat, cap_l, cap_c):
    b = pl.program_id(0)
    a = aff_ref[...]
    lat = a[:, :n_lat]
    ctx = a[:, n_lat:]
    gat_l[0] = lat
    gat_c[0] = ctx
    befores = _select(lat, float(cap_l), 0.0, sel_l, tmp_l)
    _select(ctx, float(cap_c), (b * cap_c).astype(F32), sel_c, tmp_c)
    lane = lax.broadcasted_iota(jnp.int32, (a.shape[0], LANES), 1)
    st = jnp.zeros((a.shape[0], LANES), F32)
    per_blk = ROW_BLK // LANES
    for j in range(n_lat // ROW_BLK + 1):
        st = jnp.where(lane == j, befores[j * per_blk], st)
    st_l[0] = st


def _routing(aff_t, bsz, seq_len, n_lat, cap_l, cap_c):
    n_exp = aff_t.shape[0]
    n_ctx = seq_len - n_lat
    shp = lambda n: jax.ShapeDtypeStruct((bsz, n_exp, n), F32)
    spec = lambda n: pl.BlockSpec((1, n_exp, n), lambda b: (b, 0, 0))
    return pl.pallas_call(
        functools.partial(_routing_kernel, n_lat=n_lat, cap_l=cap_l, cap_c=cap_c),
        grid=(bsz,),
        in_specs=[pl.BlockSpec((n_exp, seq_len), lambda b: (0, b))],
        out_specs=[spec(n_lat), spec(n_lat), spec(n_ctx), spec(n_ctx), spec(LANES)],
        out_shape=[shp(n_lat), shp(n_lat), shp(n_ctx), shp(n_ctx), shp(LANES)],
        scratch_shapes=[pltpu.VMEM((1, n_exp, n_lat), F32), pltpu.VMEM((1, n_exp, n_ctx), F32)],
        compiler_params=_cparams(("parallel",), 32),
        name="routing",
    )(aff_t)


def _active(starts_ref, base, j, rb, sb):
    s0 = starts_ref[base + j]
    s1 = starts_ref[base + j + 1]
    return (s1 > s0) & (s0 < (rb + 1) * sb) & (s1 > rb * sb)


def _onehot(sel_row, rb, sb):
    r = lax.broadcasted_iota(jnp.int32, (sb, sel_row.shape[1]), 0).astype(F32) + float(rb * sb)
    return sel_row == r


def _gather_kernel(starts_ref, h_ref, sel_ref, aff_ref, xg_ref, g_ref, acc_ref, gacc_ref,
                   *, tb, sb, n_tb, n_sb):
    n_exp = pl.num_programs(1)
    base = (pl.program_id(0) * n_exp + pl.program_id(1)) * (n_tb + 1)
    acc_ref[...] = jnp.zeros_like(acc_ref)
    gacc_ref[...] = jnp.zeros_like(gacc_ref)
    for j in range(n_tb):
        for rb in range(n_sb):
            @pl.when(_active(starts_ref, base, j, rb, sb))
            def _(j=j, rb=rb):
                tok = slice(j * tb, (j + 1) * tb)
                slots = slice(rb * sb, (rb + 1) * sb)
                p = _onehot(sel_ref[0, 0, :, tok], rb, sb)
                acc_ref[slots, :] += _dot(jnp.where(p, 1.0, 0.0).astype(BF16), h_ref[0, tok, :])
                gacc_ref[slots, :] += jnp.sum(jnp.where(p, aff_ref[0, 0, :, tok], 0.0), axis=1,
                                              keepdims=True)
    xg_ref[0] = acc_ref[...].astype(BF16)
    g_ref[0] = gacc_ref[...]


def _moe_gather(starts, h, sel, aff, n_tok, cap, tb, sb):
    nb, _, d = h.shape
    n_exp = sel.shape[1]
    n_tb, n_sb = n_tok // tb, cap // sb
    row = pl.BlockSpec((1, 1, 1, n_tok), lambda v, e, s: (v, e, 0, 0))
    return pl.pallas_call(
        functools.partial(_gather_kernel, tb=tb, sb=sb, n_tb=n_tb, n_sb=n_sb),
        grid_spec=pltpu.PrefetchScalarGridSpec(
            num_scalar_prefetch=1,
            grid=(nb, n_exp),
            in_specs=[pl.BlockSpec((1, n_tok, d), lambda v, e, s: (v, 0, 0)), row, row],
            out_specs=[
                pl.BlockSpec((1, cap, d), lambda v, e, s: (e, v, 0)),
                pl.BlockSpec((1, cap, 1), lambda v, e, s: (e, v, 0)),
            ],
            scratch_shapes=[pltpu.VMEM((cap, d), F32), pltpu.VMEM((cap, 1), F32)],
        ),
        out_shape=[
            jax.ShapeDtypeStruct((n_exp, nb * cap, d), BF16),
            jax.ShapeDtypeStruct((n_exp, nb * cap, 1), F32),
        ],
        compiler_params=_cparams(("parallel", "arbitrary"), 56),
        name="moe_gather",
    )(starts, h, sel, aff)


def _ffn_kernel(xg_ref, g_ref, wg_ref, wu_ref, act_ref, wgb_ref, wub_ref):
    @pl.when(pl.program_id(2) == 0)
    def _():
        wgb_ref[...] = wg_ref[0].astype(BF16)
        wub_ref[...] = wu_ref[0].astype(BF16)

    x = xg_ref[0]
    a = _dot(x, wgb_ref[...])
    u = _dot(x, wub_ref[...])
    act_ref[0] = (_silu(a) * u * g_ref[0]).astype(BF16)


def _moe_ffn(xg, gate, w_gate, w_up):
    n_exp, rows, d = xg.shape
    ff = w_gate.shape[2]
    fh = _largest_tile(ff, 512, LANES)
    rblk = _largest_tile(rows, 512)
    return pl.pallas_call(
        _ffn_kernel,
        grid=(n_exp, ff // fh, rows // rblk),
        in_specs=[
            pl.BlockSpec((1, rblk, d), lambda e, f, r: (e, r, 0)),
            pl.BlockSpec((1, rblk, 1), lambda e, f, r: (e, r, 0)),
            pl.BlockSpec((1, d, fh), lambda e, f, r: (e, 0, f)),
            pl.BlockSpec((1, d, fh), lambda e, f, r: (e, 0, f)),
        ],
        out_specs=pl.BlockSpec((1, rblk, fh), lambda e, f, r: (e, r, f)),
        out_shape=jax.ShapeDtypeStruct((n_exp, rows, ff), BF16),
        scratch_shapes=[pltpu.VMEM((d, fh), BF16), pltpu.VMEM((d, fh), BF16)],
        compiler_params=_cparams(("parallel", "parallel", "arbitrary"), 48),
        name="moe_ffn",
    )(xg, gate, w_gate, w_up)


def _scatter_kernel(starts_ref, act_ref, wd_ref, sel_ref, out_ref, *, tb, sb, n_tb, n_sb):
    n_exp = pl.num_programs(2)
    e = pl.program_id(2)
    base = (pl.program_id(0) * n_exp + e) * (n_tb + 1)

    @pl.when(e == 0)
    def _():
        out_ref[...] = jnp.zeros_like(out_ref)

    y = _dot(act_ref[0], wd_ref[0].astype(BF16)).astype(BF16)
    for j in range(n_tb):
        for rb in range(n_sb):
            @pl.when(_active(starts_ref, base, j, rb, sb))
            def _(j=j, rb=rb):
                tok = slice(j * tb, (j + 1) * tb)
                p = _onehot(sel_ref[0, 0, :, tok], rb, sb)
                out_ref[0, tok, :] += _dot_tn(jnp.where(p, 1.0, 0.0).astype(BF16),
                                              y[rb * sb:(rb + 1) * sb, :])


def _moe_scatter(starts, act, w_down, sel, n_tok, cap, tb, sb, out_rows):
    n_exp, _, ff = act.shape
    nb = sel.shape[0]
    d = w_down.shape[2]
    pw = _largest_tile(d, 512, LANES)
    n_tb, n_sb = n_tok // tb, cap // sb
    return pl.pallas_call(
        functools.partial(_scatter_kernel, tb=tb, sb=sb, n_tb=n_tb, n_sb=n_sb),
        grid_spec=pltpu.PrefetchScalarGridSpec(
            num_scalar_prefetch=1,
            grid=(nb, d // pw, n_exp),
            in_specs=[
                pl.BlockSpec((1, cap, ff), lambda v, p, e, s: (e, v, 0)),
                pl.BlockSpec((1, ff, pw), lambda v, p, e, s: (e, 0, p)),
                pl.BlockSpec((1, 1, 1, n_tok), lambda v, p, e, s: (v, e, 0, 0)),
            ],
            out_specs=pl.BlockSpec((1, n_tok, pw), lambda v, p, e, s: (v, 0, p)),
        ),
        out_shape=jax.ShapeDtypeStruct((nb, out_rows, d), F32),
        compiler_params=_cparams(("parallel", "parallel", "arbitrary"), 48),
        name="moe_scatter",
    )(starts, act, w_down, sel)


def _rope_tables(pos_groups, half):
    freqs = ROPE_BASE ** (-jnp.arange(half, dtype=F32) / half)
    cos, sa, sb = [], [], []
    for pos in pos_groups:
        ang = pos.astype(F32)[:, None] * freqs[None, :]
        c, s = jnp.cos(ang), jnp.sin(ang)
        z = jnp.zeros_like(s)
        cos += [c, c]
        sa += [-s, z]
        sb += [z, s]
    return jnp.stack([jnp.concatenate(t, axis=1) for t in (cos, sa, sb)])


def _attn_tables(n_lat, n_ctx):
    s = jnp.arange(n_lat)
    tab = _rope_tables([s // GRID_W, s % GRID_W], HEAD_DIM // 4)
    ident = jnp.stack([jnp.ones((n_ctx, HEAD_DIM), F32), jnp.zeros((n_ctx, HEAD_DIM), F32),
                       jnp.zeros((n_ctx, HEAD_DIM), F32)])
    return jnp.concatenate([tab, ident], axis=1)


def _ret_tables(n_lat, n_ctx):
    s = jnp.arange(n_lat)
    t = jnp.arange(n_ctx)
    fwd = jnp.concatenate([n_ctx + s, t])
    bwd = jnp.concatenate([n_ctx + (n_lat - 1 - s), n_ctx - 1 - t])
    return jnp.concatenate([_rope_tables([fwd], HEAD_DIM // 2),
                            _rope_tables([bwd], HEAD_DIM // 2)], axis=0)


def kernel(x, c, ctx, c_ctx, w_ada, b_ada, norm_mix, norm_ffn, w_in, q_norm, k_norm, ret_log_decay,
           w_out, w_router, w_gate, w_up, w_down):
    bsz, n_lat, d = x.shape
    n_ctx = ctx.shape[1]
    seq_len = n_lat + n_ctx
    depth = w_ada.shape[0]
    n_exp = w_router.shape[2]
    n_lat_blk = n_lat // ROW_BLK
    assert n_lat % ROW_BLK == 0 and n_ctx == ROW_BLK and bsz + 1 <= 8
    cap_l = CAPACITY_FACTOR * n_lat // n_exp
    cap_c = CAPACITY_FACTOR * n_ctx // n_exp

    cond8 = jnp.zeros((8, d), F32).at[:bsz].set(c).at[bsz].set(c_ctx)
    mods = _adaln(cond8, w_ada, b_ada).reshape(depth, 8, 6, d)

    attn_tab = _attn_tables(n_lat, n_ctx)
    ret_tab = _ret_tables(n_lat, n_ctx)
    wc = _chan_mats()
    dft_l = _dft_mats(n_lat)
    dft_c = _dft_mats(n_ctx)

    o_q, o_k = 0, ATTN_WIDTH
    o_v = o_k + KV_WIDTH
    o_f = o_v + KV_WIDTH
    o_rq = o_f + FOURIER_WIDTH
    o_rk = o_rq + RET_WIDTH
    o_rv = o_rk + RET_WIDTH
    o_g = o_rv + RET_WIDTH
    o_end = o_g + 2 * RET_WIDTH
    q_scale = HEAD_DIM ** -0.5 * math.log2(math.e)

    sb_l = min(cap_l, ROW_BLK)
    sb_c = bsz * cap_c
    starts_c = jnp.tile(jnp.arange(bsz + 1, dtype=jnp.int32) * cap_c, n_exp)

    xs = jnp.concatenate([x, ctx], axis=1)
    delta_lat = delta_ctx = None
    for l in range(depth):
        xs, h = _norm(xs, mods[l], norm_mix[l], n_lat_blk, delta_lat, delta_ctx,
                      mods[l - 1] if l else None)
        h2d = h.reshape(bsz * seq_len, d)
        wl = w_in[l].astype(BF16)
        q = _proj(h2d, wl[:, o_q:o_k], seq_len, "qk", gain=q_norm[l], tab=attn_tab,
                  n_heads=ATTN_HEADS, scale=q_scale, half=HEAD_DIM // 4)
        k = _proj(h2d, wl[:, o_k:o_v], seq_len, "qk", gain=k_norm[l], tab=attn_tab,
                  n_heads=ATTN_KV_HEADS, scale=1.0, half=HEAD_DIM // 4)
        v, f = _proj(h2d, wl[:, o_v:o_rq], seq_len, "split", split_at=KV_WIDTH)
        rq = _proj(h2d, wl[:, o_rq:o_rk], seq_len, "ret", tab=ret_tab, n_heads=RET_HEADS,
                   scale=1.0, half=HEAD_DIM // 2)
        rk = _proj(h2d, wl[:, o_rk:o_rv], seq_len, "ret", tab=ret_tab, n_heads=RET_HEADS,
                   scale=HEAD_DIM ** -0.5, half=HEAD_DIM // 2)
        rv = _proj(h2d, wl[:, o_rv:o_g], seq_len, "plain")
        sg = _proj(h2d, wl[:, o_g:o_end], seq_len, "silu")

        a = _attention(q, k, v, bsz, seq_len, n_lat)
        fm_lat = _fourier(f, bsz, seq_len, 0, n_lat, wc, dft_l)
        fm_ctx = _fourier(f, bsz, seq_len, n_lat, n_ctx, wc, dft_c)
        on = _retention(rq, rk, rv, ret_log_decay[l], bsz, seq_len, n_lat)

        xs, h2, aff_t = _outproj(a, fm_lat, fm_ctx, on, sg, w_out[l].astype(BF16), xs, mods[l],
                                 norm_ffn[l], w_router[l].T, n_lat_blk)

        sel_l, gat_l, sel_c, gat_c, st_l = _routing(aff_t, bsz, seq_len, n_lat, cap_l, cap_c)
        starts_l = st_l[:, :, :n_lat_blk + 1].astype(jnp.int32).reshape(-1)
        row4 = lambda t: t.reshape(t.shape[0], n_exp, 1, t.shape[2])
        merge = lambda t: t.transpose(1, 0, 2).reshape(1, n_exp, 1, bsz * n_ctx)
        sel_l4, gat_l4 = row4(sel_l), row4(gat_l)
        sel_c4, gat_c4 = merge(sel_c), merge(gat_c)
        h_ctx = h2[:, n_lat:, :].reshape(1, bsz * n_ctx, d)

        xg_l, g_l = _moe_gather(starts_l, h2, sel_l4, gat_l4, n_lat, cap_l, ROW_BLK, sb_l)
        xg_c, g_c = _moe_gather(starts_c, h_ctx, sel_c4, gat_c4, bsz * n_ctx, sb_c, ROW_BLK, sb_c)
        act_l = _moe_ffn(xg_l, g_l, w_gate[l], w_up[l])
        act_c = _moe_ffn(xg_c, g_c, w_gate[l], w_up[l])
        delta_lat = _moe_scatter(starts_l, act_l, w_down[l], sel_l4, n_lat, cap_l, ROW_BLK, sb_l,
                                 n_lat)
        delta_ctx = _moe_scatter(starts_c, act_c, w_down[l], sel_c4, bsz * n_ctx, sb_c, ROW_BLK,
                                 sb_c, bsz * n_ctx).reshape(bsz * n_ctx, d)

    out, _ = _norm(xs, None, None, n_lat_blk, delta_lat, delta_ctx, mods[depth - 1],
                   want_h=False, lat_only=True)
    return out
```

```python
import functools
import math

import jax
import jax.numpy as jnp
from jax import lax
from jax.experimental import pallas as pl
from jax.experimental.pallas import tpu as pltpu

HEAD_DIM = 128
ATTN_HEADS = 8
ATTN_KV_HEADS = 2
ATTN_GROUP = ATTN_HEADS // ATTN_KV_HEADS
FOURIER_GROUPS = 4
RET_HEADS = 4
RET_CHUNK = 128
GRID_W = 64
ROPE_BASE = 10000.0
EPS = 1e-6
CAPACITY_FACTOR = 2

ATTN_WIDTH = ATTN_HEADS * HEAD_DIM
KV_WIDTH = ATTN_KV_HEADS * HEAD_DIM
FOURIER_WIDTH = FOURIER_GROUPS * HEAD_DIM
RET_WIDTH = RET_HEADS * HEAD_DIM

ROW_BLK = 256
LANES = 128
MIB = 1024 * 1024

F32 = jnp.float32
BF16 = jnp.bfloat16


def _cparams(sem, vmem_mib):
    return pltpu.CompilerParams(dimension_semantics=sem, vmem_limit_bytes=vmem_mib * MIB)


def _largest_tile(n, cap, mult=8):
    best = mult
    for t in range(mult, min(n, cap) + 1, mult):
        if n % t == 0:
            best = t
    return best


def _split(a):
    hi = a.astype(BF16)
    lo = (a - hi.astype(F32)).astype(BF16)
    return hi, lo


def _dot(a, b):
    return jnp.dot(a, b, preferred_element_type=F32)


def _dot_nt(a, b):
    return lax.dot_general(a, b, (((1,), (1,)), ((), ())), preferred_element_type=F32)


def _dot_tn(a, b):
    return lax.dot_general(a, b, (((0,), (0,)), ((), ())), preferred_element_type=F32)


def _silu(a):
    return a / (1.0 + jnp.exp(-a))


def _adaln_kernel(c_ref, w_ref, b_ref, o_ref):
    s = _silu(c_ref[...])
    sh, sl = _split(s)
    wh, wl = _split(w_ref[0])
    o_ref[0] = _dot(sh, wh) + (_dot(sh, wl) + _dot(sl, wh)) + b_ref[0]


def _adaln(cond8, w_ada, b_ada):
    depth, d, n = w_ada.shape
    tn = _largest_tile(n, 768, LANES)
    return pl.pallas_call(
        _adaln_kernel,
        grid=(depth, n // tn),
        in_specs=[
            pl.BlockSpec((8, d), lambda l, j: (0, 0)),
            pl.BlockSpec((1, d, tn), lambda l, j: (l, 0, j)),
            pl.BlockSpec((1, 1, tn), lambda l, j: (l, 0, j)),
        ],
        out_specs=pl.BlockSpec((1, 8, tn), lambda l, j: (l, 0, j)),
        out_shape=jax.ShapeDtypeStruct((depth, 8, n), F32),
        compiler_params=_cparams(("parallel", "parallel"), 48),
        name="adaln",
    )(cond8, w_ada, b_ada.reshape(depth, 1, n))


def _rms_mod(x, g, shift, scale):
    y = x * lax.rsqrt(jnp.mean(x * x, axis=-1, keepdims=True) + EPS)
    return (y * g) * (1.0 + scale) + shift


def _norm_kernel(*refs, has_delta, want_h, n_lat_blk):
    refs = list(refs)
    x_ref = refs.pop(0)
    x = x_ref[0]
    if has_delta:
        dl_ref, dc_ref, pm_ref = refs.pop(0), refs.pop(0), refs.pop(0)
        is_ctx = pl.program_id(1) >= n_lat_blk
        delta = jnp.where(is_ctx, dc_ref[...], dl_ref[0])
        x = x + pm_ref[0, 5:6, :] * delta
    if want_h:
        m_ref, g_ref = refs.pop(0), refs.pop(0)
    if has_delta:
        xo_ref = refs.pop(0)
        xo_ref[0] = x
    if want_h:
        h_ref = refs.pop(0)
        h_ref[0] = _rms_mod(x, g_ref[...], m_ref[0, 0:1, :], m_ref[0, 1:2, :]).astype(BF16)


def _norm(x, mods, g, n_lat_blk, delta_lat=None, delta_ctx=None, prev_mods=None,
          want_h=True, lat_only=False):
    bsz, l, d = x.shape
    n_blk = n_lat_blk if lat_only else l // ROW_BLK
    has_delta = delta_lat is not None

    def mod_row(b, t):
        return (jnp.where(t >= n_lat_blk, bsz, b), 0, 0)

    xspec = pl.BlockSpec((1, ROW_BLK, d), lambda b, t: (b, t, 0))
    in_specs, args = [xspec], [x]
    if has_delta:
        in_specs += [
            pl.BlockSpec((1, ROW_BLK, d), lambda b, t: (b, jnp.minimum(t, n_lat_blk - 1), 0)),
            pl.BlockSpec((ROW_BLK, d), lambda b, t: (b, 0)),
            pl.BlockSpec((1, 6, d), mod_row),
        ]
        args += [delta_lat, delta_ctx, prev_mods]
    if want_h:
        in_specs += [pl.BlockSpec((1, 6, d), mod_row), pl.BlockSpec((1, d), lambda b, t: (0, 0))]
        args += [mods, g.reshape(1, d)]
    out_specs, out_shape = [], []
    rows = n_blk * ROW_BLK
    if has_delta:
        out_specs.append(xspec)
        out_shape.append(jax.ShapeDtypeStruct((bsz, rows, d), F32))
    if want_h:
        out_specs.append(xspec)
        out_shape.append(jax.ShapeDtypeStruct((bsz, rows, d), BF16))
    outs = pl.pallas_call(
        functools.partial(_norm_kernel, has_delta=has_delta, want_h=want_h, n_lat_blk=n_lat_blk),
        grid=(bsz, n_blk),
        in_specs=in_specs,
        out_specs=out_specs,
        out_shape=out_shape,
        compiler_params=_cparams(("parallel", "parallel"), 32),
        name="norm",
    )(*args)
    outs = list(outs)
    x_new = outs.pop(0) if has_delta else x
    h = outs.pop(0) if want_h else None
    return x_new, h


def _rope(z, tab_ref, k, half):
    return (z * tab_ref[k] + pltpu.roll(z, LANES - half, 1) * tab_ref[k + 1]
            + pltpu.roll(z, half, 1) * tab_ref[k + 2])


def _proj_kernel(*refs, flavor, n_heads, scale, half, split_at):
    h_ref, w_ref = refs[0], refs[1]
    acc = _dot(h_ref[...], w_ref[...])
    if flavor == "plain":
        refs[2][...] = acc.astype(BF16)
    elif flavor == "split":
        refs[2][...] = acc[:, :split_at].astype(BF16)
        refs[3][...] = acc[:, split_at:].astype(BF16)
    elif flavor == "silu":
        refs[2][...] = _silu(acc).astype(BF16)
    elif flavor == "qk":
        gain_ref, tab_ref, o_ref = refs[2], refs[3], refs[4]
        for hd in range(n_heads):
            sl = slice(hd * HEAD_DIM, (hd + 1) * HEAD_DIM)
            z = acc[:, sl]
            z = (z * lax.rsqrt(jnp.mean(z * z, axis=-1, keepdims=True) + EPS)) * gain_ref[...]
            o_ref[:, sl] = (_rope(z, tab_ref, 0, half) * scale).astype(BF16)
    elif flavor == "ret":
        tab_ref, o_ref = refs[2], refs[3]
        width = n_heads * HEAD_DIM
        for hd in range(n_heads):
            sl = slice(hd * HEAD_DIM, (hd + 1) * HEAD_DIM)
            z = acc[:, sl] * scale
            o_ref[:, sl] = _rope(z, tab_ref, 0, half).astype(BF16)
            o_ref[:, width + hd * HEAD_DIM: width + (hd + 1) * HEAD_DIM] = (
                _rope(z, tab_ref, 3, half).astype(BF16))
    else:
        raise ValueError(flavor)


def _proj(h2d, w, seq_len, flavor, *, gain=None, tab=None, n_heads=0, scale=1.0, half=0,
          split_at=0):
    t_rows, d = h2d.shape
    n = w.shape[1]
    tm = _largest_tile(seq_len, 1088)
    per_seq = seq_len // tm
    in_specs = [pl.BlockSpec((tm, d), lambda i: (i, 0)), pl.BlockSpec((d, n), lambda i: (0, 0))]
    args = [h2d, w]
    if flavor == "qk":
        in_specs.append(pl.BlockSpec((1, HEAD_DIM), lambda i: (0, 0)))
        args.append(gain.reshape(1, HEAD_DIM))
    if flavor in ("qk", "ret"):
        ntab = tab.shape[0]
        in_specs.append(pl.BlockSpec((ntab, tm, HEAD_DIM), lambda i: (0, i % per_seq, 0)))
        args.append(tab)
    if flavor == "split":
        widths = [split_at, n - split_at]
    elif flavor == "ret":
        widths = [2 * n]
    else:
        widths = [n]
    out_specs = [pl.BlockSpec((tm, wd), lambda i: (i, 0)) for wd in widths]
    out_shape = [jax.ShapeDtypeStruct((t_rows, wd), BF16) for wd in widths]
    outs = pl.pallas_call(
        functools.partial(_proj_kernel, flavor=flavor, n_heads=n_heads, scale=scale, half=half,
                          split_at=split_at),
        grid=(t_rows // tm,),
        in_specs=in_specs,
        out_specs=out_specs,
        out_shape=out_shape,
        compiler_params=_cparams(("parallel",), 48),
        name="proj_" + flavor,
    )(*args)
    return outs if len(outs) > 1 else outs[0]


def _attn_head(q, k_ref, v_ref, s_ref, p_ref, k_lo, n_keys, tk):
    s_ref[:, :n_keys] = _dot_nt(q, k_ref[k_lo:k_lo + n_keys, :])
    n_chunks = n_keys // tk
    fold = lambda t, op: functools.reduce(op, [t[:, i * LANES:(i + 1) * LANES]
                                               for i in range(tk // LANES)])
    mp = None
    for c in range(n_chunks):
        part = fold(s_ref[:, c * tk:(c + 1) * tk], jnp.maximum)
        mp = part if mp is None else jnp.maximum(mp, part)
    m = jnp.max(mp, axis=-1, keepdims=True)
    lp = None
    for c in range(n_chunks):
        p = jnp.exp2(s_ref[:, c * tk:(c + 1) * tk] - m)
        part = fold(p, jnp.add)
        lp = part if lp is None else lp + part
        p_ref[:, c * tk:(c + 1) * tk] = p.astype(BF16)
    l = jnp.sum(lp, axis=-1, keepdims=True)
    return _dot(p_ref[:, :n_keys], v_ref[k_lo:k_lo + n_keys, :]) / l


def _attn_kernel(q_ref, k_ref, v_ref, o_ref, s_ref, p_ref, *, tk, n_lat, n_ctx, n_lat_qblk):
    is_ctx = pl.program_id(2) >= n_lat_qblk

    def run(k_lo, n_keys):
        for hd in range(ATTN_GROUP):
            sl = slice(hd * HEAD_DIM, (hd + 1) * HEAD_DIM)
            o = _attn_head(q_ref[:, sl], k_ref, v_ref, s_ref.at[hd], p_ref.at[hd], k_lo, n_keys, tk)
            o_ref[:, sl] = o.astype(BF16)

    @pl.when(jnp.logical_not(is_ctx))
    def _():
        run(0, n_lat + n_ctx)

    @pl.when(is_ctx)
    def _():
        run(n_lat, n_ctx)


def _attention(q, k, v, bsz, seq_len, n_lat):
    tq = ROW_BLK
    tk = ROW_BLK
    per_seq = seq_len // tq
    gw = ATTN_GROUP * HEAD_DIM
    return pl.pallas_call(
        functools.partial(_attn_kernel, tk=tk, n_lat=n_lat, n_ctx=seq_len - n_lat,
                          n_lat_qblk=n_lat // tq),
        grid=(bsz, ATTN_KV_HEADS, per_seq),
        in_specs=[
            pl.BlockSpec((tq, gw), lambda b, g, i: (b * per_seq + i, g)),
            pl.BlockSpec((seq_len, HEAD_DIM), lambda b, g, i: (b, g)),
            pl.BlockSpec((seq_len, HEAD_DIM), lambda b, g, i: (b, g)),
        ],
        out_specs=pl.BlockSpec((tq, gw), lambda b, g, i: (b * per_seq + i, g)),
        out_shape=jax.ShapeDtypeStruct(q.shape, BF16),
        scratch_shapes=[pltpu.VMEM((ATTN_GROUP, tq, seq_len), F32),
                        pltpu.VMEM((ATTN_GROUP, tq, seq_len), BF16)],
        compiler_params=_cparams(("parallel", "parallel", "parallel"), 48),
        name="attention",
    )(q, k, v)


def _fourier_a_kernel(f_ref, wc_ref, o_ref):
    g = _dot(f_ref[...], wc_ref[...])
    wdt = f_ref.shape[1]
    o_ref[0, 0] = g[:, :wdt].astype(BF16)
    o_ref[0, 1] = g[:, wdt:].astype(BF16)


def _fourier_b_kernel(m_ref, g_ref, o_ref):
    o_ref[0] = _dot(m_ref[...], g_ref[0]).astype(BF16)


def _fourier(f2d, bsz, seq_len, row_off, n, wc, dft):
    per_seq = seq_len // ROW_BLK
    off_blk = row_off // ROW_BLK
    nb = n // ROW_BLK
    width = f2d.shape[1]
    g = pl.pallas_call(
        _fourier_a_kernel,
        grid=(bsz, nb),
        in_specs=[
            pl.BlockSpec((ROW_BLK, width), lambda b, t: (b * per_seq + off_blk + t, 0)),
            pl.BlockSpec((width, 2 * width), lambda b, t: (0, 0)),
        ],
        out_specs=pl.BlockSpec((1, 2, ROW_BLK, width), lambda b, t: (b, 0, t, 0)),
        out_shape=jax.ShapeDtypeStruct((bsz, 2, n, width), BF16),
        compiler_params=_cparams(("parallel", "parallel"), 32),
        name="fourier_chan",
    )(f2d, wc)
    g = g.reshape(bsz, 2 * n, width)
    return pl.pallas_call(
        _fourier_b_kernel,
        grid=(nb, bsz),
        in_specs=[
            pl.BlockSpec((ROW_BLK, 2 * n), lambda i, b: (i, 0)),
            pl.BlockSpec((1, 2 * n, width), lambda i, b: (b, 0, 0)),
        ],
        out_specs=pl.BlockSpec((1, ROW_BLK, width), lambda i, b: (b, i, 0)),
        out_shape=jax.ShapeDtypeStruct((bsz, n, width), BF16),
        compiler_params=_cparams(("parallel", "parallel"), 48),
        name="fourier_pos",
    )(dft, g)


def _dft_mats(n):
    i = jnp.arange(n, dtype=jnp.int32)
    prod = (i[:, None] * i[None, :]) % n
    ang = prod.astype(F32) * (2.0 * math.pi / n)
    s = n ** -0.5
    return jnp.concatenate([jnp.cos(ang) * s, jnp.sin(ang) * s], axis=1).astype(BF16)


def _chan_mats():
    i = jnp.arange(HEAD_DIM, dtype=jnp.int32)
    ang = ((i[:, None] * i[None, :]) % HEAD_DIM).astype(F32) * (2.0 * math.pi / HEAD_DIM)
    s = HEAD_DIM ** -0.5
    eye = jnp.eye(FOURIER_GROUPS, dtype=F32)
    c = jnp.kron(eye, jnp.cos(ang) * s)
    sn = jnp.kron(eye, -jnp.sin(ang) * s)
    return jnp.concatenate([c, sn], axis=1).astype(BF16)


def _ret_kernel(ld_ref, q_ref, k_ref, v_ref, o_ref, state_ref, *, n_lat_chunks, n_ctx_chunks):
    h = pl.program_id(1)
    d = pl.program_id(2)
    fwd = d == 0
    c = RET_CHUNK
    lg = ld_ref[d, h]
    ii = lax.broadcasted_iota(jnp.int32, (c, c), 0).astype(F32)
    jj = lax.broadcasted_iota(jnp.int32, (c, c), 1).astype(F32)
    diff = jnp.where(fwd, ii - jj, jj - ii)
    intra = jnp.where(diff >= 0, jnp.exp(lg * jnp.maximum(diff, 0.0)), 0.0)
    ri = lax.broadcasted_iota(jnp.int32, (c, 1), 0).astype(F32)
    q_dec = jnp.exp(lg * jnp.where(fwd, ri + 1.0, c - ri))
    k_dec = jnp.exp(lg * jnp.where(fwd, c - 1.0 - ri, ri))
    c_dec = jnp.exp(jnp.full((1, HEAD_DIM), lg * c, F32))
    state_ref[...] = jnp.zeros_like(state_ref)
    n_all = n_lat_chunks + n_ctx_chunks

    def body(s, _):
        in_ctx = s < n_ctx_chunks
        ctx_idx = jnp.where(fwd, s, n_ctx_chunks - 1 - s)
        lat_idx = jnp.where(fwd, s - n_ctx_chunks, n_all - 1 - s)
        chunk = jnp.where(in_ctx, n_lat_chunks + ctx_idx, lat_idx)
        off = pl.multiple_of(chunk * c, c)
        q = q_ref[pl.ds(off, c), :]
        k = k_ref[pl.ds(off, c), :]
        v = v_ref[pl.ds(off, c), :]
        st = state_ref[...]
        sc = _dot_nt(q, k) * intra
        o = _dot(sc.astype(BF16), v) + _dot(q, st.astype(BF16)) * q_dec
        kd = (k.astype(F32) * k_dec).astype(BF16)
        state_ref[...] = st * c_dec + _dot_tn(kd, v)
        mu = jnp.mean(o, axis=-1, keepdims=True)
        var = jnp.mean(jnp.square(o - mu), axis=-1, keepdims=True)
        o_ref[pl.ds(off, c), :] = (o - mu) * lax.rsqrt(var + EPS)
        return 0

    lax.fori_loop(0, n_all, body, 0)


def _retention(rq, rk, rv, log_decay, bsz, seq_len, n_lat):
    return pl.pallas_call(
        functools.partial(_ret_kernel, n_lat_chunks=n_lat // RET_CHUNK,
                          n_ctx_chunks=(seq_len - n_lat) // RET_CHUNK),
        grid=(bsz, RET_HEADS, 2),
        in_specs=[
            pl.BlockSpec(memory_space=pltpu.SMEM),
            pl.BlockSpec((seq_len, HEAD_DIM), lambda b, h, d: (b, d * RET_HEADS + h)),
            pl.BlockSpec((seq_len, HEAD_DIM), lambda b, h, d: (b, d * RET_HEADS + h)),
            pl.BlockSpec((seq_len, HEAD_DIM), lambda b, h, d: (b, h)),
        ],
        out_specs=pl.BlockSpec((seq_len, HEAD_DIM), lambda b, h, d: (b, d * RET_HEADS + h)),
        out_shape=jax.ShapeDtypeStruct((bsz * seq_len, 2 * RET_WIDTH), F32),
        scratch_shapes=[pltpu.VMEM((HEAD_DIM, HEAD_DIM), F32)],
        compiler_params=_cparams(("parallel", "parallel", "parallel"), 32),
        name="retention",
    )(log_decay, rq, rk, rv)


def _outproj_kernel(a_ref, fl_ref, fc_ref, on_ref, sg_ref, w_ref, x_ref, m_ref, g_ref, wr_ref,
                    xo_ref, h_ref, aff_ref, *, n_lat_blk):
    is_ctx = pl.program_id(1) >= n_lat_blk
    fm = jnp.where(is_ctx, fc_ref[0], fl_ref[0])
    rw = RET_WIDTH
    r = (sg_ref[:, :rw].astype(F32) * on_ref[:, :rw]
         + sg_ref[:, rw:].astype(F32) * on_ref[:, rw:]).astype(BF16)
    a0, f0 = ATTN_WIDTH, ATTN_WIDTH + FOURIER_WIDTH
    y = (_dot(a_ref[...], w_ref[:a0, :]) + _dot(fm, w_ref[a0:f0, :])) + _dot(r, w_ref[f0:, :])
    x = x_ref[0] + m_ref[0, 2:3, :] * y
    xo_ref[0] = x
    h = _rms_mod(x, g_ref[...], m_ref[0, 3:4, :], m_ref[0, 4:5, :])
    h_ref[0] = h.astype(BF16)
    hh, hl = _split(h)
    wh, wl = _split(wr_ref[...])
    lt = _dot_nt(wh, hh) + (_dot_nt(wh, hl) + _dot_nt(wl, hh))
    e = jnp.exp(lt - jnp.max(lt, axis=0, keepdims=True))
    aff_ref[...] = e / jnp.sum(e, axis=0, keepdims=True)


def _outproj(a, fm_lat, fm_ctx, on, sg, w_out, x, mods, g, wr_t, n_lat_blk):
    bsz, l, d = x.shape
    per_seq = l // ROW_BLK
    n_exp = wr_t.shape[0]
    fw = fm_lat.shape[-1]

    def flat(b, t):
        return (b * per_seq + t, 0)

    return pl.pallas_call(
        functools.partial(_outproj_kernel, n_lat_blk=n_lat_blk),
        grid=(bsz, per_seq),
        in_specs=[
            pl.BlockSpec((ROW_BLK, a.shape[1]), flat),
            pl.BlockSpec((1, ROW_BLK, fw), lambda b, t: (b, jnp.minimum(t, n_lat_blk - 1), 0)),
            pl.BlockSpec((1, ROW_BLK, fw), lambda b, t: (b, 0, 0)),
            pl.BlockSpec((ROW_BLK, on.shape[1]), flat),
            pl.BlockSpec((ROW_BLK, sg.shape[1]), flat),
            pl.BlockSpec(w_out.shape, lambda b, t: (0, 0)),
            pl.BlockSpec((1, ROW_BLK, d), lambda b, t: (b, t, 0)),
            pl.BlockSpec((1, 6, d), lambda b, t: (jnp.where(t >= n_lat_blk, bsz, b), 0, 0)),
            pl.BlockSpec((1, d), lambda b, t: (0, 0)),
            pl.BlockSpec(wr_t.shape, lambda b, t: (0, 0)),
        ],
        out_specs=[
            pl.BlockSpec((1, ROW_BLK, d), lambda b, t: (b, t, 0)),
            pl.BlockSpec((1, ROW_BLK, d), lambda b, t: (b, t, 0)),
            pl.BlockSpec((n_exp, ROW_BLK), lambda b, t: (0, b * per_seq + t)),
        ],
        out_shape=[
            jax.ShapeDtypeStruct((bsz, l, d), F32),
            jax.ShapeDtypeStruct((bsz, l, d), BF16),
            jax.ShapeDtypeStruct((n_exp, bsz * l), F32),
        ],
        compiler_params=_cparams(("parallel", "parallel"), 48),
        name="outproj",
    )(a, fm_lat, fm_ctx, on, sg, w_out, x, mods, g.reshape(1, d), wr_t)


def _cumsum_lanes(m, out_ref, fin):
    n_exp, n = m.shape
    tri = (lax.broadcasted_iota(jnp.int32, (LANES, LANES), 0)
           <= lax.broadcasted_iota(jnp.int32, (LANES, LANES), 1)).astype(BF16)
    run = jnp.zeros((n_exp, 1), F32)
    befores = []
    for k in range(n // LANES):
        befores.append(run)
        sl = slice(k * LANES, (k + 1) * LANES)
        cnt = _dot(m[:, sl].astype(BF16), tri) + run
        out_ref[0, :, sl] = fin(cnt, sl)
        run = cnt[:, LANES - 1:LANES]
    befores.append(run)
    return befores


def _select(seg, cap, slot_off, sel_ref, tmp_ref):
    bits = pltpu.bitcast(seg, jnp.int32)
    n_exp = seg.shape[0]

    def body(it, t):
        tt = t | lax.shift_left(jnp.int32(1), 30 - it)
        cnt = jnp.sum(jnp.where(bits >= tt, 1.0, 0.0), axis=1, keepdims=True)
        return jnp.where(cnt >= cap, tt, t)

    t = lax.fori_loop(0, 31, body, jnp.zeros((n_exp, 1), jnp.int32))
    gt = bits > t
    eq = bits == t
    need = cap - jnp.sum(jnp.where(gt, 1.0, 0.0), axis=1, keepdims=True)
    eqf = jnp.where(eq, 1.0, 0.0)
    _cumsum_lanes(eqf, tmp_ref, lambda cnt, sl: cnt)
    take = eq & (tmp_ref[0] - eqf < need)
    mask = gt | take
    maskf = jnp.where(mask, 1.0, 0.0)
    return _cumsum_lanes(
        maskf, sel_ref,
        lambda cnt, sl: jnp.where(maskf[:, sl] > 0.5, cnt - 1.0 + slot_off, -1.0))


def _routing_kernel(aff_ref, sel_l, gat_l, sel_c, gat_c, st_l, tmp_l, tmp_c, *, n_lat, cap_l, cap_c):
    b = pl.program_id(0)
    a = aff_ref[...]
    lat = a[:, :n_lat]
    ctx = a[:, n_lat:]
    gat_l[0] = lat
    gat_c[0] = ctx
    befores = _select(lat, float(cap_l), 0.0, sel_l, tmp_l)
    _select(ctx, float(cap_c), (b * cap_c).astype(F32), sel_c, tmp_c)
    lane = lax.broadcasted_iota(jnp.int32, (a.shape[0], LANES), 1)
    st = jnp.zeros((a.shape[0], LANES), F32)
    per_blk = ROW_BLK // LANES
    for j in range(n_lat // ROW_BLK + 1):
        st = jnp.where(lane == j, befores[j * per_blk], st)
    st_l[0] = st


def _routing(aff_t, bsz, seq_len, n_lat, cap_l, cap_c):
    n_exp = aff_t.shape[0]
    n_ctx = seq_len - n_lat
    shp = lambda n: jax.ShapeDtypeStruct((bsz, n_exp, n), F32)
    spec = lambda n: pl.BlockSpec((1, n_exp, n), lambda b: (b, 0, 0))
    return pl.pallas_call(
        functools.partial(_routing_kernel, n_lat=n_lat, cap_l=cap_l, cap_c=cap_c),
        grid=(bsz,),
        in_specs=[pl.BlockSpec((n_exp, seq_len), lambda b: (0, b))],
        out_specs=[spec(n_lat), spec(n_lat), spec(n_ctx), spec(n_ctx), spec(LANES)],
        out_shape=[shp(n_lat), shp(n_lat), shp(n_ctx), shp(n_ctx), shp(LANES)],
        scratch_shapes=[pltpu.VMEM((1, n_exp, n_lat), F32), pltpu.VMEM((1, n_exp, n_ctx), F32)],
        compiler_params=_cparams(("parallel",), 32),
        name="routing",
    )(aff_t)


EXPERT_GROUP = 8
SLOT_ALIGN = 16
SLOT_WINDOW = 64


def _windows(starts_ref, base, stride, n, j, cap, win):
    w, rounds = [], jnp.int32(0)
    for k in range(n):
        s0 = starts_ref[base + k * stride + j]
        s1 = starts_ref[base + k * stride + j + 1]
        wk = jnp.minimum((s0 // SLOT_ALIGN) * SLOT_ALIGN, cap - win)
        w.append(wk)
        rounds = jnp.maximum(rounds, (s1 - wk + win - 1) // win)
    return w, rounds


def _gather_kernel(starts_ref, h_ref, sel_ref, xg_ref, *, n_tb, cap, win, n_exp):
    v, g, j = pl.program_id(0), pl.program_id(1), pl.program_id(2)
    ng = sel_ref.shape[2]
    base = (v * n_exp + g * ng) * (n_tb + 1)

    @pl.when(j == 0)
    def _():
        xg_ref[...] = jnp.zeros_like(xg_ref)

    w, rounds = _windows(starts_ref, base, n_tb + 1, ng, j, cap, win)
    tb = h_ref.shape[1]
    row_i = lax.broadcasted_iota(jnp.int32, (win, tb), 0).astype(F32)

    def body(r, _):
        starts, pieces = [], []
        for k in range(ng):
            lo = w[k] + r * win
            c = pl.multiple_of(jnp.minimum(lo, cap - win), SLOT_ALIGN)
            sel = sel_ref[0, 0, k:k + 1, :]
            rel = jnp.where(sel >= lo.astype(F32), sel, -1.0) - c.astype(F32)
            pieces.append(jnp.where(rel == row_i, 1.0, 0.0).astype(BF16))
            starts.append(c)
        res = _dot(jnp.concatenate(pieces, axis=0), h_ref[0])
        for k in range(ng):
            rows = pl.ds(starts[k], win)
            xg_ref[k, rows, :] = (xg_ref[k, rows, :].astype(F32)
                                  + res[k * win:(k + 1) * win, :]).astype(BF16)
        return 0

    lax.fori_loop(0, rounds, body, 0)


def _moe_gather(starts, h, sel, n_tok, cap, win):
    nb, _, d = h.shape
    n_exp = sel.shape[1]
    ng = min(EXPERT_GROUP, n_exp)
    n_tb = n_tok // ROW_BLK
    sel = sel.reshape(nb, n_exp // ng, ng, n_tok)
    return pl.pallas_call(
        functools.partial(_gather_kernel, n_tb=n_tb, cap=cap, win=win, n_exp=n_exp),
        grid_spec=pltpu.PrefetchScalarGridSpec(
            num_scalar_prefetch=1,
            grid=(nb, n_exp // ng, n_tb),
            in_specs=[
                pl.BlockSpec((1, ROW_BLK, d), lambda v, g, j, s: (v, j, 0)),
                pl.BlockSpec((1, 1, ng, ROW_BLK), lambda v, g, j, s: (v, g, 0, j)),
            ],
            out_specs=pl.BlockSpec((ng, cap, d), lambda v, g, j, s: (g, v, 0)),
        ),
        out_shape=jax.ShapeDtypeStruct((n_exp, nb * cap, d), BF16),
        compiler_params=_cparams(("parallel", "parallel", "arbitrary"), 56),
        name="moe_gather",
    )(starts, h, sel)


def _ffn_kernel(xg_ref, wg_ref, wu_ref, act_ref, wgb_ref, wub_ref):
    @pl.when(pl.program_id(2) == 0)
    def _():
        wgb_ref[...] = wg_ref[0, 0].astype(BF16)
        wub_ref[...] = wu_ref[0, 0].astype(BF16)

    x = xg_ref[0]
    a = _dot(x, wgb_ref[...])
    u = _dot(x, wub_ref[...])
    act_ref[0] = (_silu(a) * u).astype(BF16)


def _moe_ffn(xg, w_gate, w_up, layer):
    n_exp, rows, d = xg.shape
    ff = w_gate.shape[3]
    fh = _largest_tile(ff, 512, LANES)
    rblk = _largest_tile(rows, 512)
    wspec = pl.BlockSpec((1, 1, d, fh), lambda e, f, r: (layer, e, 0, f))
    return pl.pallas_call(
        _ffn_kernel,
        grid=(n_exp, ff // fh, rows // rblk),
        in_specs=[pl.BlockSpec((1, rblk, d), lambda e, f, r: (e, r, 0)), wspec, wspec],
        out_specs=pl.BlockSpec((1, rblk, fh), lambda e, f, r: (e, r, f)),
        out_shape=jax.ShapeDtypeStruct((n_exp, rows, ff), BF16),
        scratch_shapes=[pltpu.VMEM((d, fh), BF16), pltpu.VMEM((d, fh), BF16)],
        compiler_params=_cparams(("parallel", "parallel", "arbitrary"), 48),
        name="moe_ffn",
    )(xg, w_gate, w_up)


def _down_kernel(act_ref, wd_ref, y_ref, wdb_ref):
    @pl.when(pl.program_id(1) == 0)
    def _():
        wdb_ref[...] = wd_ref[0, 0].astype(BF16)

    y_ref[0] = _dot(act_ref[0], wdb_ref[...]).astype(BF16)


def _moe_down(act, w_down, layer, cap):
    n_exp, rows, ff = act.shape
    d = w_down.shape[3]
    nb = rows // cap
    return pl.pallas_call(
        _down_kernel,
        grid=(n_exp, nb),
        in_specs=[
            pl.BlockSpec((1, cap, ff), lambda e, v: (e, v, 0)),
            pl.BlockSpec((1, 1, ff, d), lambda e, v: (layer, e, 0, 0)),
        ],
        out_specs=pl.BlockSpec((1, cap, d), lambda e, v: (v, e, 0)),
        out_shape=jax.ShapeDtypeStruct((nb, n_exp * cap, d), BF16),
        scratch_shapes=[pltpu.VMEM((ff, d), BF16)],
        compiler_params=_cparams(("parallel", "arbitrary"), 48),
        name="moe_down",
    )(act, w_down)


def _combine_kernel(starts_ref, y_ref, sel_ref, aff_ref, out_ref, st_ref, *, n_tb, cap, win, n_exp):
    v, j = pl.program_id(0), pl.program_id(2)
    base = v * n_exp * (n_tb + 1)
    w, rounds = _windows(starts_ref, base, n_tb + 1, n_exp, j, cap, win)
    tb = out_ref.shape[1]
    per_grp = LANES // win
    lane = lax.broadcasted_iota(jnp.int32, (tb, LANES), 1)
    lane_f = lane.astype(F32)
    out_ref[...] = jnp.zeros_like(out_ref)

    def body(r, _):
        groups = []
        for t in range(n_exp // per_grp):
            tgt = jnp.full((tb, LANES), -1.0, F32)
            gsel = jnp.zeros((tb, LANES), F32)
            for q in range(per_grp):
                e = t * per_grp + q
                lo = w[e] + r * win
                c = pl.multiple_of(jnp.minimum(lo, cap - win), SLOT_ALIGN)
                src = pl.multiple_of(e * cap + c, SLOT_ALIGN)
                st_ref[e * win:(e + 1) * win, :] = y_ref[0, pl.ds(src, win), :]
                sel = sel_ref[0, :, e:e + 1]
                rel = jnp.where(sel >= lo.astype(F32), sel, -1.0) - c.astype(F32)
                rel = jnp.where(rel >= 0.0, rel + float(q * win), -1.0)
                in_rng = (lane >= q * win) & (lane < (q + 1) * win)
                tgt = jnp.where(in_rng, rel, tgt)
                gsel = jnp.where(in_rng, aff_ref[0, :, e:e + 1], gsel)
            groups.append(jnp.where(tgt == lane_f, gsel, 0.0))
        gmat = jnp.concatenate(groups, axis=1)
        g_hi, g_lo = _split(gmat)
        st = st_ref[...]
        out_ref[0] += _dot(g_hi, st) + _dot(g_lo, st)
        return 0

    lax.fori_loop(0, rounds, body, 0)


def _moe_combine(starts, y, sel_t, aff_t, n_tok, cap, win):
    nb, _, d = y.shape
    n_exp = sel_t.shape[2]
    n_tb = n_tok // ROW_BLK
    pw = _largest_tile(d, 1024, LANES)
    tok = pl.BlockSpec((1, ROW_BLK, n_exp), lambda v, p, j, s: (v, j, 0))
    return pl.pallas_call(
        functools.partial(_combine_kernel, n_tb=n_tb, cap=cap, win=win, n_exp=n_exp),
        grid_spec=pltpu.PrefetchScalarGridSpec(
            num_scalar_prefetch=1,
            grid=(nb, d // pw, n_tb),
            in_specs=[pl.BlockSpec((1, n_exp * cap, pw), lambda v, p, j, s: (v, 0, p)), tok, tok],
            out_specs=pl.BlockSpec((1, ROW_BLK, pw), lambda v, p, j, s: (v, j, p)),
            scratch_shapes=[pltpu.VMEM((n_exp * win, pw), BF16)],
        ),
        out_shape=jax.ShapeDtypeStruct((nb, n_tok, d), F32),
        compiler_params=_cparams(("parallel", "parallel", "arbitrary"), 48),
        name="moe_combine",
    )(starts, y, sel_t, aff_t)


def _rope_tables(pos_groups, half):
    freqs = ROPE_BASE ** (-jnp.arange(half, dtype=F32) / half)
    cos, sa, sb = [], [], []
    for pos in pos_groups:
        ang = pos.astype(F32)[:, None] * freqs[None, :]
        c, s = jnp.cos(ang), jnp.sin(ang)
        z = jnp.zeros_like(s)
        cos += [c, c]
        sa += [-s, z]
        sb += [z, s]
    return jnp.stack([jnp.concatenate(t, axis=1) for t in (cos, sa, sb)])


def _attn_tables(n_lat, n_ctx):
    s = jnp.arange(n_lat)
    tab = _rope_tables([s // GRID_W, s % GRID_W], HEAD_DIM // 4)
    ident = jnp.stack([jnp.ones((n_ctx, HEAD_DIM), F32), jnp.zeros((n_ctx, HEAD_DIM), F32),
                       jnp.zeros((n_ctx, HEAD_DIM), F32)])
    return jnp.concatenate([tab, ident], axis=1)


def _ret_tables(n_lat, n_ctx):
    s = jnp.arange(n_lat)
    t = jnp.arange(n_ctx)
    fwd = jnp.concatenate([n_ctx + s, t])
    bwd = jnp.concatenate([n_ctx + (n_lat - 1 - s), n_ctx - 1 - t])
    return jnp.concatenate([_rope_tables([fwd], HEAD_DIM // 2),
                            _rope_tables([bwd], HEAD_DIM // 2)], axis=0)


def kernel(x, c, ctx, c_ctx, w_ada, b_ada, norm_mix, norm_ffn, w_in, q_norm, k_norm, ret_log_decay,
           w_out, w_router, w_gate, w_up, w_down):
    bsz, n_lat, d = x.shape
    n_ctx = ctx.shape[1]
    seq_len = n_lat + n_ctx
    depth = w_ada.shape[0]
    n_exp = w_router.shape[2]
    n_lat_blk = n_lat // ROW_BLK
    assert n_lat % ROW_BLK == 0 and n_ctx == ROW_BLK and bsz + 1 <= 8
    cap_l = CAPACITY_FACTOR * n_lat // n_exp
    cap_c = CAPACITY_FACTOR * n_ctx // n_exp

    cond8 = jnp.zeros((8, d), F32).at[:bsz].set(c).at[bsz].set(c_ctx)
    mods = _adaln(cond8, w_ada, b_ada).reshape(depth, 8, 6, d)

    attn_tab = _attn_tables(n_lat, n_ctx)
    ret_tab = _ret_tables(n_lat, n_ctx)
    wc = _chan_mats()
    dft_l = _dft_mats(n_lat)
    dft_c = _dft_mats(n_ctx)

    o_q, o_k = 0, ATTN_WIDTH
    o_v = o_k + KV_WIDTH
    o_f = o_v + KV_WIDTH
    o_rq = o_f + FOURIER_WIDTH
    o_rk = o_rq + RET_WIDTH
    o_rv = o_rk + RET_WIDTH
    o_g = o_rv + RET_WIDTH
    o_end = o_g + 2 * RET_WIDTH
    q_scale = HEAD_DIM ** -0.5 * math.log2(math.e)

    sb_c = bsz * cap_c
    win_l = min(cap_l, SLOT_WINDOW)
    win_c = cap_c
    starts_c = jnp.tile(jnp.arange(bsz + 1, dtype=jnp.int32) * cap_c, n_exp)

    xs = jnp.concatenate([x, ctx], axis=1)
    delta_lat = delta_ctx = None
    for l in range(depth):
        xs, h = _norm(xs, mods[l], norm_mix[l], n_lat_blk, delta_lat, delta_ctx,
                      mods[l - 1] if l else None)
        h2d = h.reshape(bsz * seq_len, d)
        wl = w_in[l].astype(BF16)
        q = _proj(h2d, wl[:, o_q:o_k], seq_len, "qk", gain=q_norm[l], tab=attn_tab,
                  n_heads=ATTN_HEADS, scale=q_scale, half=HEAD_DIM // 4)
        k = _proj(h2d, wl[:, o_k:o_v], seq_len, "qk", gain=k_norm[l], tab=attn_tab,
                  n_heads=ATTN_KV_HEADS, scale=1.0, half=HEAD_DIM // 4)
        v, f = _proj(h2d, wl[:, o_v:o_rq], seq_len, "split", split_at=KV_WIDTH)
        rq = _proj(h2d, wl[:, o_rq:o_rk], seq_len, "ret", tab=ret_tab, n_heads=RET_HEADS,
                   scale=1.0, half=HEAD_DIM // 2)
        rk = _proj(h2d, wl[:, o_rk:o_rv], seq_len, "ret", tab=ret_tab, n_heads=RET_HEADS,
                   scale=HEAD_DIM ** -0.5, half=HEAD_DIM // 2)
        rv = _proj(h2d, wl[:, o_rv:o_g], seq_len, "plain")
        sg = _proj(h2d, wl[:, o_g:o_end], seq_len, "silu")

        a = _attention(q, k, v, bsz, seq_len, n_lat)
        fm_lat = _fourier(f, bsz, seq_len, 0, n_lat, wc, dft_l)
        fm_ctx = _fourier(f, bsz, seq_len, n_lat, n_ctx, wc, dft_c)
        on = _retention(rq, rk, rv, ret_log_decay[l], bsz, seq_len, n_lat)

        xs, h2, aff_t = _outproj(a, fm_lat, fm_ctx, on, sg, w_out[l].astype(BF16), xs, mods[l],
                                 norm_ffn[l], w_router[l].T, n_lat_blk)

        sel_l, gat_l, sel_c, gat_c, st_l = _routing(aff_t, bsz, seq_len, n_lat, cap_l, cap_c)
        starts_l = st_l[:, :, :n_lat_blk + 1].astype(jnp.int32).reshape(-1)
        merge = lambda t: t.transpose(1, 0, 2).reshape(1, n_exp, bsz * n_ctx)
        tok_major = lambda t: t.transpose(0, 2, 1)
        sel_c, gat_c = merge(sel_c), merge(gat_c)
        h_ctx = h2[:, n_lat:, :].reshape(1, bsz * n_ctx, d)

        xg_l = _moe_gather(starts_l, h2, sel_l, n_lat, cap_l, win_l)
        xg_c = _moe_gather(starts_c, h_ctx, sel_c, bsz * n_ctx, sb_c, win_c)
        y_l = _moe_down(_moe_ffn(xg_l, w_gate, w_up, l), w_down, l, cap_l)
        y_c = _moe_down(_moe_ffn(xg_c, w_gate, w_up, l), w_down, l, sb_c)
        delta_lat = _moe_combine(starts_l, y_l, tok_major(sel_l), tok_major(gat_l), n_lat, cap_l,
                                 win_l)
        delta_ctx = _moe_combine(starts_c, y_c, tok_major(sel_c), tok_major(gat_c), bsz * n_ctx,
                                 sb_c, win_c).reshape(bsz * n_ctx, d)

    out, _ = _norm(xs, None, None, n_lat_blk, delta_lat, delta_ctx, mods[depth - 1],
                   want_h=False, lat_only=True)
    return out
```

```python
import functools
import math

import jax
import jax.numpy as jnp
from jax import lax
from jax.experimental import pallas as pl
from jax.experimental.pallas import tpu as pltpu

HEAD_DIM = 128
ATTN_HEADS = 8
ATTN_KV_HEADS = 2
ATTN_GROUP = ATTN_HEADS // ATTN_KV_HEADS
FOURIER_GROUPS = 4
RET_HEADS = 4
RET_CHUNK = 128
GRID_W = 64
ROPE_BASE = 10000.0
EPS = 1e-6
CAPACITY_FACTOR = 2

ATTN_WIDTH = ATTN_HEADS * HEAD_DIM
KV_WIDTH = ATTN_KV_HEADS * HEAD_DIM
FOURIER_WIDTH = FOURIER_GROUPS * HEAD_DIM
RET_WIDTH = RET_HEADS * HEAD_DIM

ROW_BLK = 256
LANES = 128
MIB = 1024 * 1024

F32 = jnp.float32
BF16 = jnp.bfloat16


def _cparams(sem, vmem_mib):
    return pltpu.CompilerParams(dimension_semantics=sem, vmem_limit_bytes=vmem_mib * MIB)


def _largest_tile(n, cap, mult=8):
    best = mult
    for t in range(mult, min(n, cap) + 1, mult):
        if n % t == 0:
            best = t
    return best


def _split(a):
    hi = a.astype(BF16)
    lo = (a - hi.astype(F32)).astype(BF16)
    return hi, lo


def _dot(a, b):
    return jnp.dot(a, b, preferred_element_type=F32)


def _dot_nt(a, b):
    return lax.dot_general(a, b, (((1,), (1,)), ((), ())), preferred_element_type=F32)


def _dot_tn(a, b):
    return lax.dot_general(a, b, (((0,), (0,)), ((), ())), preferred_element_type=F32)


def _silu(a):
    return a / (1.0 + jnp.exp(-a))


def _adaln_kernel(c_ref, w_ref, b_ref, o_ref):
    s = _silu(c_ref[...])
    sh, sl = _split(s)
    wh, wl = _split(w_ref[0])
    o_ref[0] = _dot(sh, wh) + (_dot(sh, wl) + _dot(sl, wh)) + b_ref[0]


def _adaln(cond8, w_ada, b_ada):
    depth, d, n = w_ada.shape
    tn = _largest_tile(n, 768, LANES)
    return pl.pallas_call(
        _adaln_kernel,
        grid=(depth, n // tn),
        in_specs=[
            pl.BlockSpec((8, d), lambda l, j: (0, 0)),
            pl.BlockSpec((1, d, tn), lambda l, j: (l, 0, j)),
            pl.BlockSpec((1, 1, tn), lambda l, j: (l, 0, j)),
        ],
        out_specs=pl.BlockSpec((1, 8, tn), lambda l, j: (l, 0, j)),
        out_shape=jax.ShapeDtypeStruct((depth, 8, n), F32),
        compiler_params=_cparams(("parallel", "parallel"), 48),
        name="adaln",
    )(cond8, w_ada, b_ada.reshape(depth, 1, n))


def _rms_mod(x, g, shift, scale):
    y = x * lax.rsqrt(jnp.mean(x * x, axis=-1, keepdims=True) + EPS)
    return (y * g) * (1.0 + scale) + shift


def _norm_kernel(*refs, has_delta, want_h, n_lat_blk):
    refs = list(refs)
    x_ref = refs.pop(0)
    x = x_ref[0]
    if has_delta:
        dl_ref, dc_ref, pm_ref = refs.pop(0), refs.pop(0), refs.pop(0)
        is_ctx = pl.program_id(1) >= n_lat_blk
        delta = jnp.where(is_ctx, dc_ref[...], dl_ref[0])
        x = x + pm_ref[0, 5:6, :] * delta
    if want_h:
        m_ref, g_ref = refs.pop(0), refs.pop(0)
    if has_delta:
        xo_ref = refs.pop(0)
        xo_ref[0] = x
    if want_h:
        h_ref = refs.pop(0)
        h_ref[0] = _rms_mod(x, g_ref[...], m_ref[0, 0:1, :], m_ref[0, 1:2, :]).astype(BF16)


def _norm(x, mods, g, n_lat_blk, delta_lat=None, delta_ctx=None, prev_mods=None,
          want_h=True, lat_only=False):
    bsz, l, d = x.shape
    n_blk = n_lat_blk if lat_only else l // ROW_BLK
    has_delta = delta_lat is not None

    def mod_row(b, t):
        return (jnp.where(t >= n_lat_blk, bsz, b), 0, 0)

    xspec = pl.BlockSpec((1, ROW_BLK, d), lambda b, t: (b, t, 0))
    in_specs, args = [xspec], [x]
    if has_delta:
        in_specs += [
            pl.BlockSpec((1, ROW_BLK, d), lambda b, t: (b, jnp.minimum(t, n_lat_blk - 1), 0)),
            pl.BlockSpec((ROW_BLK, d), lambda b, t: (b, 0)),
            pl.BlockSpec((1, 6, d), mod_row),
        ]
        args += [delta_lat, delta_ctx, prev_mods]
    if want_h:
        in_specs += [pl.BlockSpec((1, 6, d), mod_row), pl.BlockSpec((1, d), lambda b, t: (0, 0))]
        args += [mods, g.reshape(1, d)]
    out_specs, out_shape = [], []
    rows = n_blk * ROW_BLK
    if has_delta:
        out_specs.append(xspec)
        out_shape.append(jax.ShapeDtypeStruct((bsz, rows, d), F32))
    if want_h:
        out_specs.append(xspec)
        out_shape.append(jax.ShapeDtypeStruct((bsz, rows, d), BF16))
    outs = pl.pallas_call(
        functools.partial(_norm_kernel, has_delta=has_delta, want_h=want_h, n_lat_blk=n_lat_blk),
        grid=(bsz, n_blk),
        in_specs=in_specs,
        out_specs=out_specs,
        out_shape=out_shape,
        compiler_params=_cparams(("parallel", "parallel"), 32),
        name="norm",
    )(*args)
    outs = list(outs)
    x_new = outs.pop(0) if has_delta else x
    h = outs.pop(0) if want_h else None
    return x_new, h


def _rope(z, tab_ref, k, half):
    return (z * tab_ref[k] + pltpu.roll(z, LANES - half, 1) * tab_ref[k + 1]
            + pltpu.roll(z, half, 1) * tab_ref[k + 2])


def _proj_kernel(*refs, flavor, n_heads, scale, half, split_at):
    h_ref, w_ref = refs[0], refs[1]
    acc = _dot(h_ref[...], w_ref[...])
    if flavor == "plain":
        refs[2][...] = acc.astype(BF16)
    elif flavor == "split":
        refs[2][...] = acc[:, :split_at].astype(BF16)
        refs[3][...] = acc[:, split_at:].astype(BF16)
    elif flavor == "silu":
        refs[2][...] = _silu(acc).astype(BF16)
    elif flavor == "qk":
        gain_ref, tab_ref, o_ref = refs[2], refs[3], refs[4]
        for hd in range(n_heads):
            sl = slice(hd * HEAD_DIM, (hd + 1) * HEAD_DIM)
            z = acc[:, sl]
            z = (z * lax.rsqrt(jnp.mean(z * z, axis=-1, keepdims=True) + EPS)) * gain_ref[...]
            o_ref[:, sl] = (_rope(z, tab_ref, 0, half) * scale).astype(BF16)
    elif flavor == "ret":
        tab_ref, o_ref = refs[2], refs[3]
        width = n_heads * HEAD_DIM
        for hd in range(n_heads):
            sl = slice(hd * HEAD_DIM, (hd + 1) * HEAD_DIM)
            z = acc[:, sl] * scale
            o_ref[:, sl] = _rope(z, tab_ref, 0, half).astype(BF16)
            o_ref[:, width + hd * HEAD_DIM: width + (hd + 1) * HEAD_DIM] = (
                _rope(z, tab_ref, 3, half).astype(BF16))
    else:
        raise ValueError(flavor)


def _proj(h2d, w, seq_len, flavor, *, gain=None, tab=None, n_heads=0, scale=1.0, half=0,
          split_at=0):
    t_rows, d = h2d.shape
    n = w.shape[1]
    tm = _largest_tile(seq_len, 1088)
    per_seq = seq_len // tm
    in_specs = [pl.BlockSpec((tm, d), lambda i: (i, 0)), pl.BlockSpec((d, n), lambda i: (0, 0))]
    args = [h2d, w]
    if flavor == "qk":
        in_specs.append(pl.BlockSpec((1, HEAD_DIM), lambda i: (0, 0)))
        args.append(gain.reshape(1, HEAD_DIM))
    if flavor in ("qk", "ret"):
        ntab = tab.shape[0]
        in_specs.append(pl.BlockSpec((ntab, tm, HEAD_DIM), lambda i: (0, i % per_seq, 0)))
        args.append(tab)
    if flavor == "split":
        widths = [split_at, n - split_at]
    elif flavor == "ret":
        widths = [2 * n]
    else:
        widths = [n]
    out_specs = [pl.BlockSpec((tm, wd), lambda i: (i, 0)) for wd in widths]
    out_shape = [jax.ShapeDtypeStruct((t_rows, wd), BF16) for wd in widths]
    outs = pl.pallas_call(
        functools.partial(_proj_kernel, flavor=flavor, n_heads=n_heads, scale=scale, half=half,
                          split_at=split_at),
        grid=(t_rows // tm,),
        in_specs=in_specs,
        out_specs=out_specs,
        out_shape=out_shape,
        compiler_params=_cparams(("parallel",), 48),
        name="proj_" + flavor,
    )(*args)
    return outs if len(outs) > 1 else outs[0]


SUBLANES = 8


SAFE_SHIFT = 50.0


def _lane_fold(t, op):
    return functools.reduce(op, [t[:, i * LANES:(i + 1) * LANES]
                                 for i in range(t.shape[1] // LANES)])


def _attn_head_bounded(q, m, k_ref, v_ref, p_ref, k_lo, n_keys, tk):
    lp = None
    for c in range(n_keys // tk):
        s = _dot_nt(q, k_ref[k_lo + c * tk:k_lo + (c + 1) * tk, :])
        p = jnp.exp2(s - m)
        part = _lane_fold(p, jnp.add)
        lp = part if lp is None else lp + part
        p_ref[:, c * tk:(c + 1) * tk] = p.astype(BF16)
    l = jnp.sum(lp, axis=-1, keepdims=True)
    return _dot(p_ref[:, :n_keys], v_ref[k_lo:k_lo + n_keys, :]) / l


def _attn_head_exact(q, k_ref, v_ref, s_ref, p_ref, k_lo, n_keys, tk):
    s_ref[:, :n_keys] = _dot_nt(q, k_ref[k_lo:k_lo + n_keys, :])
    n_chunks = n_keys // tk
    mp = None
    for c in range(n_chunks):
        part = _lane_fold(s_ref[:, c * tk:(c + 1) * tk], jnp.maximum)
        mp = part if mp is None else jnp.maximum(mp, part)
    m = jnp.max(mp, axis=-1, keepdims=True)
    lp = None
    for c in range(n_chunks):
        p = jnp.exp2(s_ref[:, c * tk:(c + 1) * tk] - m)
        part = _lane_fold(p, jnp.add)
        lp = part if lp is None else lp + part
        p_ref[:, c * tk:(c + 1) * tk] = p.astype(BF16)
    l = jnp.sum(lp, axis=-1, keepdims=True)
    return _dot(p_ref[:, :n_keys], v_ref[k_lo:k_lo + n_keys, :]) / l


def _attn_kernel(q_ref, k_ref, v_ref, o_ref, s_ref, p_ref, kmax_ref, *, tk, n_lat, n_ctx,
                 n_lat_qblk):
    i = pl.program_id(2)
    is_ctx = i >= n_lat_qblk

    @pl.when(i == 0)
    def _():
        kk = k_ref[...].astype(F32)
        kn2 = jnp.max(jnp.sum(kk * kk, axis=-1, keepdims=True), axis=0, keepdims=True)
        kmax_ref[...] = jnp.broadcast_to(jnp.sqrt(kn2), kmax_ref.shape)

    heads = [slice(hd * HEAD_DIM, (hd + 1) * HEAD_DIM) for hd in range(ATTN_GROUP)]
    bounds = []
    for sl in heads:
        qf = q_ref[:, sl].astype(F32)
        qn = jnp.sqrt(jnp.sum(qf * qf, axis=-1, keepdims=True))
        bounds.append(qn * kmax_ref[0:1, 0:1] * (1.0 + 2.0 ** -10))
    worst = functools.reduce(jnp.maximum, [jnp.max(b) for b in bounds])
    bounded_ok = worst <= SAFE_SHIFT

    def run(k_lo, n_keys, bounded):
        for hd, sl in enumerate(heads):
            if bounded:
                o = _attn_head_bounded(q_ref[:, sl], bounds[hd], k_ref, v_ref, p_ref.at[hd], k_lo,
                                       n_keys, tk)
            else:
                o = _attn_head_exact(q_ref[:, sl], k_ref, v_ref, s_ref.at[hd], p_ref.at[hd], k_lo,
                                     n_keys, tk)
            o_ref[:, sl] = o.astype(BF16)

    for ctx_case, (k_lo, n_keys) in ((False, (0, n_lat + n_ctx)), (True, (n_lat, n_ctx))):
        for bounded in (True, False):
            @pl.when((is_ctx == ctx_case) & (bounded_ok == bounded))
            def _(k_lo=k_lo, n_keys=n_keys, bounded=bounded):
                run(k_lo, n_keys, bounded)


def _attention(q, k, v, bsz, seq_len, n_lat):
    tq = ROW_BLK
    tk = ROW_BLK
    per_seq = seq_len // tq
    gw = ATTN_GROUP * HEAD_DIM
    return pl.pallas_call(
        functools.partial(_attn_kernel, tk=tk, n_lat=n_lat, n_ctx=seq_len - n_lat,
                          n_lat_qblk=n_lat // tq),
        grid=(bsz, ATTN_KV_HEADS, per_seq),
        in_specs=[
            pl.BlockSpec((tq, gw), lambda b, g, i: (b * per_seq + i, g)),
            pl.BlockSpec((seq_len, HEAD_DIM), lambda b, g, i: (b, g)),
            pl.BlockSpec((seq_len, HEAD_DIM), lambda b, g, i: (b, g)),
        ],
        out_specs=pl.BlockSpec((tq, gw), lambda b, g, i: (b * per_seq + i, g)),
        out_shape=jax.ShapeDtypeStruct(q.shape, BF16),
        scratch_shapes=[pltpu.VMEM((ATTN_GROUP, tq, seq_len), F32),
                        pltpu.VMEM((ATTN_GROUP, tq, seq_len), BF16),
                        pltpu.VMEM((SUBLANES, LANES), F32)],
        compiler_params=_cparams(("parallel", "parallel", "arbitrary"), 48),
        name="attention",
    )(q, k, v)


def _fourier_a_kernel(f_ref, wc_ref, o_ref):
    g = _dot(f_ref[...], wc_ref[...])
    wdt = f_ref.shape[1]
    o_ref[0, 0] = g[:, :wdt].astype(BF16)
    o_ref[0, 1] = g[:, wdt:].astype(BF16)


def _fourier_b_kernel(m_ref, g_ref, o_ref):
    o_ref[0] = _dot(m_ref[...], g_ref[0]).astype(BF16)


def _fourier(f2d, bsz, seq_len, row_off, n, wc, dft):
    per_seq = seq_len // ROW_BLK
    off_blk = row_off // ROW_BLK
    nb = n // ROW_BLK
    width = f2d.shape[1]
    g = pl.pallas_call(
        _fourier_a_kernel,
        grid=(bsz, nb),
        in_specs=[
            pl.BlockSpec((ROW_BLK, width), lambda b, t: (b * per_seq + off_blk + t, 0)),
            pl.BlockSpec((width, 2 * width), lambda b, t: (0, 0)),
        ],
        out_specs=pl.BlockSpec((1, 2, ROW_BLK, width), lambda b, t: (b, 0, t, 0)),
        out_shape=jax.ShapeDtypeStruct((bsz, 2, n, width), BF16),
        compiler_params=_cparams(("parallel", "parallel"), 32),
        name="fourier_chan",
    )(f2d, wc)
    g = g.reshape(bsz, 2 * n, width)
    return pl.pallas_call(
        _fourier_b_kernel,
        grid=(nb, bsz),
        in_specs=[
            pl.BlockSpec((ROW_BLK, 2 * n), lambda i, b: (i, 0)),
            pl.BlockSpec((1, 2 * n, width), lambda i, b: (b, 0, 0)),
        ],
        out_specs=pl.BlockSpec((1, ROW_BLK, width), lambda i, b: (b, i, 0)),
        out_shape=jax.ShapeDtypeStruct((bsz, n, width), BF16),
        compiler_params=_cparams(("parallel", "parallel"), 48),
        name="fourier_pos",
    )(dft, g)


def _dft_mats(n):
    i = jnp.arange(n, dtype=jnp.int32)
    prod = (i[:, None] * i[None, :]) % n
    ang = prod.astype(F32) * (2.0 * math.pi / n)
    s = n ** -0.5
    return jnp.concatenate([jnp.cos(ang) * s, jnp.sin(ang) * s], axis=1).astype(BF16)


def _chan_mats():
    i = jnp.arange(HEAD_DIM, dtype=jnp.int32)
    ang = ((i[:, None] * i[None, :]) % HEAD_DIM).astype(F32) * (2.0 * math.pi / HEAD_DIM)
    s = HEAD_DIM ** -0.5
    eye = jnp.eye(FOURIER_GROUPS, dtype=F32)
    c = jnp.kron(eye, jnp.cos(ang) * s)
    sn = jnp.kron(eye, -jnp.sin(ang) * s)
    return jnp.concatenate([c, sn], axis=1).astype(BF16)


RET_HEADS_PER_STEP = 2


def _ret_kernel(ld_ref, qf_ref, qb_ref, kf_ref, kb_ref, v_ref, of_ref, ob_ref, state_ref,
                *, n_lat_chunks, n_ctx_chunks):
    c = RET_CHUNK
    n_all = n_lat_chunks + n_ctx_chunks
    h0 = pl.program_id(1) * RET_HEADS_PER_STEP
    ii = lax.broadcasted_iota(jnp.int32, (c, c), 0).astype(F32)
    jj = lax.broadcasted_iota(jnp.int32, (c, c), 1).astype(F32)
    ri = lax.broadcasted_iota(jnp.int32, (c, 1), 0).astype(F32)
    state_ref[...] = jnp.zeros_like(state_ref)

    chains = []
    for hh in range(RET_HEADS_PER_STEP):
        sl = slice(hh * HEAD_DIM, (hh + 1) * HEAD_DIM)
        for d, (q_ref, k_ref, o_ref) in enumerate(((qf_ref, kf_ref, of_ref),
                                                   (qb_ref, kb_ref, ob_ref))):
            lg = ld_ref[d, h0 + hh]
            diff = ii - jj if d == 0 else jj - ii
            intra = jnp.where(diff >= 0, jnp.exp(lg * jnp.maximum(diff, 0.0)), 0.0)
            q_dec = jnp.exp(lg * (ri + 1.0 if d == 0 else c - ri))
            k_dec = jnp.exp(lg * (c - 1.0 - ri if d == 0 else ri))
            c_dec = jnp.exp(jnp.full((1, HEAD_DIM), lg * c, F32))
            chains.append((len(chains), q_ref, k_ref, o_ref, sl, (intra, q_dec, k_dec, c_dec), d))

    def body(s, _):
        in_ctx = s < n_ctx_chunks
        for slot, q_ref, k_ref, o_ref, sl, (intra, q_dec, k_dec, c_dec), d in chains:
            if d == 0:
                chunk = jnp.where(in_ctx, n_lat_chunks + s, s - n_ctx_chunks)
            else:
                chunk = n_all - 1 - s
            off = pl.multiple_of(chunk * c, c)
            q = q_ref[pl.ds(off, c), sl]
            k = k_ref[pl.ds(off, c), sl]
            v = v_ref[pl.ds(off, c), sl]
            st = state_ref[slot]
            sc = _dot_nt(q, k) * intra
            o = _dot(sc.astype(BF16), v) + _dot(q, st.astype(BF16)) * q_dec
            kd = (k.astype(F32) * k_dec).astype(BF16)
            state_ref[slot] = st * c_dec + _dot_tn(kd, v)
            mu = jnp.mean(o, axis=-1, keepdims=True)
            var = jnp.mean(jnp.square(o - mu), axis=-1, keepdims=True)
            o_ref[pl.ds(off, c), sl] = (o - mu) * lax.rsqrt(var + EPS)
        return 0

    lax.fori_loop(0, n_all, body, 0)


def _retention(rq, rk, rv, log_decay, bsz, seq_len, n_lat):
    hw = RET_HEADS_PER_STEP * HEAD_DIM
    n_hb = RET_HEADS // RET_HEADS_PER_STEP
    fwd = pl.BlockSpec((seq_len, hw), lambda b, h: (b, h))
    bwd = pl.BlockSpec((seq_len, hw), lambda b, h: (b, n_hb + h))
    return pl.pallas_call(
        functools.partial(_ret_kernel, n_lat_chunks=n_lat // RET_CHUNK,
                          n_ctx_chunks=(seq_len - n_lat) // RET_CHUNK),
        grid=(bsz, n_hb),
        in_specs=[pl.BlockSpec(memory_space=pltpu.SMEM), fwd, bwd, fwd, bwd, fwd],
        out_specs=[fwd, fwd],
        out_shape=[jax.ShapeDtypeStruct((bsz * seq_len, RET_WIDTH), F32)] * 2,
        scratch_shapes=[pltpu.VMEM((2 * RET_HEADS_PER_STEP, HEAD_DIM, HEAD_DIM), F32)],
        compiler_params=_cparams(("parallel", "parallel"), 56),
        name="retention",
    )(log_decay, rq, rq, rk, rk, rv)


def _outproj_kernel(a_ref, fl_ref, fc_ref, of_ref, ob_ref, sg_ref, w_ref, x_ref, m_ref, g_ref, wr_ref,
                    xo_ref, h_ref, aff_ref, *, n_lat_blk):
    is_ctx = pl.program_id(1) >= n_lat_blk
    fm = jnp.where(is_ctx, fc_ref[0], fl_ref[0])
    rw = RET_WIDTH
    r = (sg_ref[:, :rw].astype(F32) * of_ref[...]
         + sg_ref[:, rw:].astype(F32) * ob_ref[...]).astype(BF16)
    a0, f0 = ATTN_WIDTH, ATTN_WIDTH + FOURIER_WIDTH
    y = (_dot(a_ref[...], w_ref[:a0, :]) + _dot(fm, w_ref[a0:f0, :])) + _dot(r, w_ref[f0:, :])
    x = x_ref[0] + m_ref[0, 2:3, :] * y
    xo_ref[0] = x
    h = _rms_mod(x, g_ref[...], m_ref[0, 3:4, :], m_ref[0, 4:5, :])
    h_ref[0] = h.astype(BF16)
    hh, hl = _split(h)
    wh, wl = _split(wr_ref[...])
    lt = _dot_nt(wh, hh) + (_dot_nt(wh, hl) + _dot_nt(wl, hh))
    e = jnp.exp(lt - jnp.max(lt, axis=0, keepdims=True))
    aff_ref[...] = e / jnp.sum(e, axis=0, keepdims=True)


def _outproj(a, fm_lat, fm_ctx, o_f, o_b, sg, w_out, x, mods, g, wr_t, n_lat_blk):
    bsz, l, d = x.shape
    per_seq = l // ROW_BLK
    n_exp = wr_t.shape[0]
    fw = fm_lat.shape[-1]

    def flat(b, t):
        return (b * per_seq + t, 0)

    return pl.pallas_call(
        functools.partial(_outproj_kernel, n_lat_blk=n_lat_blk),
        grid=(bsz, per_seq),
        in_specs=[
            pl.BlockSpec((ROW_BLK, a.shape[1]), flat),
            pl.BlockSpec((1, ROW_BLK, fw), lambda b, t: (b, jnp.minimum(t, n_lat_blk - 1), 0)),
            pl.BlockSpec((1, ROW_BLK, fw), lambda b, t: (b, 0, 0)),
            pl.BlockSpec((ROW_BLK, o_f.shape[1]), flat),
            pl.BlockSpec((ROW_BLK, o_b.shape[1]), flat),
            pl.BlockSpec((ROW_BLK, sg.shape[1]), flat),
            pl.BlockSpec(w_out.shape, lambda b, t: (0, 0)),
            pl.BlockSpec((1, ROW_BLK, d), lambda b, t: (b, t, 0)),
            pl.BlockSpec((1, 6, d), lambda b, t: (jnp.where(t >= n_lat_blk, bsz, b), 0, 0)),
            pl.BlockSpec((1, d), lambda b, t: (0, 0)),
            pl.BlockSpec(wr_t.shape, lambda b, t: (0, 0)),
        ],
        out_specs=[
            pl.BlockSpec((1, ROW_BLK, d), lambda b, t: (b, t, 0)),
            pl.BlockSpec((1, ROW_BLK, d), lambda b, t: (b, t, 0)),
            pl.BlockSpec((n_exp, ROW_BLK), lambda b, t: (0, b * per_seq + t)),
        ],
        out_shape=[
            jax.ShapeDtypeStruct((bsz, l, d), F32),
            jax.ShapeDtypeStruct((bsz, l, d), BF16),
            jax.ShapeDtypeStruct((n_exp, bsz * l), F32),
        ],
        compiler_params=_cparams(("parallel", "parallel"), 48),
        name="outproj",
    )(a, fm_lat, fm_ctx, o_f, o_b, sg, w_out, x, mods, g.reshape(1, d), wr_t)


def _cumsum_lanes(m, out_ref, fin):
    n_exp, n = m.shape
    tri = (lax.broadcasted_iota(jnp.int32, (LANES, LANES), 0)
           <= lax.broadcasted_iota(jnp.int32, (LANES, LANES), 1)).astype(BF16)
    run = jnp.zeros((n_exp, 1), F32)
    befores = []
    for k in range(n // LANES):
        befores.append(run)
        sl = slice(k * LANES, (k + 1) * LANES)
        cnt = _dot(m[:, sl].astype(BF16), tri) + run
        out_ref[0, :, sl] = fin(cnt, sl)
        run = cnt[:, LANES - 1:LANES]
    befores.append(run)
    return befores


def _select(seg, cap, slot_off, sel_ref, tmp_ref):
    bits = pltpu.bitcast(seg, jnp.int32)
    n_exp = seg.shape[0]

    def body(it, t):
        tt = t | lax.shift_left(jnp.int32(1), 30 - it)
        cnt = jnp.sum(jnp.where(bits >= tt, 1.0, 0.0), axis=1, keepdims=True)
        return jnp.where(cnt >= cap, tt, t)

    t = lax.fori_loop(0, 31, body, jnp.zeros((n_exp, 1), jnp.int32))
    gt = bits > t
    eq = bits == t
    need = cap - jnp.sum(jnp.where(gt, 1.0, 0.0), axis=1, keepdims=True)
    eqf = jnp.where(eq, 1.0, 0.0)
    _cumsum_lanes(eqf, tmp_ref, lambda cnt, sl: cnt)
    take = eq & (tmp_ref[0] - eqf < need)
    mask = gt | take
    maskf = jnp.where(mask, 1.0, 0.0)
    return _cumsum_lanes(
        maskf, sel_ref,
        lambda cnt, sl: jnp.where(maskf[:, sl] > 0.5, cnt - 1.0 + slot_off, -1.0))


def _routing_kernel(aff_ref, sel_l, gat_l, sel_c, gat_c, st_l, tmp_l, tmp_c, *, n_lat, cap_l, cap_c):
    b = pl.program_id(0)
    a = aff_ref[...]
    lat = a[:, :n_lat]
    ctx = a[:, n_lat:]
    gat_l[0] = lat
    gat_c[0] = ctx
    befores = _select(lat, float(cap_l), 0.0, sel_l, tmp_l)
    _select(ctx, float(cap_c), (b * cap_c).astype(F32), sel_c, tmp_c)
    lane = lax.broadcasted_iota(jnp.int32, (a.shape[0], LANES), 1)
    st = jnp.zeros((a.shape[0], LANES), F32)
    per_blk = ROW_BLK // LANES
    for j in range(n_lat // ROW_BLK + 1):
        st = jnp.where(lane == j, befores[j * per_blk], st)
    st_l[0] = st


def _routing(aff_t, bsz, seq_len, n_lat, cap_l, cap_c):
    n_exp = aff_t.shape[0]
    n_ctx = seq_len - n_lat
    shp = lambda n: jax.ShapeDtypeStruct((bsz, n_exp, n), F32)
    spec = lambda n: pl.BlockSpec((1, n_exp, n), lambda b: (b, 0, 0))
    return pl.pallas_call(
        functools.partial(_routing_kernel, n_lat=n_lat, cap_l=cap_l, cap_c=cap_c),
        grid=(bsz,),
        in_specs=[pl.BlockSpec((n_exp, seq_len), lambda b: (0, b))],
        out_specs=[spec(n_lat), spec(n_lat), spec(n_ctx), spec(n_ctx), spec(LANES)],
        out_shape=[shp(n_lat), shp(n_lat), shp(n_ctx), shp(n_ctx), shp(LANES)],
        scratch_shapes=[pltpu.VMEM((1, n_exp, n_lat), F32), pltpu.VMEM((1, n_exp, n_ctx), F32)],
        compiler_params=_cparams(("parallel",), 32),
        name="routing",
    )(aff_t)


EXPERT_GROUP = 8
SLOT_ALIGN = 16
SLOT_WINDOW = 64


def _windows(starts_ref, base, stride, n, j, cap, win):
    w, rounds = [], jnp.int32(0)
    for k in range(n):
        s0 = starts_ref[base + k * stride + j]
        s1 = starts_ref[base + k * stride + j + 1]
        wk = jnp.minimum((s0 // SLOT_ALIGN) * SLOT_ALIGN, cap - win)
        w.append(wk)
        rounds = jnp.maximum(rounds, (s1 - wk + win - 1) // win)
    return w, rounds


def _gather_kernel(starts_ref, h_ref, sel_ref, xg_ref, *, n_tb, cap, win, n_exp):
    v, g, j = pl.program_id(0), pl.program_id(1), pl.program_id(2)
    ng = sel_ref.shape[2]
    base = (v * n_exp + g * ng) * (n_tb + 1)

    @pl.when(j == 0)
    def _():
        xg_ref[...] = jnp.zeros_like(xg_ref)

    w, rounds = _windows(starts_ref, base, n_tb + 1, ng, j, cap, win)
    tb = h_ref.shape[1]
    row_i = lax.broadcasted_iota(jnp.int32, (win, tb), 0).astype(F32)

    def body(r, _):
        starts, pieces = [], []
        for k in range(ng):
            lo = w[k] + r * win
            c = pl.multiple_of(jnp.minimum(lo, cap - win), SLOT_ALIGN)
            sel = sel_ref[0, 0, k:k + 1, :]
            rel = jnp.where(sel >= lo.astype(F32), sel, -1.0) - c.astype(F32)
            pieces.append(jnp.where(rel == row_i, 1.0, 0.0).astype(BF16))
            starts.append(c)
        res = _dot(jnp.concatenate(pieces, axis=0), h_ref[0])
        for k in range(ng):
            rows = pl.ds(starts[k], win)
            xg_ref[k, rows, :] = (xg_ref[k, rows, :].astype(F32)
                                  + res[k * win:(k + 1) * win, :]).astype(BF16)
        return 0

    lax.fori_loop(0, rounds, body, 0)


def _moe_gather(starts, h, sel, n_tok, cap, win):
    nb, _, d = h.shape
    n_exp = sel.shape[1]
    ng = min(EXPERT_GROUP, n_exp)
    n_tb = n_tok // ROW_BLK
    sel = sel.reshape(nb, n_exp // ng, ng, n_tok)
    return pl.pallas_call(
        functools.partial(_gather_kernel, n_tb=n_tb, cap=cap, win=win, n_exp=n_exp),
        grid_spec=pltpu.PrefetchScalarGridSpec(
            num_scalar_prefetch=1,
            grid=(nb, n_exp // ng, n_tb),
            in_specs=[
                pl.BlockSpec((1, ROW_BLK, d), lambda v, g, j, s: (v, j, 0)),
                pl.BlockSpec((1, 1, ng, ROW_BLK), lambda v, g, j, s: (v, g, 0, j)),
            ],
            out_specs=pl.BlockSpec((ng, cap, d), lambda v, g, j, s: (g, v, 0)),
        ),
        out_shape=jax.ShapeDtypeStruct((n_exp, nb * cap, d), BF16),
        compiler_params=_cparams(("parallel", "parallel", "arbitrary"), 56),
        name="moe_gather",
    )(starts, h, sel)


def _ffn_kernel(xg_ref, wg_ref, wu_ref, act_ref, wgb_ref, wub_ref):
    @pl.when(pl.program_id(2) == 0)
    def _():
        wgb_ref[...] = wg_ref[0, 0].astype(BF16)
        wub_ref[...] = wu_ref[0, 0].astype(BF16)

    x = xg_ref[0]
    a = _dot(x, wgb_ref[...])
    u = _dot(x, wub_ref[...])
    act_ref[0] = (_silu(a) * u).astype(BF16)


def _moe_ffn(xg, w_gate, w_up, layer):
    n_exp, rows, d = xg.shape
    ff = w_gate.shape[3]
    fh = _largest_tile(ff, 512, LANES)
    rblk = _largest_tile(rows, 512)
    wspec = pl.BlockSpec((1, 1, d, fh), lambda e, f, r: (layer, e, 0, f))
    return pl.pallas_call(
        _ffn_kernel,
        grid=(n_exp, ff // fh, rows // rblk),
        in_specs=[pl.BlockSpec((1, rblk, d), lambda e, f, r: (e, r, 0)), wspec, wspec],
        out_specs=pl.BlockSpec((1, rblk, fh), lambda e, f, r: (e, r, f)),
        out_shape=jax.ShapeDtypeStruct((n_exp, rows, ff), BF16),
        scratch_shapes=[pltpu.VMEM((d, fh), BF16), pltpu.VMEM((d, fh), BF16)],
        compiler_params=_cparams(("parallel", "parallel", "arbitrary"), 48),
        name="moe_ffn",
    )(xg, w_gate, w_up)


def _down_kernel(act_ref, wd_ref, y_ref, wdb_ref):
    @pl.when(pl.program_id(1) == 0)
    def _():
        wdb_ref[...] = wd_ref[0, 0].astype(BF16)

    y_ref[0] = _dot(act_ref[0], wdb_ref[...]).astype(BF16)


def _moe_down(act, w_down, layer, cap):
    n_exp, rows, ff = act.shape
    d = w_down.shape[3]
    nb = rows // cap
    return pl.pallas_call(
        _down_kernel,
        grid=(n_exp, nb),
        in_specs=[
            pl.BlockSpec((1, cap, ff), lambda e, v: (e, v, 0)),
            pl.BlockSpec((1, 1, ff, d), lambda e, v: (layer, e, 0, 0)),
        ],
        out_specs=pl.BlockSpec((1, cap, d), lambda e, v: (v, e, 0)),
        out_shape=jax.ShapeDtypeStruct((nb, n_exp * cap, d), BF16),
        scratch_shapes=[pltpu.VMEM((ff, d), BF16)],
        compiler_params=_cparams(("parallel", "arbitrary"), 48),
        name="moe_down",
    )(act, w_down)


def _combine_kernel(starts_ref, y_ref, sel_ref, aff_ref, out_ref, st_ref, *, n_tb, cap, win, n_exp):
    v, j = pl.program_id(0), pl.program_id(2)
    base = v * n_exp * (n_tb + 1)
    w, rounds = _windows(starts_ref, base, n_tb + 1, n_exp, j, cap, win)
    tb = out_ref.shape[1]
    per_grp = LANES // win
    lane = lax.broadcasted_iota(jnp.int32, (tb, LANES), 1)
    lane_f = lane.astype(F32)
    out_ref[...] = jnp.zeros_like(out_ref)

    def body(r, _):
        groups = []
        for t in range(n_exp // per_grp):
            tgt = jnp.full((tb, LANES), -1.0, F32)
            gsel = jnp.zeros((tb, LANES), F32)
            for q in range(per_grp):
                e = t * per_grp + q
                lo = w[e] + r * win
                c = pl.multiple_of(jnp.minimum(lo, cap - win), SLOT_ALIGN)
                src = pl.multiple_of(e * cap + c, SLOT_ALIGN)
                st_ref[e * win:(e + 1) * win, :] = y_ref[0, pl.ds(src, win), :]
                sel = sel_ref[0, :, e:e + 1]
                rel = jnp.where(sel >= lo.astype(F32), sel, -1.0) - c.astype(F32)
                rel = jnp.where(rel >= 0.0, rel + float(q * win), -1.0)
                in_rng = (lane >= q * win) & (lane < (q + 1) * win)
                tgt = jnp.where(in_rng, rel, tgt)
                gsel = jnp.where(in_rng, aff_ref[0, :, e:e + 1], gsel)
            groups.append(jnp.where(tgt == lane_f, gsel, 0.0))
        gmat = jnp.concatenate(groups, axis=1)
        g_hi, g_lo = _split(gmat)
        st = st_ref[...]
        out_ref[0] += _dot(g_hi, st) + _dot(g_lo, st)
        return 0

    lax.fori_loop(0, rounds, body, 0)


def _moe_combine(starts, y, sel_t, aff_t, n_tok, cap, win):
    nb, _, d = y.shape
    n_exp = sel_t.shape[2]
    n_tb = n_tok // ROW_BLK
    pw = _largest_tile(d, 1024, LANES)
    tok = pl.BlockSpec((1, ROW_BLK, n_exp), lambda v, p, j, s: (v, j, 0))
    return pl.pallas_call(
        functools.partial(_combine_kernel, n_tb=n_tb, cap=cap, win=win, n_exp=n_exp),
        grid_spec=pltpu.PrefetchScalarGridSpec(
            num_scalar_prefetch=1,
            grid=(nb, d // pw, n_tb),
            in_specs=[pl.BlockSpec((1, n_exp * cap, pw), lambda v, p, j, s: (v, 0, p)), tok, tok],
            out_specs=pl.BlockSpec((1, ROW_BLK, pw), lambda v, p, j, s: (v, j, p)),
            scratch_shapes=[pltpu.VMEM((n_exp * win, pw), BF16)],
        ),
        out_shape=jax.ShapeDtypeStruct((nb, n_tok, d), F32),
        compiler_params=_cparams(("parallel", "parallel", "arbitrary"), 48),
        name="moe_combine",
    )(starts, y, sel_t, aff_t)


def _rope_tables(pos_groups, half):
    freqs = ROPE_BASE ** (-jnp.arange(half, dtype=F32) / half)
    cos, sa, sb = [], [], []
    for pos in pos_groups:
        ang = pos.astype(F32)[:, None] * freqs[None, :]
        c, s = jnp.cos(ang), jnp.sin(ang)
        z = jnp.zeros_like(s)
        cos += [c, c]
        sa += [-s, z]
        sb += [z, s]
    return jnp.stack([jnp.concatenate(t, axis=1) for t in (cos, sa, sb)])


def _attn_tables(n_lat, n_ctx):
    s = jnp.arange(n_lat)
    tab = _rope_tables([s // GRID_W, s % GRID_W], HEAD_DIM // 4)
    ident = jnp.stack([jnp.ones((n_ctx, HEAD_DIM), F32), jnp.zeros((n_ctx, HEAD_DIM), F32),
                       jnp.zeros((n_ctx, HEAD_DIM), F32)])
    return jnp.concatenate([tab, ident], axis=1)


def _ret_tables(n_lat, n_ctx):
    s = jnp.arange(n_lat)
    t = jnp.arange(n_ctx)
    fwd = jnp.concatenate([n_ctx + s, t])
    bwd = jnp.concatenate([n_ctx + (n_lat - 1 - s), n_ctx - 1 - t])
    return jnp.concatenate([_rope_tables([fwd], HEAD_DIM // 2),
                            _rope_tables([bwd], HEAD_DIM // 2)], axis=0)


def kernel(x, c, ctx, c_ctx, w_ada, b_ada, norm_mix, norm_ffn, w_in, q_norm, k_norm, ret_log_decay,
           w_out, w_router, w_gate, w_up, w_down):
    bsz, n_lat, d = x.shape
    n_ctx = ctx.shape[1]
    seq_len = n_lat + n_ctx
    depth = w_ada.shape[0]
    n_exp = w_router.shape[2]
    n_lat_blk = n_lat // ROW_BLK
    assert n_lat % ROW_BLK == 0 and n_ctx == ROW_BLK and bsz + 1 <= 8
    cap_l = CAPACITY_FACTOR * n_lat // n_exp
    cap_c = CAPACITY_FACTOR * n_ctx // n_exp

    cond8 = jnp.zeros((8, d), F32).at[:bsz].set(c).at[bsz].set(c_ctx)
    mods = _adaln(cond8, w_ada, b_ada).reshape(depth, 8, 6, d)

    attn_tab = _attn_tables(n_lat, n_ctx)
    ret_tab = _ret_tables(n_lat, n_ctx)
    wc = _chan_mats()
    dft_l = _dft_mats(n_lat)
    dft_c = _dft_mats(n_ctx)

    o_q, o_k = 0, ATTN_WIDTH
    o_v = o_k + KV_WIDTH
    o_f = o_v + KV_WIDTH
    o_rq = o_f + FOURIER_WIDTH
    o_rk = o_rq + RET_WIDTH
    o_rv = o_rk + RET_WIDTH
    o_g = o_rv + RET_WIDTH
    o_end = o_g + 2 * RET_WIDTH
    q_scale = HEAD_DIM ** -0.5 * math.log2(math.e)

    sb_c = bsz * cap_c
    win_l = min(cap_l, SLOT_WINDOW)
    win_c = cap_c
    starts_c = jnp.tile(jnp.arange(bsz + 1, dtype=jnp.int32) * cap_c, n_exp)

    xs = jnp.concatenate([x, ctx], axis=1)
    delta_lat = delta_ctx = None
    for l in range(depth):
        xs, h = _norm(xs, mods[l], norm_mix[l], n_lat_blk, delta_lat, delta_ctx,
                      mods[l - 1] if l else None)
        h2d = h.reshape(bsz * seq_len, d)
        wl = w_in[l].astype(BF16)
        q = _proj(h2d, wl[:, o_q:o_k], seq_len, "qk", gain=q_norm[l], tab=attn_tab,
                  n_heads=ATTN_HEADS, scale=q_scale, half=HEAD_DIM // 4)
        k = _proj(h2d, wl[:, o_k:o_v], seq_len, "qk", gain=k_norm[l], tab=attn_tab,
                  n_heads=ATTN_KV_HEADS, scale=1.0, half=HEAD_DIM // 4)
        v, f = _proj(h2d, wl[:, o_v:o_rq], seq_len, "split", split_at=KV_WIDTH)
        rq = _proj(h2d, wl[:, o_rq:o_rk], seq_len, "ret", tab=ret_tab, n_heads=RET_HEADS,
                   scale=1.0, half=HEAD_DIM // 2)
        rk = _proj(h2d, wl[:, o_rk:o_rv], seq_len, "ret", tab=ret_tab, n_heads=RET_HEADS,
                   scale=HEAD_DIM ** -0.5, half=HEAD_DIM // 2)
        rv = _proj(h2d, wl[:, o_rv:o_g], seq_len, "plain")
        sg = _proj(h2d, wl[:, o_g:o_end], seq_len, "silu")

        a = _attention(q, k, v, bsz, seq_len, n_lat)
        fm_lat = _fourier(f, bsz, seq_len, 0, n_lat, wc, dft_l)
        fm_ctx = _fourier(f, bsz, seq_len, n_lat, n_ctx, wc, dft_c)
        o_f, o_b = _retention(rq, rk, rv, ret_log_decay[l], bsz, seq_len, n_lat)

        xs, h2, aff_t = _outproj(a, fm_lat, fm_ctx, o_f, o_b, sg, w_out[l].astype(BF16), xs, mods[l],
                                 norm_ffn[l], w_router[l].T, n_lat_blk)

        sel_l, gat_l, sel_c, gat_c, st_l = _routing(aff_t, bsz, seq_len, n_lat, cap_l, cap_c)
        starts_l = st_l[:, :, :n_lat_blk + 1].astype(jnp.int32).reshape(-1)
        merge = lambda t: t.transpose(1, 0, 2).reshape(1, n_exp, bsz * n_ctx)
        tok_major = lambda t: t.transpose(0, 2, 1)
        sel_c, gat_c = merge(sel_c), merge(gat_c)
        h_ctx = h2[:, n_lat:, :].reshape(1, bsz * n_ctx, d)

        xg_l = _moe_gather(starts_l, h2, sel_l, n_lat, cap_l, win_l)
        xg_c = _moe_gather(starts_c, h_ctx, sel_c, bsz * n_ctx, sb_c, win_c)
        y_l = _moe_down(_moe_ffn(xg_l, w_gate, w_up, l), w_down, l, cap_l)
        y_c = _moe_down(_moe_ffn(xg_c, w_gate, w_up, l), w_down, l, sb_c)
        delta_lat = _moe_combine(starts_l, y_l, tok_major(sel_l), tok_major(gat_l), n_lat, cap_l,
                                 win_l)
        delta_ctx = _moe_combine(starts_c, y_c, tok_major(sel_c), tok_major(gat_c), bsz * n_ctx,
                                 sb_c, win_c).reshape(bsz * n_ctx, d)

    out, _ = _norm(xs, None, None, n_lat_blk, delta_lat, delta_ctx, mods[depth - 1],
                   want_h=False, lat_only=True)
    return out
```

```python
import functools
import math

import jax
import jax.numpy as jnp
from jax import lax
from jax.experimental import pallas as pl
from jax.experimental.pallas import tpu as pltpu

HEAD_DIM = 128
ATTN_HEADS = 8
ATTN_KV_HEADS = 2
ATTN_GROUP = ATTN_HEADS // ATTN_KV_HEADS
FOURIER_GROUPS = 4
RET_HEADS = 4
RET_CHUNK = 128
GRID_W = 64
ROPE_BASE = 10000.0
EPS = 1e-6
CAPACITY_FACTOR = 2

ATTN_WIDTH = ATTN_HEADS * HEAD_DIM
KV_WIDTH = ATTN_KV_HEADS * HEAD_DIM
FOURIER_WIDTH = FOURIER_GROUPS * HEAD_DIM
RET_WIDTH = RET_HEADS * HEAD_DIM

ROW_BLK = 256
LANES = 128
MIB = 1024 * 1024

F32 = jnp.float32
BF16 = jnp.bfloat16


def _cparams(sem, vmem_mib):
    return pltpu.CompilerParams(dimension_semantics=sem, vmem_limit_bytes=vmem_mib * MIB)


def _largest_tile(n, cap, mult=8):
    best = mult
    for t in range(mult, min(n, cap) + 1, mult):
        if n % t == 0:
            best = t
    return best


def _split(a):
    hi = a.astype(BF16)
    lo = (a - hi.astype(F32)).astype(BF16)
    return hi, lo


def _dot(a, b):
    return jnp.dot(a, b, preferred_element_type=F32)


def _dot_nt(a, b):
    return lax.dot_general(a, b, (((1,), (1,)), ((), ())), preferred_element_type=F32)


def _dot_tn(a, b):
    return lax.dot_general(a, b, (((0,), (0,)), ((), ())), preferred_element_type=F32)


def _silu(a):
    return a / (1.0 + jnp.exp(-a))


def _adaln_kernel(c_ref, w_ref, b_ref, o_ref):
    s = _silu(c_ref[...])
    sh, sl = _split(s)
    wh, wl = _split(w_ref[0])
    o_ref[0] = _dot(sh, wh) + (_dot(sh, wl) + _dot(sl, wh)) + b_ref[0]


def _adaln(cond8, w_ada, b_ada):
    depth, d, n = w_ada.shape
    tn = _largest_tile(n, 768, LANES)
    return pl.pallas_call(
        _adaln_kernel,
        grid=(depth, n // tn),
        in_specs=[
            pl.BlockSpec((8, d), lambda l, j: (0, 0)),
            pl.BlockSpec((1, d, tn), lambda l, j: (l, 0, j)),
            pl.BlockSpec((1, 1, tn), lambda l, j: (l, 0, j)),
        ],
        out_specs=pl.BlockSpec((1, 8, tn), lambda l, j: (l, 0, j)),
        out_shape=jax.ShapeDtypeStruct((depth, 8, n), F32),
        compiler_params=_cparams(("parallel", "parallel"), 48),
        name="adaln",
    )(cond8, w_ada, b_ada.reshape(depth, 1, n))


def _rms_mod(x, g, shift, scale):
    y = x * lax.rsqrt(jnp.mean(x * x, axis=-1, keepdims=True) + EPS)
    return (y * g) * (1.0 + scale) + shift


def _norm_kernel(*refs, has_delta, want_h, n_lat_blk):
    refs = list(refs)
    x_ref = refs.pop(0)
    x = x_ref[0]
    if has_delta:
        dl_ref, dc_ref, pm_ref = refs.pop(0), refs.pop(0), refs.pop(0)
        is_ctx = pl.program_id(1) >= n_lat_blk
        delta = jnp.where(is_ctx, dc_ref[...], dl_ref[0])
        x = x + pm_ref[0, 5:6, :] * delta
    if want_h:
        m_ref, g_ref = refs.pop(0), refs.pop(0)
    if has_delta:
        xo_ref = refs.pop(0)
        xo_ref[0] = x
    if want_h:
        h_ref = refs.pop(0)
        h_ref[0] = _rms_mod(x, g_ref[...], m_ref[0, 0:1, :], m_ref[0, 1:2, :]).astype(BF16)


def _norm(x, mods, g, n_lat_blk, delta_lat=None, delta_ctx=None, prev_mods=None,
          want_h=True, lat_only=False):
    bsz, l, d = x.shape
    n_blk = n_lat_blk if lat_only else l // ROW_BLK
    has_delta = delta_lat is not None

    def mod_row(b, t):
        return (jnp.where(t >= n_lat_blk, bsz, b), 0, 0)

    xspec = pl.BlockSpec((1, ROW_BLK, d), lambda b, t: (b, t, 0))
    in_specs, args = [xspec], [x]
    if has_delta:
        in_specs += [
            pl.BlockSpec((1, ROW_BLK, d), lambda b, t: (b, jnp.minimum(t, n_lat_blk - 1), 0)),
            pl.BlockSpec((ROW_BLK, d), lambda b, t: (b, 0)),
            pl.BlockSpec((1, 6, d), mod_row),
        ]
        args += [delta_lat, delta_ctx, prev_mods]
    if want_h:
        in_specs += [pl.BlockSpec((1, 6, d), mod_row), pl.BlockSpec((1, d), lambda b, t: (0, 0))]
        args += [mods, g.reshape(1, d)]
    out_specs, out_shape = [], []
    rows = n_blk * ROW_BLK
    if has_delta:
        out_specs.append(xspec)
        out_shape.append(jax.ShapeDtypeStruct((bsz, rows, d), F32))
    if want_h:
        out_specs.append(xspec)
        out_shape.append(jax.ShapeDtypeStruct((bsz, rows, d), BF16))
    outs = pl.pallas_call(
        functools.partial(_norm_kernel, has_delta=has_delta, want_h=want_h, n_lat_blk=n_lat_blk),
        grid=(bsz, n_blk),
        in_specs=in_specs,
        out_specs=out_specs,
        out_shape=out_shape,
        compiler_params=_cparams(("parallel", "parallel"), 32),
        name="norm",
    )(*args)
    outs = list(outs)
    x_new = outs.pop(0) if has_delta else x
    h = outs.pop(0) if want_h else None
    return x_new, h


def _rope(z, tab_ref, k, half):
    return (z * tab_ref[k] + pltpu.roll(z, LANES - half, 1) * tab_ref[k + 1]
            + pltpu.roll(z, half, 1) * tab_ref[k + 2])


def _proj_kernel(*refs, flavor, n_heads, scale, half, split_at):
    h_ref, w_ref = refs[0], refs[1]
    acc = _dot(h_ref[...], w_ref[...])
    if flavor == "plain":
        refs[2][...] = acc.astype(BF16)
    elif flavor == "split":
        refs[2][...] = acc[:, :split_at].astype(BF16)
        refs[3][...] = acc[:, split_at:].astype(BF16)
    elif flavor == "silu":
        refs[2][...] = _silu(acc).astype(BF16)
    elif flavor == "qk":
        gain_ref, tab_ref, o_ref = refs[2], refs[3], refs[4]
        for hd in range(n_heads):
            sl = slice(hd * HEAD_DIM, (hd + 1) * HEAD_DIM)
            z = acc[:, sl]
            z = (z * lax.rsqrt(jnp.mean(z * z, axis=-1, keepdims=True) + EPS)) * gain_ref[...]
            o_ref[:, sl] = (_rope(z, tab_ref, 0, half) * scale).astype(BF16)
    elif flavor == "ret":
        tab_ref, o_ref = refs[2], refs[3]
        width = n_heads * HEAD_DIM
        for hd in range(n_heads):
            sl = slice(hd * HEAD_DIM, (hd + 1) * HEAD_DIM)
            z = acc[:, sl] * scale
            o_ref[:, sl] = _rope(z, tab_ref, 0, half).astype(BF16)
            o_ref[:, width + hd * HEAD_DIM: width + (hd + 1) * HEAD_DIM] = (
                _rope(z, tab_ref, 3, half).astype(BF16))
    else:
        raise ValueError(flavor)


def _proj(h2d, w, seq_len, flavor, *, gain=None, tab=None, n_heads=0, scale=1.0, half=0,
          split_at=0):
    t_rows, d = h2d.shape
    n = w.shape[1]
    tm = _largest_tile(seq_len, 1088)
    per_seq = seq_len // tm
    in_specs = [pl.BlockSpec((tm, d), lambda i: (i, 0)), pl.BlockSpec((d, n), lambda i: (0, 0))]
    args = [h2d, w]
    if flavor == "qk":
        in_specs.append(pl.BlockSpec((1, HEAD_DIM), lambda i: (0, 0)))
        args.append(gain.reshape(1, HEAD_DIM))
    if flavor in ("qk", "ret"):
        ntab = tab.shape[0]
        in_specs.append(pl.BlockSpec((ntab, tm, HEAD_DIM), lambda i: (0, i % per_seq, 0)))
        args.append(tab)
    if flavor == "split":
        widths = [split_at, n - split_at]
    elif flavor == "ret":
        widths = [2 * n]
    else:
        widths = [n]
    out_specs = [pl.BlockSpec((tm, wd), lambda i: (i, 0)) for wd in widths]
    out_shape = [jax.ShapeDtypeStruct((t_rows, wd), BF16) for wd in widths]
    outs = pl.pallas_call(
        functools.partial(_proj_kernel, flavor=flavor, n_heads=n_heads, scale=scale, half=half,
                          split_at=split_at),
        grid=(t_rows // tm,),
        in_specs=in_specs,
        out_specs=out_specs,
        out_shape=out_shape,
        compiler_params=_cparams(("parallel",), 48),
        name="proj_" + flavor,
    )(*args)
    return outs if len(outs) > 1 else outs[0]


SUBLANES = 8


SAFE_SHIFT = 50.0


def _lane_fold(t, op):
    return functools.reduce(op, [t[:, i * LANES:(i + 1) * LANES]
                                 for i in range(t.shape[1] // LANES)])


def _attn_head_bounded(q, m, k_ref, v_ref, p_ref, k_lo, n_keys, tk):
    lp = None
    for c in range(n_keys // tk):
        s = _dot_nt(q, k_ref[k_lo + c * tk:k_lo + (c + 1) * tk, :])
        p = jnp.exp2(s - m)
        part = _lane_fold(p, jnp.add)
        lp = part if lp is None else lp + part
        p_ref[:, c * tk:(c + 1) * tk] = p.astype(BF16)
    l = jnp.sum(lp, axis=-1, keepdims=True)
    return _dot(p_ref[:, :n_keys], v_ref[k_lo:k_lo + n_keys, :]) / l


def _attn_head_exact(q, k_ref, v_ref, s_ref, p_ref, k_lo, n_keys, tk):
    s_ref[:, :n_keys] = _dot_nt(q, k_ref[k_lo:k_lo + n_keys, :])
    n_chunks = n_keys // tk
    mp = None
    for c in range(n_chunks):
        part = _lane_fold(s_ref[:, c * tk:(c + 1) * tk], jnp.maximum)
        mp = part if mp is None else jnp.maximum(mp, part)
    m = jnp.max(mp, axis=-1, keepdims=True)
    lp = None
    for c in range(n_chunks):
        p = jnp.exp2(s_ref[:, c * tk:(c + 1) * tk] - m)
        part = _lane_fold(p, jnp.add)
        lp = part if lp is None else lp + part
        p_ref[:, c * tk:(c + 1) * tk] = p.astype(BF16)
    l = jnp.sum(lp, axis=-1, keepdims=True)
    return _dot(p_ref[:, :n_keys], v_ref[k_lo:k_lo + n_keys, :]) / l


def _attn_kernel(q_ref, k_ref, v_ref, o_ref, s_ref, p_ref, kmax_ref, *, tk, n_lat, n_ctx,
                 n_lat_qblk):
    i = pl.program_id(2)
    is_ctx = i >= n_lat_qblk

    @pl.when(i == 0)
    def _():
        kk = k_ref[...].astype(F32)
        kn2 = jnp.max(jnp.sum(kk * kk, axis=-1, keepdims=True), axis=0, keepdims=True)
        kmax_ref[...] = jnp.broadcast_to(jnp.sqrt(kn2), kmax_ref.shape)

    heads = [slice(hd * HEAD_DIM, (hd + 1) * HEAD_DIM) for hd in range(ATTN_GROUP)]
    bounds = []
    for sl in heads:
        qf = q_ref[:, sl].astype(F32)
        qn = jnp.sqrt(jnp.sum(qf * qf, axis=-1, keepdims=True))
        bounds.append(qn * kmax_ref[0:1, 0:1] * (1.0 + 2.0 ** -10))
    worst = functools.reduce(jnp.maximum, [jnp.max(b) for b in bounds])
    bounded_ok = worst <= SAFE_SHIFT

    def run(k_lo, n_keys, bounded):
        for hd, sl in enumerate(heads):
            if bounded:
                o = _attn_head_bounded(q_ref[:, sl], bounds[hd], k_ref, v_ref, p_ref.at[hd], k_lo,
                                       n_keys, tk)
            else:
                o = _attn_head_exact(q_ref[:, sl], k_ref, v_ref, s_ref.at[hd], p_ref.at[hd], k_lo,
                                     n_keys, tk)
            o_ref[:, sl] = o.astype(BF16)

    for ctx_case, (k_lo, n_keys) in ((False, (0, n_lat + n_ctx)), (True, (n_lat, n_ctx))):
        for bounded in (True, False):
            @pl.when((is_ctx == ctx_case) & (bounded_ok == bounded))
            def _(k_lo=k_lo, n_keys=n_keys, bounded=bounded):
                run(k_lo, n_keys, bounded)


def _attention(q, k, v, bsz, seq_len, n_lat):
    tq = ROW_BLK
    tk = ROW_BLK
    per_seq = seq_len // tq
    gw = ATTN_GROUP * HEAD_DIM
    return pl.pallas_call(
        functools.partial(_attn_kernel, tk=tk, n_lat=n_lat, n_ctx=seq_len - n_lat,
                          n_lat_qblk=n_lat // tq),
        grid=(bsz, ATTN_KV_HEADS, per_seq),
        in_specs=[
            pl.BlockSpec((tq, gw), lambda b, g, i: (b * per_seq + i, g)),
            pl.BlockSpec((seq_len, HEAD_DIM), lambda b, g, i: (b, g)),
            pl.BlockSpec((seq_len, HEAD_DIM), lambda b, g, i: (b, g)),
        ],
        out_specs=pl.BlockSpec((tq, gw), lambda b, g, i: (b * per_seq + i, g)),
        out_shape=jax.ShapeDtypeStruct(q.shape, BF16),
        scratch_shapes=[pltpu.VMEM((ATTN_GROUP, tq, seq_len), F32),
                        pltpu.VMEM((ATTN_GROUP, tq, seq_len), BF16),
                        pltpu.VMEM((SUBLANES, LANES), F32)],
        compiler_params=_cparams(("parallel", "parallel", "arbitrary"), 48),
        name="attention",
    )(q, k, v)


def _fourier_a_kernel(f_ref, wc_ref, o_ref):
    g = _dot(f_ref[...], wc_ref[...])
    wdt = f_ref.shape[1]
    o_ref[0, 0] = g[:, :wdt].astype(BF16)
    o_ref[0, 1] = g[:, wdt:].astype(BF16)


def _fourier_b_kernel(m_ref, g_ref, o_ref):
    o_ref[0] = _dot(m_ref[...], g_ref[0]).astype(BF16)


FFT_COLS = 64
FFT_PER_STEP = 4


def _fft1_kernel(g_ref, m_ref, tc_ref, ts_ref, o_ref):
    n1 = g_ref.shape[2]
    width = o_ref.shape[4]
    a = _dot(m_ref[...], jnp.concatenate([g_ref[0, 0], g_ref[0, 1]], axis=0))
    ar, ai = a[:n1], a[n1:]
    for j in range(o_ref.shape[2]):
        tc = tc_ref[0, :, j * LANES:(j + 1) * LANES]
        ts = ts_ref[0, :, j * LANES:(j + 1) * LANES]
        for q in range(width // LANES):
            src = slice(j * width + q * LANES, j * width + (q + 1) * LANES)
            dst = slice(q * LANES, (q + 1) * LANES)
            r, i = ar[:, src], ai[:, src]
            o_ref[0, 0, j, :, dst] = (r * tc + i * ts).astype(BF16)
            o_ref[0, 1, j, :, dst] = (i * tc - r * ts).astype(BF16)


def _fft2_kernel(b_ref, m_ref, o_ref):
    o_ref[0] = _dot(m_ref[...], jnp.concatenate([b_ref[0, 0], b_ref[0, 1]], axis=0)).astype(BF16)


def _fourier_two_stage(g, fft):
    m1, tc, ts, m3 = fft
    bsz, _, n, width = g.shape
    n2 = m3.shape[0]
    n1 = n // n2
    per = FFT_PER_STEP
    g = g.reshape(bsz, 2, n1, n2 * width)
    b = pl.pallas_call(
        _fft1_kernel,
        grid=(bsz, n2 // per),
        in_specs=[
            pl.BlockSpec((1, 2, n1, per * width), lambda b, k: (b, 0, 0, k)),
            pl.BlockSpec(m1.shape, lambda b, k: (0, 0)),
            pl.BlockSpec((1, n1, per * LANES), lambda b, k: (k, 0, 0)),
            pl.BlockSpec((1, n1, per * LANES), lambda b, k: (k, 0, 0)),
        ],
        out_specs=pl.BlockSpec((1, 2, per, n1, width), lambda b, k: (b, 0, k, 0, 0)),
        out_shape=jax.ShapeDtypeStruct((bsz, 2, n2, n1, width), BF16),
        compiler_params=_cparams(("parallel", "parallel"), 32),
        name="fourier_fft1",
    )(g, m1, tc, ts)
    b = b.reshape(bsz, 2, n2, n1 * width)
    lanes = _largest_tile(n1 * width, 4096, LANES)
    out = pl.pallas_call(
        _fft2_kernel,
        grid=(bsz, n1 * width // lanes),
        in_specs=[
            pl.BlockSpec((1, 2, n2, lanes), lambda b, k: (b, 0, 0, k)),
            pl.BlockSpec(m3.shape, lambda b, k: (0, 0)),
        ],
        out_specs=pl.BlockSpec((1, n2, lanes), lambda b, k: (b, 0, k)),
        out_shape=jax.ShapeDtypeStruct((bsz, n2, n1 * width), BF16),
        compiler_params=_cparams(("parallel", "parallel"), 32),
        name="fourier_fft2",
    )(b, m3)
    return out.reshape(bsz, n, width)


def _fft_mats(n):
    n2 = FFT_COLS
    n1 = n // n2
    per = FFT_PER_STEP

    def cs(rows, cols, period):
        ang = ((rows[:, None] * cols[None, :]) % period).astype(F32) * (2.0 * math.pi / period)
        return jnp.cos(ang), jnp.sin(ang)

    i1, i2 = jnp.arange(n1, dtype=jnp.int32), jnp.arange(n2, dtype=jnp.int32)
    c1, s1 = cs(i1, i1, n1)
    m1 = (jnp.block([[c1, s1], [-s1, c1]]) * n1 ** -0.5).astype(BF16)
    c3, s3 = cs(i2, i2, n2)
    m3 = (jnp.concatenate([c3, s3], axis=1) * n2 ** -0.5).astype(BF16)
    tc, ts = cs(i1, i2, n)
    expand = lambda t: jnp.repeat(t.T.reshape(n2 // per, per, n1).transpose(0, 2, 1), LANES,
                                  axis=2)
    return m1, expand(tc), expand(ts), m3


def _fourier(f2d, bsz, seq_len, row_off, n, wc, dft=None, fft=None):
    per_seq = seq_len // ROW_BLK
    off_blk = row_off // ROW_BLK
    nb = n // ROW_BLK
    width = f2d.shape[1]
    g = pl.pallas_call(
        _fourier_a_kernel,
        grid=(bsz, nb),
        in_specs=[
            pl.BlockSpec((ROW_BLK, width), lambda b, t: (b * per_seq + off_blk + t, 0)),
            pl.BlockSpec((width, 2 * width), lambda b, t: (0, 0)),
        ],
        out_specs=pl.BlockSpec((1, 2, ROW_BLK, width), lambda b, t: (b, 0, t, 0)),
        out_shape=jax.ShapeDtypeStruct((bsz, 2, n, width), BF16),
        compiler_params=_cparams(("parallel", "parallel"), 32),
        name="fourier_chan",
    )(f2d, wc)
    if fft is not None:
        return _fourier_two_stage(g, fft)
    g = g.reshape(bsz, 2 * n, width)
    return pl.pallas_call(
        _fourier_b_kernel,
        grid=(nb, bsz),
        in_specs=[
            pl.BlockSpec((ROW_BLK, 2 * n), lambda i, b: (i, 0)),
            pl.BlockSpec((1, 2 * n, width), lambda i, b: (b, 0, 0)),
        ],
        out_specs=pl.BlockSpec((1, ROW_BLK, width), lambda i, b: (b, i, 0)),
        out_shape=jax.ShapeDtypeStruct((bsz, n, width), BF16),
        compiler_params=_cparams(("parallel", "parallel"), 48),
        name="fourier_pos",
    )(dft, g)


def _dft_mats(n):
    i = jnp.arange(n, dtype=jnp.int32)
    prod = (i[:, None] * i[None, :]) % n
    ang = prod.astype(F32) * (2.0 * math.pi / n)
    s = n ** -0.5
    return jnp.concatenate([jnp.cos(ang) * s, jnp.sin(ang) * s], axis=1).astype(BF16)


def _chan_mats():
    i = jnp.arange(HEAD_DIM, dtype=jnp.int32)
    ang = ((i[:, None] * i[None, :]) % HEAD_DIM).astype(F32) * (2.0 * math.pi / HEAD_DIM)
    s = HEAD_DIM ** -0.5
    eye = jnp.eye(FOURIER_GROUPS, dtype=F32)
    c = jnp.kron(eye, jnp.cos(ang) * s)
    sn = jnp.kron(eye, -jnp.sin(ang) * s)
    return jnp.concatenate([c, sn], axis=1).astype(BF16)


RET_HEADS_PER_STEP = 2


def _ret_kernel(ld_ref, qf_ref, qb_ref, kf_ref, kb_ref, v_ref, of_ref, ob_ref, state_ref,
                *, n_lat_chunks, n_ctx_chunks):
    c = RET_CHUNK
    n_all = n_lat_chunks + n_ctx_chunks
    h0 = pl.program_id(1) * RET_HEADS_PER_STEP
    ii = lax.broadcasted_iota(jnp.int32, (c, c), 0).astype(F32)
    jj = lax.broadcasted_iota(jnp.int32, (c, c), 1).astype(F32)
    ri = lax.broadcasted_iota(jnp.int32, (c, 1), 0).astype(F32)
    state_ref[...] = jnp.zeros_like(state_ref)

    chains = []
    for hh in range(RET_HEADS_PER_STEP):
        sl = slice(hh * HEAD_DIM, (hh + 1) * HEAD_DIM)
        for d, (q_ref, k_ref, o_ref) in enumerate(((qf_ref, kf_ref, of_ref),
                                                   (qb_ref, kb_ref, ob_ref))):
            lg = ld_ref[d, h0 + hh]
            diff = ii - jj if d == 0 else jj - ii
            intra = jnp.where(diff >= 0, jnp.exp(lg * jnp.maximum(diff, 0.0)), 0.0)
            q_dec = jnp.exp(lg * (ri + 1.0 if d == 0 else c - ri))
            k_dec = jnp.exp(lg * (c - 1.0 - ri if d == 0 else ri))
            c_dec = jnp.exp(jnp.full((1, HEAD_DIM), lg * c, F32))
            chains.append((len(chains), q_ref, k_ref, o_ref, sl, (intra, q_dec, k_dec, c_dec), d))

    def body(s, _):
        in_ctx = s < n_ctx_chunks
        for slot, q_ref, k_ref, o_ref, sl, (intra, q_dec, k_dec, c_dec), d in chains:
            if d == 0:
                chunk = jnp.where(in_ctx, n_lat_chunks + s, s - n_ctx_chunks)
            else:
                chunk = n_all - 1 - s
            off = pl.multiple_of(chunk * c, c)
            q = q_ref[pl.ds(off, c), sl]
            k = k_ref[pl.ds(off, c), sl]
            v = v_ref[pl.ds(off, c), sl]
            st = state_ref[slot]
            sc = _dot_nt(q, k) * intra
            o = _dot(sc.astype(BF16), v) + _dot(q, st.astype(BF16)) * q_dec
            kd = (k.astype(F32) * k_dec).astype(BF16)
            state_ref[slot] = st * c_dec + _dot_tn(kd, v)
            mu = jnp.mean(o, axis=-1, keepdims=True)
            var = jnp.mean(jnp.square(o - mu), axis=-1, keepdims=True)
            o_ref[pl.ds(off, c), sl] = (o - mu) * lax.rsqrt(var + EPS)
        return 0

    lax.fori_loop(0, n_all, body, 0)


def _retention(rq, rk, rv, log_decay, bsz, seq_len, n_lat):
    hw = RET_HEADS_PER_STEP * HEAD_DIM
    n_hb = RET_HEADS // RET_HEADS_PER_STEP
    fwd = pl.BlockSpec((seq_len, hw), lambda b, h: (b, h))
    bwd = pl.BlockSpec((seq_len, hw), lambda b, h: (b, n_hb + h))
    return pl.pallas_call(
        functools.partial(_ret_kernel, n_lat_chunks=n_lat // RET_CHUNK,
                          n_ctx_chunks=(seq_len - n_lat) // RET_CHUNK),
        grid=(bsz, n_hb),
        in_specs=[pl.BlockSpec(memory_space=pltpu.SMEM), fwd, bwd, fwd, bwd, fwd],
        out_specs=[fwd, fwd],
        out_shape=[jax.ShapeDtypeStruct((bsz * seq_len, RET_WIDTH), F32)] * 2,
        scratch_shapes=[pltpu.VMEM((2 * RET_HEADS_PER_STEP, HEAD_DIM, HEAD_DIM), F32)],
        compiler_params=_cparams(("parallel", "parallel"), 56),
        name="retention",
    )(log_decay, rq, rq, rk, rk, rv)


def _outproj_kernel(a_ref, fl_ref, fc_ref, of_ref, ob_ref, sg_ref, w_ref, x_ref, m_ref, g_ref, wr_ref,
                    xo_ref, h_ref, aff_ref, *, n_lat_blk):
    is_ctx = pl.program_id(1) >= n_lat_blk
    fm = jnp.where(is_ctx, fc_ref[0], fl_ref[0])
    rw = RET_WIDTH
    r = (sg_ref[:, :rw].astype(F32) * of_ref[...]
         + sg_ref[:, rw:].astype(F32) * ob_ref[...]).astype(BF16)
    a0, f0 = ATTN_WIDTH, ATTN_WIDTH + FOURIER_WIDTH
    y = (_dot(a_ref[...], w_ref[:a0, :]) + _dot(fm, w_ref[a0:f0, :])) + _dot(r, w_ref[f0:, :])
    x = x_ref[0] + m_ref[0, 2:3, :] * y
    xo_ref[0] = x
    h = _rms_mod(x, g_ref[...], m_ref[0, 3:4, :], m_ref[0, 4:5, :])
    h_ref[0] = h.astype(BF16)
    hh, hl = _split(h)
    wh, wl = _split(wr_ref[...])
    lt = _dot_nt(wh, hh) + (_dot_nt(wh, hl) + _dot_nt(wl, hh))
    e = jnp.exp(lt - jnp.max(lt, axis=0, keepdims=True))
    aff_ref[...] = e / jnp.sum(e, axis=0, keepdims=True)


def _outproj(a, fm_lat, fm_ctx, o_f, o_b, sg, w_out, x, mods, g, wr_t, n_lat_blk):
    bsz, l, d = x.shape
    per_seq = l // ROW_BLK
    n_exp = wr_t.shape[0]
    fw = fm_lat.shape[-1]

    def flat(b, t):
        return (b * per_seq + t, 0)

    return pl.pallas_call(
        functools.partial(_outproj_kernel, n_lat_blk=n_lat_blk),
        grid=(bsz, per_seq),
        in_specs=[
            pl.BlockSpec((ROW_BLK, a.shape[1]), flat),
            pl.BlockSpec((1, ROW_BLK, fw), lambda b, t: (b, jnp.minimum(t, n_lat_blk - 1), 0)),
            pl.BlockSpec((1, ROW_BLK, fw), lambda b, t: (b, 0, 0)),
            pl.BlockSpec((ROW_BLK, o_f.shape[1]), flat),
            pl.BlockSpec((ROW_BLK, o_b.shape[1]), flat),
            pl.BlockSpec((ROW_BLK, sg.shape[1]), flat),
            pl.BlockSpec(w_out.shape, lambda b, t: (0, 0)),
            pl.BlockSpec((1, ROW_BLK, d), lambda b, t: (b, t, 0)),
            pl.BlockSpec((1, 6, d), lambda b, t: (jnp.where(t >= n_lat_blk, bsz, b), 0, 0)),
            pl.BlockSpec((1, d), lambda b, t: (0, 0)),
            pl.BlockSpec(wr_t.shape, lambda b, t: (0, 0)),
        ],
        out_specs=[
            pl.BlockSpec((1, ROW_BLK, d), lambda b, t: (b, t, 0)),
            pl.BlockSpec((1, ROW_BLK, d), lambda b, t: (b, t, 0)),
            pl.BlockSpec((n_exp, ROW_BLK), lambda b, t: (0, b * per_seq + t)),
        ],
        out_shape=[
            jax.ShapeDtypeStruct((bsz, l, d), F32),
            jax.ShapeDtypeStruct((bsz, l, d), BF16),
            jax.ShapeDtypeStruct((n_exp, bsz * l), F32),
        ],
        compiler_params=_cparams(("parallel", "parallel"), 48),
        name="outproj",
    )(a, fm_lat, fm_ctx, o_f, o_b, sg, w_out, x, mods, g.reshape(1, d), wr_t)


def _cumsum_lanes(m, out_ref, fin):
    n_exp, n = m.shape
    tri = (lax.broadcasted_iota(jnp.int32, (LANES, LANES), 0)
           <= lax.broadcasted_iota(jnp.int32, (LANES, LANES), 1)).astype(BF16)
    run = jnp.zeros((n_exp, 1), F32)
    befores = []
    for k in range(n // LANES):
        befores.append(run)
        sl = slice(k * LANES, (k + 1) * LANES)
        cnt = _dot(m[:, sl].astype(BF16), tri) + run
        out_ref[0, :, sl] = fin(cnt, sl)
        run = cnt[:, LANES - 1:LANES]
    befores.append(run)
    return befores


def _select(seg, cap, slot_off, sel_ref, tmp_ref):
    bits = pltpu.bitcast(seg, jnp.int32)
    n_exp = seg.shape[0]

    def body(it, t):
        tt = t | lax.shift_left(jnp.int32(1), 30 - it)
        cnt = jnp.sum(jnp.where(bits >= tt, 1.0, 0.0), axis=1, keepdims=True)
        return jnp.where(cnt >= cap, tt, t)

    t = lax.fori_loop(0, 31, body, jnp.zeros((n_exp, 1), jnp.int32))
    gt = bits > t
    eq = bits == t
    need = cap - jnp.sum(jnp.where(gt, 1.0, 0.0), axis=1, keepdims=True)
    eqf = jnp.where(eq, 1.0, 0.0)
    _cumsum_lanes(eqf, tmp_ref, lambda cnt, sl: cnt)
    take = eq & (tmp_ref[0] - eqf < need)
    mask = gt | take
    maskf = jnp.where(mask, 1.0, 0.0)
    return _cumsum_lanes(
        maskf, sel_ref,
        lambda cnt, sl: jnp.where(maskf[:, sl] > 0.5, cnt - 1.0 + slot_off, -1.0))


def _routing_kernel(aff_ref, sel_l, gat_l, sel_c, gat_c, st_l, tmp_l, tmp_c, *, n_lat, cap_l, cap_c):
    b = pl.program_id(0)
    a = aff_ref[...]
    lat = a[:, :n_lat]
    ctx = a[:, n_lat:]
    gat_l[0] = lat
    gat_c[0] = ctx
    befores = _select(lat, float(cap_l), 0.0, sel_l, tmp_l)
    _select(ctx, float(cap_c), (b * cap_c).astype(F32), sel_c, tmp_c)
    lane = lax.broadcasted_iota(jnp.int32, (a.shape[0], LANES), 1)
    st = jnp.zeros((a.shape[0], LANES), F32)
    per_blk = ROW_BLK // LANES
    for j in range(n_lat // ROW_BLK + 1):
        st = jnp.where(lane == j, befores[j * per_blk], st)
    st_l[0] = st


def _routing(aff_t, bsz, seq_len, n_lat, cap_l, cap_c):
    n_exp = aff_t.shape[0]
    n_ctx = seq_len - n_lat
    shp = lambda n: jax.ShapeDtypeStruct((bsz, n_exp, n), F32)
    spec = lambda n: pl.BlockSpec((1, n_exp, n), lambda b: (b, 0, 0))
    return pl.pallas_call(
        functools.partial(_routing_kernel, n_lat=n_lat, cap_l=cap_l, cap_c=cap_c),
        grid=(bsz,),
        in_specs=[pl.BlockSpec((n_exp, seq_len), lambda b: (0, b))],
        out_specs=[spec(n_lat), spec(n_lat), spec(n_ctx), spec(n_ctx), spec(LANES)],
        out_shape=[shp(n_lat), shp(n_lat), shp(n_ctx), shp(n_ctx), shp(LANES)],
        scratch_shapes=[pltpu.VMEM((1, n_exp, n_lat), F32), pltpu.VMEM((1, n_exp, n_ctx), F32)],
        compiler_params=_cparams(("parallel",), 32),
        name="routing",
    )(aff_t)


EXPERT_GROUP = 8
SLOT_ALIGN = 16
SLOT_WINDOW = 64


def _windows(starts_ref, base, stride, n, j, cap, win):
    w, rounds = [], jnp.int32(0)
    for k in range(n):
        s0 = starts_ref[base + k * stride + j]
        s1 = starts_ref[base + k * stride + j + 1]
        wk = jnp.minimum((s0 // SLOT_ALIGN) * SLOT_ALIGN, cap - win)
        w.append(wk)
        rounds = jnp.maximum(rounds, (s1 - wk + win - 1) // win)
    return w, rounds


def _gather_kernel(starts_ref, h_ref, sel_ref, xg_ref, *, n_tb, cap, win, n_exp):
    v, g, j = pl.program_id(0), pl.program_id(1), pl.program_id(2)
    ng = sel_ref.shape[2]
    base = (v * n_exp + g * ng) * (n_tb + 1)

    @pl.when(j == 0)
    def _():
        xg_ref[...] = jnp.zeros_like(xg_ref)

    w, rounds = _windows(starts_ref, base, n_tb + 1, ng, j, cap, win)
    tb = h_ref.shape[1]
    row_i = lax.broadcasted_iota(jnp.int32, (win, tb), 0).astype(F32)

    def body(r, _):
        starts, pieces = [], []
        for k in range(ng):
            lo = w[k] + r * win
            c = pl.multiple_of(jnp.minimum(lo, cap - win), SLOT_ALIGN)
            sel = sel_ref[0, 0, k:k + 1, :]
            rel = jnp.where(sel >= lo.astype(F32), sel, -1.0) - c.astype(F32)
            pieces.append(jnp.where(rel == row_i, 1.0, 0.0).astype(BF16))
            starts.append(c)
        res = _dot(jnp.concatenate(pieces, axis=0), h_ref[0])
        for k in range(ng):
            rows = pl.ds(starts[k], win)
            xg_ref[k, rows, :] = (xg_ref[k, rows, :].astype(F32)
                                  + res[k * win:(k + 1) * win, :]).astype(BF16)
        return 0

    lax.fori_loop(0, rounds, body, 0)


def _moe_gather(starts, h, sel, n_tok, cap, win):
    nb, _, d = h.shape
    n_exp = sel.shape[1]
    ng = min(EXPERT_GROUP, n_exp)
    n_tb = n_tok // ROW_BLK
    sel = sel.reshape(nb, n_exp // ng, ng, n_tok)
    return pl.pallas_call(
        functools.partial(_gather_kernel, n_tb=n_tb, cap=cap, win=win, n_exp=n_exp),
        grid_spec=pltpu.PrefetchScalarGridSpec(
            num_scalar_prefetch=1,
            grid=(nb, n_exp // ng, n_tb),
            in_specs=[
                pl.BlockSpec((1, ROW_BLK, d), lambda v, g, j, s: (v, j, 0)),
                pl.BlockSpec((1, 1, ng, ROW_BLK), lambda v, g, j, s: (v, g, 0, j)),
            ],
            out_specs=pl.BlockSpec((ng, cap, d), lambda v, g, j, s: (g, v, 0)),
        ),
        out_shape=jax.ShapeDtypeStruct((n_exp, nb * cap, d), BF16),
        compiler_params=_cparams(("parallel", "parallel", "arbitrary"), 56),
        name="moe_gather",
    )(starts, h, sel)


def _ffn_kernel(xg_ref, wg_ref, wu_ref, act_ref, wgb_ref, wub_ref):
    @pl.when(pl.program_id(2) == 0)
    def _():
        wgb_ref[...] = wg_ref[0, 0].astype(BF16)
        wub_ref[...] = wu_ref[0, 0].astype(BF16)

    x = xg_ref[0]
    a = _dot(x, wgb_ref[...])
    u = _dot(x, wub_ref[...])
    act_ref[0] = (_silu(a) * u).astype(BF16)


def _moe_ffn(xg, w_gate, w_up, layer):
    n_exp, rows, d = xg.shape
    ff = w_gate.shape[3]
    fh = _largest_tile(ff, 512, LANES)
    rblk = _largest_tile(rows, 512)
    wspec = pl.BlockSpec((1, 1, d, fh), lambda e, f, r: (layer, e, 0, f))
    return pl.pallas_call(
        _ffn_kernel,
        grid=(n_exp, ff // fh, rows // rblk),
        in_specs=[pl.BlockSpec((1, rblk, d), lambda e, f, r: (e, r, 0)), wspec, wspec],
        out_specs=pl.BlockSpec((1, rblk, fh), lambda e, f, r: (e, r, f)),
        out_shape=jax.ShapeDtypeStruct((n_exp, rows, ff), BF16),
        scratch_shapes=[pltpu.VMEM((d, fh), BF16), pltpu.VMEM((d, fh), BF16)],
        compiler_params=_cparams(("parallel", "parallel", "arbitrary"), 48),
        name="moe_ffn",
    )(xg, w_gate, w_up)


def _down_kernel(act_ref, wd_ref, y_ref, wdb_ref):
    @pl.when(pl.program_id(1) == 0)
    def _():
        wdb_ref[...] = wd_ref[0, 0].astype(BF16)

    y_ref[0] = _dot(act_ref[0], wdb_ref[...]).astype(BF16)


def _moe_down(act, w_down, layer, cap):
    n_exp, rows, ff = act.shape
    d = w_down.shape[3]
    nb = rows // cap
    return pl.pallas_call(
        _down_kernel,
        grid=(n_exp, nb),
        in_specs=[
            pl.BlockSpec((1, cap, ff), lambda e, v: (e, v, 0)),
            pl.BlockSpec((1, 1, ff, d), lambda e, v: (layer, e, 0, 0)),
        ],
        out_specs=pl.BlockSpec((1, cap, d), lambda e, v: (v, e, 0)),
        out_shape=jax.ShapeDtypeStruct((nb, n_exp * cap, d), BF16),
        scratch_shapes=[pltpu.VMEM((ff, d), BF16)],
        compiler_params=_cparams(("parallel", "arbitrary"), 48),
        name="moe_down",
    )(act, w_down)


def _combine_kernel(starts_ref, y_ref, sel_ref, aff_ref, out_ref, st_ref, *, n_tb, cap, win, n_exp):
    v, j = pl.program_id(0), pl.program_id(2)
    base = v * n_exp * (n_tb + 1)
    w, rounds = _windows(starts_ref, base, n_tb + 1, n_exp, j, cap, win)
    out_ref[...] = jnp.zeros_like(out_ref)
    cols = lax.broadcasted_iota(jnp.int32, (n_exp, n_exp * win), 1)
    spread = jnp.where(cols // win == lax.broadcasted_iota(jnp.int32, (n_exp, n_exp * win), 0),
                       1.0, 0.0).astype(BF16)
    slot_in_win = (lax.broadcasted_iota(jnp.int32, (1, n_exp * win), 1) % win + 1).astype(F32)
    exp_lane = lax.broadcasted_iota(jnp.int32, (1, n_exp), 1)
    a_hi, a_lo = _split(aff_ref[0])
    gate_hi = _dot(a_hi, spread)
    gate_lo = _dot(a_lo, spread)

    def body(r, _):
        lo_vec = jnp.zeros((1, n_exp), F32)
        c_vec = jnp.zeros((1, n_exp), F32)
        for e in range(n_exp):
            lo = w[e] + r * win
            c = pl.multiple_of(jnp.minimum(lo, cap - win), SLOT_ALIGN)
            src = pl.multiple_of(e * cap + c, SLOT_ALIGN)
            st_ref[e * win:(e + 1) * win, :] = y_ref[0, pl.ds(src, win), :]
            lo_vec = jnp.where(exp_lane == e, lo.astype(F32), lo_vec)
            c_vec = jnp.where(exp_lane == e, c.astype(F32), c_vec)
        sel = sel_ref[0]
        rel = jnp.where(sel >= lo_vec, sel, -1.0) - c_vec
        rel1 = jnp.where((rel >= 0.0) & (rel < float(win)), rel + 1.0, 0.0)
        hit = _dot(rel1.astype(BF16), spread) == slot_in_win
        st = st_ref[...]
        out_ref[0] += (_dot(jnp.where(hit, gate_hi, 0.0).astype(BF16), st)
                       + _dot(jnp.where(hit, gate_lo, 0.0).astype(BF16), st))
        return 0

    lax.fori_loop(0, rounds, body, 0)


def _moe_combine(starts, y, sel_t, aff_t, n_tok, cap, win):
    nb, _, d = y.shape
    n_exp = sel_t.shape[2]
    n_tb = n_tok // ROW_BLK
    pw = _largest_tile(d, 1024, LANES)
    tok = pl.BlockSpec((1, ROW_BLK, n_exp), lambda v, p, j, s: (v, j, 0))
    return pl.pallas_call(
        functools.partial(_combine_kernel, n_tb=n_tb, cap=cap, win=win, n_exp=n_exp),
        grid_spec=pltpu.PrefetchScalarGridSpec(
            num_scalar_prefetch=1,
            grid=(nb, d // pw, n_tb),
            in_specs=[pl.BlockSpec((1, n_exp * cap, pw), lambda v, p, j, s: (v, 0, p)), tok, tok],
            out_specs=pl.BlockSpec((1, ROW_BLK, pw), lambda v, p, j, s: (v, j, p)),
            scratch_shapes=[pltpu.VMEM((n_exp * win, pw), BF16)],
        ),
        out_shape=jax.ShapeDtypeStruct((nb, n_tok, d), F32),
        compiler_params=_cparams(("parallel", "parallel", "arbitrary"), 48),
        name="moe_combine",
    )(starts, y, sel_t, aff_t)


def _rope_tables(pos_groups, half):
    freqs = ROPE_BASE ** (-jnp.arange(half, dtype=F32) / half)
    cos, sa, sb = [], [], []
    for pos in pos_groups:
        ang = pos.astype(F32)[:, None] * freqs[None, :]
        c, s = jnp.cos(ang), jnp.sin(ang)
        z = jnp.zeros_like(s)
        cos += [c, c]
        sa += [-s, z]
        sb += [z, s]
    return jnp.stack([jnp.concatenate(t, axis=1) for t in (cos, sa, sb)])


def _attn_tables(n_lat, n_ctx):
    s = jnp.arange(n_lat)
    tab = _rope_tables([s // GRID_W, s % GRID_W], HEAD_DIM // 4)
    ident = jnp.stack([jnp.ones((n_ctx, HEAD_DIM), F32), jnp.zeros((n_ctx, HEAD_DIM), F32),
                       jnp.zeros((n_ctx, HEAD_DIM), F32)])
    return jnp.concatenate([tab, ident], axis=1)


def _ret_tables(n_lat, n_ctx):
    s = jnp.arange(n_lat)
    t = jnp.arange(n_ctx)
    fwd = jnp.concatenate([n_ctx + s, t])
    bwd = jnp.concatenate([n_ctx + (n_lat - 1 - s), n_ctx - 1 - t])
    return jnp.concatenate([_rope_tables([fwd], HEAD_DIM // 2),
                            _rope_tables([bwd], HEAD_DIM // 2)], axis=0)


def kernel(x, c, ctx, c_ctx, w_ada, b_ada, norm_mix, norm_ffn, w_in, q_norm, k_norm, ret_log_decay,
           w_out, w_router, w_gate, w_up, w_down):
    bsz, n_lat, d = x.shape
    n_ctx = ctx.shape[1]
    seq_len = n_lat + n_ctx
    depth = w_ada.shape[0]
    n_exp = w_router.shape[2]
    n_lat_blk = n_lat // ROW_BLK
    assert n_lat % ROW_BLK == 0 and n_ctx == ROW_BLK and bsz + 1 <= 8
    cap_l = CAPACITY_FACTOR * n_lat // n_exp
    cap_c = CAPACITY_FACTOR * n_ctx // n_exp

    cond8 = jnp.zeros((8, d), F32).at[:bsz].set(c).at[bsz].set(c_ctx)
    mods = _adaln(cond8, w_ada, b_ada).reshape(depth, 8, 6, d)

    attn_tab = _attn_tables(n_lat, n_ctx)
    ret_tab = _ret_tables(n_lat, n_ctx)
    wc = _chan_mats()
    two_stage = n_lat % (FFT_COLS * 2 * SUBLANES) == 0
    lat_mats = dict(fft=_fft_mats(n_lat)) if two_stage else dict(dft=_dft_mats(n_lat))
    dft_c = _dft_mats(n_ctx)

    o_q, o_k = 0, ATTN_WIDTH
    o_v = o_k + KV_WIDTH
    o_f = o_v + KV_WIDTH
    o_rq = o_f + FOURIER_WIDTH
    o_rk = o_rq + RET_WIDTH
    o_rv = o_rk + RET_WIDTH
    o_g = o_rv + RET_WIDTH
    o_end = o_g + 2 * RET_WIDTH
    q_scale = HEAD_DIM ** -0.5 * math.log2(math.e)

    sb_c = bsz * cap_c
    win_l = min(cap_l, SLOT_WINDOW)
    win_c = cap_c
    starts_c = jnp.tile(jnp.arange(bsz + 1, dtype=jnp.int32) * cap_c, n_exp)

    xs = jnp.concatenate([x, ctx], axis=1)
    delta_lat = delta_ctx = None
    for l in range(depth):
        xs, h = _norm(xs, mods[l], norm_mix[l], n_lat_blk, delta_lat, delta_ctx,
                      mods[l - 1] if l else None)
        h2d = h.reshape(bsz * seq_len, d)
        wl = w_in[l].astype(BF16)
        q = _proj(h2d, wl[:, o_q:o_k], seq_len, "qk", gain=q_norm[l], tab=attn_tab,
                  n_heads=ATTN_HEADS, scale=q_scale, half=HEAD_DIM // 4)
        k = _proj(h2d, wl[:, o_k:o_v], seq_len, "qk", gain=k_norm[l], tab=attn_tab,
                  n_heads=ATTN_KV_HEADS, scale=1.0, half=HEAD_DIM // 4)
        v, f = _proj(h2d, wl[:, o_v:o_rq], seq_len, "split", split_at=KV_WIDTH)
        rq = _proj(h2d, wl[:, o_rq:o_rk], seq_len, "ret", tab=ret_tab, n_heads=RET_HEADS,
                   scale=1.0, half=HEAD_DIM // 2)
        rk = _proj(h2d, wl[:, o_rk:o_rv], seq_len, "ret", tab=ret_tab, n_heads=RET_HEADS,
                   scale=HEAD_DIM ** -0.5, half=HEAD_DIM // 2)
        rv = _proj(h2d, wl[:, o_rv:o_g], seq_len, "plain")
        sg = _proj(h2d, wl[:, o_g:o_end], seq_len, "silu")

        a = _attention(q, k, v, bsz, seq_len, n_lat)
        fm_lat = _fourier(f, bsz, seq_len, 0, n_lat, wc, **lat_mats)
        fm_ctx = _fourier(f, bsz, seq_len, n_lat, n_ctx, wc, dft=dft_c)
        o_f, o_b = _retention(rq, rk, rv, ret_log_decay[l], bsz, seq_len, n_lat)

        xs, h2, aff_t = _outproj(a, fm_lat, fm_ctx, o_f, o_b, sg, w_out[l].astype(BF16), xs, mods[l],
                                 norm_ffn[l], w_router[l].T, n_lat_blk)

        sel_l, gat_l, sel_c, gat_c, st_l = _routing(aff_t, bsz, seq_len, n_lat, cap_l, cap_c)
        starts_l = st_l[:, :, :n_lat_blk + 1].astype(jnp.int32).reshape(-1)
        merge = lambda t: t.transpose(1, 0, 2).reshape(1, n_exp, bsz * n_ctx)
        tok_major = lambda t: t.transpose(0, 2, 1)
        sel_c, gat_c = merge(sel_c), merge(gat_c)
        h_ctx = h2[:, n_lat:, :].reshape(1, bsz * n_ctx, d)

        xg_l = _moe_gather(starts_l, h2, sel_l, n_lat, cap_l, win_l)
        xg_c = _moe_gather(starts_c, h_ctx, sel_c, bsz * n_ctx, sb_c, win_c)
        y_l = _moe_down(_moe_ffn(xg_l, w_gate, w_up, l), w_down, l, cap_l)
        y_c = _moe_down(_moe_ffn(xg_c, w_gate, w_up, l), w_down, l, sb_c)
        delta_lat = _moe_combine(starts_l, y_l, tok_major(sel_l), tok_major(gat_l), n_lat, cap_l,
                                 win_l)
        delta_ctx = _moe_combine(starts_c, y_c, tok_major(sel_c), tok_major(gat_c), bsz * n_ctx,
                                 sb_c, win_c).reshape(bsz * n_ctx, d)

    out, _ = _norm(xs, None, None, n_lat_blk, delta_lat, delta_ctx, mods[depth - 1],
                   want_h=False, lat_only=True)
    return out
```

```python
import functools
import math

import jax
import jax.numpy as jnp
from jax import lax
from jax.experimental import pallas as pl
from jax.experimental.pallas import tpu as pltpu

HEAD_DIM = 128
ATTN_HEADS = 8
ATTN_KV_HEADS = 2
ATTN_GROUP = ATTN_HEADS // ATTN_KV_HEADS
FOURIER_GROUPS = 4
RET_HEADS = 4
RET_CHUNK = 128
GRID_W = 64
ROPE_BASE = 10000.0
EPS = 1e-6
CAPACITY_FACTOR = 2

ATTN_WIDTH = ATTN_HEADS * HEAD_DIM
KV_WIDTH = ATTN_KV_HEADS * HEAD_DIM
FOURIER_WIDTH = FOURIER_GROUPS * HEAD_DIM
RET_WIDTH = RET_HEADS * HEAD_DIM

ROW_BLK = 256
LANES = 128
MIB = 1024 * 1024

F32 = jnp.float32
BF16 = jnp.bfloat16


def _cparams(sem, vmem_mib):
    return pltpu.CompilerParams(dimension_semantics=sem, vmem_limit_bytes=vmem_mib * MIB)


def _largest_tile(n, cap, mult=8):
    best = mult
    for t in range(mult, min(n, cap) + 1, mult):
        if n % t == 0:
            best = t
    return best


def _split(a):
    hi = a.astype(BF16)
    lo = (a - hi.astype(F32)).astype(BF16)
    return hi, lo


def _dot(a, b):
    return jnp.dot(a, b, preferred_element_type=F32)


def _dot_nt(a, b):
    return lax.dot_general(a, b, (((1,), (1,)), ((), ())), preferred_element_type=F32)


def _dot_tn(a, b):
    return lax.dot_general(a, b, (((0,), (0,)), ((), ())), preferred_element_type=F32)


def _silu(a):
    return a / (1.0 + jnp.exp(-a))


def _adaln_kernel(c_ref, w_ref, b_ref, o_ref):
    s = _silu(c_ref[...])
    sh, sl = _split(s)
    wh, wl = _split(w_ref[0])
    o_ref[0] = _dot(sh, wh) + (_dot(sh, wl) + _dot(sl, wh)) + b_ref[0]


def _adaln(cond8, w_ada, b_ada):
    depth, d, n = w_ada.shape
    tn = _largest_tile(n, 768, LANES)
    return pl.pallas_call(
        _adaln_kernel,
        grid=(depth, n // tn),
        in_specs=[
            pl.BlockSpec((8, d), lambda l, j: (0, 0)),
            pl.BlockSpec((1, d, tn), lambda l, j: (l, 0, j)),
            pl.BlockSpec((1, 1, tn), lambda l, j: (l, 0, j)),
        ],
        out_specs=pl.BlockSpec((1, 8, tn), lambda l, j: (l, 0, j)),
        out_shape=jax.ShapeDtypeStruct((depth, 8, n), F32),
        compiler_params=_cparams(("parallel", "parallel"), 48),
        name="adaln",
    )(cond8, w_ada, b_ada.reshape(depth, 1, n))


def _rms_mod(x, g, shift, scale):
    y = x * lax.rsqrt(jnp.mean(x * x, axis=-1, keepdims=True) + EPS)
    return (y * g) * (1.0 + scale) + shift


def _norm_kernel(*refs, has_delta, want_h, n_lat_blk):
    refs = list(refs)
    x_ref = refs.pop(0)
    x = x_ref[0]
    if has_delta:
        dl_ref, dc_ref, pm_ref = refs.pop(0), refs.pop(0), refs.pop(0)
        is_ctx = pl.program_id(1) >= n_lat_blk
        delta = jnp.where(is_ctx, dc_ref[...], dl_ref[0])
        x = x + pm_ref[0, 5:6, :] * delta
    if want_h:
        m_ref, g_ref = refs.pop(0), refs.pop(0)
    if has_delta:
        xo_ref = refs.pop(0)
        xo_ref[0] = x
    if want_h:
        h_ref = refs.pop(0)
        h_ref[0] = _rms_mod(x, g_ref[...], m_ref[0, 0:1, :], m_ref[0, 1:2, :]).astype(BF16)


def _norm(x, mods, g, n_lat_blk, delta_lat=None, delta_ctx=None, prev_mods=None,
          want_h=True, lat_only=False):
    bsz, l, d = x.shape
    n_blk = n_lat_blk if lat_only else l // ROW_BLK
    has_delta = delta_lat is not None

    def mod_row(b, t):
        return (jnp.where(t >= n_lat_blk, bsz, b), 0, 0)

    xspec = pl.BlockSpec((1, ROW_BLK, d), lambda b, t: (b, t, 0))
    in_specs, args = [xspec], [x]
    if has_delta:
        in_specs += [
            pl.BlockSpec((1, ROW_BLK, d), lambda b, t: (b, jnp.minimum(t, n_lat_blk - 1), 0)),
            pl.BlockSpec((ROW_BLK, d), lambda b, t: (b, 0)),
            pl.BlockSpec((1, 6, d), mod_row),
        ]
        args += [delta_lat, delta_ctx, prev_mods]
    if want_h:
        in_specs += [pl.BlockSpec((1, 6, d), mod_row), pl.BlockSpec((1, d), lambda b, t: (0, 0))]
        args += [mods, g.reshape(1, d)]
    out_specs, out_shape = [], []
    rows = n_blk * ROW_BLK
    if has_delta:
        out_specs.append(xspec)
        out_shape.append(jax.ShapeDtypeStruct((bsz, rows, d), F32))
    if want_h:
        out_specs.append(xspec)
        out_shape.append(jax.ShapeDtypeStruct((bsz, rows, d), BF16))
    outs = pl.pallas_call(
        functools.partial(_norm_kernel, has_delta=has_delta, want_h=want_h, n_lat_blk=n_lat_blk),
        grid=(bsz, n_blk),
        in_specs=in_specs,
        out_specs=out_specs,
        out_shape=out_shape,
        compiler_params=_cparams(("parallel", "parallel"), 32),
        name="norm",
    )(*args)
    outs = list(outs)
    x_new = outs.pop(0) if has_delta else x
    h = outs.pop(0) if want_h else None
    return x_new, h


def _dot_wide(a, m):
    hi, lo = _split(a)
    return _dot(hi, m) + _dot(lo, m)


def _partner_matrix(half):
    src = lax.broadcasted_iota(jnp.int32, (LANES, LANES), 0)
    dst = lax.broadcasted_iota(jnp.int32, (LANES, LANES), 1)
    mate = jnp.where(dst % (2 * half) < half, dst + half, dst - half)
    return jnp.where(src == mate, 1.0, 0.0).astype(BF16)


def _proj_kernel(*refs, flavor, n_heads, scale, half, split_at):
    h_ref, w_ref = refs[0], refs[1]
    acc = _dot(h_ref[...], w_ref[...])
    if flavor == "plain":
        refs[2][...] = acc.astype(BF16)
    elif flavor == "split":
        refs[2][...] = acc[:, :split_at].astype(BF16)
        refs[3][...] = acc[:, split_at:].astype(BF16)
    elif flavor == "silu":
        refs[2][...] = _silu(acc).astype(BF16)
    elif flavor == "qk":
        gain_ref, tab_ref, o_ref = refs[2], refs[3], refs[4]
        partner = _partner_matrix(half)
        ones = jnp.ones((LANES, LANES), BF16)
        for hd in range(n_heads):
            sl = slice(hd * HEAD_DIM, (hd + 1) * HEAD_DIM)
            z = acc[:, sl]
            ms = _dot_wide(z * z, ones) * (1.0 / HEAD_DIM)
            z = (z * lax.rsqrt(ms + EPS)) * gain_ref[...]
            z = z * tab_ref[0] + _dot_wide(z, partner) * tab_ref[1]
            o_ref[:, sl] = (z * scale).astype(BF16)
    elif flavor == "ret":
        tab_ref, o_ref = refs[2], refs[3]
        partner = _partner_matrix(half)
        width = n_heads * HEAD_DIM
        for hd in range(n_heads):
            sl = slice(hd * HEAD_DIM, (hd + 1) * HEAD_DIM)
            z = acc[:, sl] * scale
            pz = _dot_wide(z, partner)
            o_ref[:, sl] = (z * tab_ref[0] + pz * tab_ref[1]).astype(BF16)
            o_ref[:, width + hd * HEAD_DIM: width + (hd + 1) * HEAD_DIM] = (
                z * tab_ref[2] + pz * tab_ref[3]).astype(BF16)
    else:
        raise ValueError(flavor)


def _proj(h2d, w, seq_len, flavor, *, gain=None, tab=None, n_heads=0, scale=1.0, half=0,
          split_at=0):
    t_rows, d = h2d.shape
    n = w.shape[1]
    tm = _largest_tile(seq_len, 1088)
    per_seq = seq_len // tm
    in_specs = [pl.BlockSpec((tm, d), lambda i: (i, 0)), pl.BlockSpec((d, n), lambda i: (0, 0))]
    args = [h2d, w]
    if flavor == "qk":
        in_specs.append(pl.BlockSpec((1, HEAD_DIM), lambda i: (0, 0)))
        args.append(gain.reshape(1, HEAD_DIM))
    if flavor in ("qk", "ret"):
        ntab = tab.shape[0]
        in_specs.append(pl.BlockSpec((ntab, tm, HEAD_DIM), lambda i: (0, i % per_seq, 0)))
        args.append(tab)
    if flavor == "split":
        widths = [split_at, n - split_at]
    elif flavor == "ret":
        widths = [2 * n]
    else:
        widths = [n]
    out_specs = [pl.BlockSpec((tm, wd), lambda i: (i, 0)) for wd in widths]
    out_shape = [jax.ShapeDtypeStruct((t_rows, wd), BF16) for wd in widths]
    outs = pl.pallas_call(
        functools.partial(_proj_kernel, flavor=flavor, n_heads=n_heads, scale=scale, half=half,
                          split_at=split_at),
        grid=(t_rows // tm,),
        in_specs=in_specs,
        out_specs=out_specs,
        out_shape=out_shape,
        compiler_params=_cparams(("parallel",), 48),
        name="proj_" + flavor,
    )(*args)
    return outs if len(outs) > 1 else outs[0]


SUBLANES = 8


SAFE_SHIFT = 50.0


def _lane_fold(t, op):
    return functools.reduce(op, [t[:, i * LANES:(i + 1) * LANES]
                                 for i in range(t.shape[1] // LANES)])


def _attn_head_bounded(q, m, k_ref, v_ref, p_ref, k_lo, n_keys, tk):
    lp = None
    for c in range(n_keys // tk):
        s = _dot_nt(q, k_ref[k_lo + c * tk:k_lo + (c + 1) * tk, :])
        p = jnp.exp2(s - m)
        part = _lane_fold(p, jnp.add)
        lp = part if lp is None else lp + part
        p_ref[:, c * tk:(c + 1) * tk] = p.astype(BF16)
    l = jnp.sum(lp, axis=-1, keepdims=True)
    return _dot(p_ref[:, :n_keys], v_ref[k_lo:k_lo + n_keys, :]) / l


def _attn_head_exact(q, k_ref, v_ref, s_ref, p_ref, k_lo, n_keys, tk):
    s_ref[:, :n_keys] = _dot_nt(q, k_ref[k_lo:k_lo + n_keys, :])
    n_chunks = n_keys // tk
    mp = None
    for c in range(n_chunks):
        part = _lane_fold(s_ref[:, c * tk:(c + 1) * tk], jnp.maximum)
        mp = part if mp is None else jnp.maximum(mp, part)
    m = jnp.max(mp, axis=-1, keepdims=True)
    lp = None
    for c in range(n_chunks):
        p = jnp.exp2(s_ref[:, c * tk:(c + 1) * tk] - m)
        part = _lane_fold(p, jnp.add)
        lp = part if lp is None else lp + part
        p_ref[:, c * tk:(c + 1) * tk] = p.astype(BF16)
    l = jnp.sum(lp, axis=-1, keepdims=True)
    return _dot(p_ref[:, :n_keys], v_ref[k_lo:k_lo + n_keys, :]) / l


def _attn_kernel(q_ref, k_ref, v_ref, o_ref, s_ref, p_ref, kmax_ref, *, tk, n_lat, n_ctx,
                 n_lat_qblk):
    i = pl.program_id(2)
    is_ctx = i >= n_lat_qblk

    @pl.when(i == 0)
    def _():
        kk = k_ref[...].astype(F32)
        kn2 = jnp.max(jnp.sum(kk * kk, axis=-1, keepdims=True), axis=0, keepdims=True)
        kmax_ref[...] = jnp.broadcast_to(jnp.sqrt(kn2), kmax_ref.shape)

    heads = [slice(hd * HEAD_DIM, (hd + 1) * HEAD_DIM) for hd in range(ATTN_GROUP)]
    bounds = []
    for sl in heads:
        qf = q_ref[:, sl].astype(F32)
        qn = jnp.sqrt(jnp.sum(qf * qf, axis=-1, keepdims=True))
        bounds.append(qn * kmax_ref[0:1, 0:1] * (1.0 + 2.0 ** -10))
    worst = functools.reduce(jnp.maximum, [jnp.max(b) for b in bounds])
    bounded_ok = worst <= SAFE_SHIFT

    def run(k_lo, n_keys, bounded):
        for hd, sl in enumerate(heads):
            if bounded:
                o = _attn_head_bounded(q_ref[:, sl], bounds[hd], k_ref, v_ref, p_ref.at[hd], k_lo,
                                       n_keys, tk)
            else:
                o = _attn_head_exact(q_ref[:, sl], k_ref, v_ref, s_ref.at[hd], p_ref.at[hd], k_lo,
                                     n_keys, tk)
            o_ref[:, sl] = o.astype(BF16)

    for ctx_case, (k_lo, n_keys) in ((False, (0, n_lat + n_ctx)), (True, (n_lat, n_ctx))):
        for bounded in (True, False):
            @pl.when((is_ctx == ctx_case) & (bounded_ok == bounded))
            def _(k_lo=k_lo, n_keys=n_keys, bounded=bounded):
                run(k_lo, n_keys, bounded)


def _attention(q, k, v, bsz, seq_len, n_lat):
    tq = ROW_BLK
    tk = ROW_BLK
    per_seq = seq_len // tq
    gw = ATTN_GROUP * HEAD_DIM
    return pl.pallas_call(
        functools.partial(_attn_kernel, tk=tk, n_lat=n_lat, n_ctx=seq_len - n_lat,
                          n_lat_qblk=n_lat // tq),
        grid=(bsz, ATTN_KV_HEADS, per_seq),
        in_specs=[
            pl.BlockSpec((tq, gw), lambda b, g, i: (b * per_seq + i, g)),
            pl.BlockSpec((seq_len, HEAD_DIM), lambda b, g, i: (b, g)),
            pl.BlockSpec((seq_len, HEAD_DIM), lambda b, g, i: (b, g)),
        ],
        out_specs=pl.BlockSpec((tq, gw), lambda b, g, i: (b * per_seq + i, g)),
        out_shape=jax.ShapeDtypeStruct(q.shape, BF16),
        scratch_shapes=[pltpu.VMEM((ATTN_GROUP, tq, seq_len), F32),
                        pltpu.VMEM((ATTN_GROUP, tq, seq_len), BF16),
                        pltpu.VMEM((SUBLANES, LANES), F32)],
        compiler_params=_cparams(("parallel", "parallel", "arbitrary"), 48),
        name="attention",
    )(q, k, v)


def _fourier_a_kernel(f_ref, wc_ref, o_ref):
    g = _dot(f_ref[...], wc_ref[...])
    wdt = f_ref.shape[1]
    o_ref[0, 0] = g[:, :wdt].astype(o_ref.dtype)
    o_ref[0, 1] = g[:, wdt:].astype(o_ref.dtype)


def _fourier_b_kernel(m_ref, g_ref, o_ref):
    o_ref[0] = _dot(m_ref[...], g_ref[0]).astype(BF16)


FFT_COLS = 64
FFT_PER_STEP = SUBLANES


def _fft1_kernel(g_ref, m_ref, tc_ref, ts_ref, o_ref):
    width = o_ref.shape[4]
    n1 = g_ref.shape[2]
    for s in range(g_ref.shape[3]):
        g = jnp.concatenate([g_ref[0, 0, :, s, :], g_ref[0, 1, :, s, :]], axis=0).astype(BF16)
        a = _dot(m_ref[...], g)
        ar, ai = a[:n1], a[n1:]
        tc = tc_ref[0, :, s * LANES:(s + 1) * LANES]
        ts = ts_ref[0, :, s * LANES:(s + 1) * LANES]
        lanes = [slice(q * LANES, (q + 1) * LANES) for q in range(width // LANES)]
        o_ref[0, 0, :, s, :] = jnp.concatenate([ar[:, q] * tc + ai[:, q] * ts for q in lanes], axis=1)
        o_ref[0, 1, :, s, :] = jnp.concatenate([ai[:, q] * tc - ar[:, q] * ts for q in lanes], axis=1)


def _fft2_kernel(b_ref, m_ref, o_ref):
    for s in range(b_ref.shape[2]):
        b = jnp.concatenate([b_ref[0, 0, s], b_ref[0, 1, s]], axis=0).astype(BF16)
        o_ref[0, :, s, :] = _dot(m_ref[...], b)


def _fourier_two_stage(g, fft):
    m1, tc, ts, m3 = fft
    bsz, _, n, width = g.shape
    n2 = m3.shape[0]
    n1 = n // n2
    per = FFT_PER_STEP
    b = pl.pallas_call(
        _fft1_kernel,
        grid=(bsz, n2 // per),
        in_specs=[
            pl.BlockSpec((1, 2, n1, per, width), lambda b, k: (b, 0, 0, k, 0)),
            pl.BlockSpec(m1.shape, lambda b, k: (0, 0)),
            pl.BlockSpec((1, n1, per * LANES), lambda b, k: (k, 0, 0)),
            pl.BlockSpec((1, n1, per * LANES), lambda b, k: (k, 0, 0)),
        ],
        out_specs=pl.BlockSpec((1, 2, n1, per, width), lambda b, k: (b, 0, 0, k, 0)),
        out_shape=jax.ShapeDtypeStruct((bsz, 2, n1, n2, width), F32),
        compiler_params=_cparams(("parallel", "parallel"), 32),
        name="fourier_fft1",
    )(g.reshape(bsz, 2, n1, n2, width), m1, tc, ts)
    out = pl.pallas_call(
        _fft2_kernel,
        grid=(bsz, n1 // per),
        in_specs=[
            pl.BlockSpec((1, 2, per, n2, width), lambda b, k: (b, 0, k, 0, 0)),
            pl.BlockSpec(m3.shape, lambda b, k: (0, 0)),
        ],
        out_specs=pl.BlockSpec((1, n2, per, width), lambda b, k: (b, 0, k, 0)),
        out_shape=jax.ShapeDtypeStruct((bsz, n2, n1, width), F32),
        compiler_params=_cparams(("parallel", "parallel"), 32),
        name="fourier_fft2",
    )(b, m3)
    return out.reshape(bsz, n, width)


def _fft_mats(n):
    n2 = FFT_COLS
    n1 = n // n2
    per = FFT_PER_STEP

    def cs(rows, cols, period):
        ang = ((rows[:, None] * cols[None, :]) % period).astype(F32) * (2.0 * math.pi / period)
        return jnp.cos(ang), jnp.sin(ang)

    i1, i2 = jnp.arange(n1, dtype=jnp.int32), jnp.arange(n2, dtype=jnp.int32)
    c1, s1 = cs(i1, i1, n1)
    m1 = (jnp.block([[c1, s1], [-s1, c1]]) * n1 ** -0.5).astype(BF16)
    c3, s3 = cs(i2, i2, n2)
    m3 = (jnp.concatenate([c3, s3], axis=1) * n2 ** -0.5).astype(BF16)
    tc, ts = cs(i1, i2, n)
    expand = lambda t: jnp.repeat(t.T.reshape(n2 // per, per, n1).transpose(0, 2, 1), LANES,
                                  axis=2)
    return m1, expand(tc), expand(ts), m3


def _fourier(f2d, bsz, seq_len, row_off, n, wc, dft=None, fft=None):
    per_seq = seq_len // ROW_BLK
    off_blk = row_off // ROW_BLK
    nb = n // ROW_BLK
    width = f2d.shape[1]
    g = pl.pallas_call(
        _fourier_a_kernel,
        grid=(bsz, nb),
        in_specs=[
            pl.BlockSpec((ROW_BLK, width), lambda b, t: (b * per_seq + off_blk + t, 0)),
            pl.BlockSpec((width, 2 * width), lambda b, t: (0, 0)),
        ],
        out_specs=pl.BlockSpec((1, 2, ROW_BLK, width), lambda b, t: (b, 0, t, 0)),
        out_shape=jax.ShapeDtypeStruct((bsz, 2, n, width), BF16 if fft is None else F32),
        compiler_params=_cparams(("parallel", "parallel"), 32),
        name="fourier_chan",
    )(f2d, wc)
    if fft is not None:
        return _fourier_two_stage(g, fft)
    g = g.reshape(bsz, 2 * n, width)
    return pl.pallas_call(
        _fourier_b_kernel,
        grid=(nb, bsz),
        in_specs=[
            pl.BlockSpec((ROW_BLK, 2 * n), lambda i, b: (i, 0)),
            pl.BlockSpec((1, 2 * n, width), lambda i, b: (b, 0, 0)),
        ],
        out_specs=pl.BlockSpec((1, ROW_BLK, width), lambda i, b: (b, i, 0)),
        out_shape=jax.ShapeDtypeStruct((bsz, n, width), BF16),
        compiler_params=_cparams(("parallel", "parallel"), 48),
        name="fourier_pos",
    )(dft, g)


def _dft_mats(n):
    i = jnp.arange(n, dtype=jnp.int32)
    prod = (i[:, None] * i[None, :]) % n
    ang = prod.astype(F32) * (2.0 * math.pi / n)
    s = n ** -0.5
    return jnp.concatenate([jnp.cos(ang) * s, jnp.sin(ang) * s], axis=1).astype(BF16)


def _chan_mats():
    i = jnp.arange(HEAD_DIM, dtype=jnp.int32)
    ang = ((i[:, None] * i[None, :]) % HEAD_DIM).astype(F32) * (2.0 * math.pi / HEAD_DIM)
    s = HEAD_DIM ** -0.5
    eye = jnp.eye(FOURIER_GROUPS, dtype=F32)
    c = jnp.kron(eye, jnp.cos(ang) * s)
    sn = jnp.kron(eye, -jnp.sin(ang) * s)
    return jnp.concatenate([c, sn], axis=1).astype(BF16)


RET_HEADS_PER_STEP = 2


def _ret_kernel(ld_ref, qf_ref, qb_ref, kf_ref, kb_ref, v_ref, of_ref, ob_ref, state_ref,
                *, n_lat_chunks, n_ctx_chunks):
    c = RET_CHUNK
    n_all = n_lat_chunks + n_ctx_chunks
    h0 = pl.program_id(1) * RET_HEADS_PER_STEP
    ii = lax.broadcasted_iota(jnp.int32, (c, c), 0).astype(F32)
    jj = lax.broadcasted_iota(jnp.int32, (c, c), 1).astype(F32)
    ri = lax.broadcasted_iota(jnp.int32, (c, 1), 0).astype(F32)
    state_ref[...] = jnp.zeros_like(state_ref)

    chains = []
    for hh in range(RET_HEADS_PER_STEP):
        sl = slice(hh * HEAD_DIM, (hh + 1) * HEAD_DIM)
        for d, (q_ref, k_ref, o_ref) in enumerate(((qf_ref, kf_ref, of_ref),
                                                   (qb_ref, kb_ref, ob_ref))):
            lg = ld_ref[d, h0 + hh]
            diff = ii - jj if d == 0 else jj - ii
            intra = jnp.where(diff >= 0, jnp.exp(lg * jnp.maximum(diff, 0.0)), 0.0)
            q_dec = jnp.exp(lg * (ri + 1.0 if d == 0 else c - ri))
            k_dec = jnp.exp(lg * (c - 1.0 - ri if d == 0 else ri))
            c_dec = jnp.exp(jnp.full((1, HEAD_DIM), lg * c, F32))
            chains.append((len(chains), q_ref, k_ref, o_ref, sl, (intra, q_dec, k_dec, c_dec), d))

    def body(s, _):
        in_ctx = s < n_ctx_chunks
        for slot, q_ref, k_ref, o_ref, sl, (intra, q_dec, k_dec, c_dec), d in chains:
            if d == 0:
                chunk = jnp.where(in_ctx, n_lat_chunks + s, s - n_ctx_chunks)
            else:
                chunk = n_all - 1 - s
            off = pl.multiple_of(chunk * c, c)
            q = q_ref[pl.ds(off, c), sl]
            k = k_ref[pl.ds(off, c), sl]
            v = v_ref[pl.ds(off, c), sl]
            st = state_ref[slot]
            sc = _dot_nt(q, k) * intra
            o = _dot(sc.astype(BF16), v) + _dot(q, st.astype(BF16)) * q_dec
            kd = (k.astype(F32) * k_dec).astype(BF16)
            state_ref[slot] = st * c_dec + _dot_tn(kd, v)
            mu = jnp.mean(o, axis=-1, keepdims=True)
            var = jnp.mean(jnp.square(o - mu), axis=-1, keepdims=True)
            o_ref[pl.ds(off, c), sl] = (o - mu) * lax.rsqrt(var + EPS)
        return 0

    lax.fori_loop(0, n_all, body, 0)


def _retention(rq, rk, rv, log_decay, bsz, seq_len, n_lat):
    hw = RET_HEADS_PER_STEP * HEAD_DIM
    n_hb = RET_HEADS // RET_HEADS_PER_STEP
    fwd = pl.BlockSpec((seq_len, hw), lambda b, h: (b, h))
    bwd = pl.BlockSpec((seq_len, hw), lambda b, h: (b, n_hb + h))
    return pl.pallas_call(
        functools.partial(_ret_kernel, n_lat_chunks=n_lat // RET_CHUNK,
                          n_ctx_chunks=(seq_len - n_lat) // RET_CHUNK),
        grid=(bsz, n_hb),
        in_specs=[pl.BlockSpec(memory_space=pltpu.SMEM), fwd, bwd, fwd, bwd, fwd],
        out_specs=[fwd, fwd],
        out_shape=[jax.ShapeDtypeStruct((bsz * seq_len, RET_WIDTH), F32)] * 2,
        scratch_shapes=[pltpu.VMEM((2 * RET_HEADS_PER_STEP, HEAD_DIM, HEAD_DIM), F32)],
        compiler_params=_cparams(("parallel", "parallel"), 56),
        name="retention",
    )(log_decay, rq, rq, rk, rk, rv)


def _outproj_kernel(a_ref, fl_ref, fc_ref, of_ref, ob_ref, sg_ref, w_ref, x_ref, m_ref, g_ref, wr_ref,
                    xo_ref, h_ref, aff_ref, *, n_lat_blk):
    is_ctx = pl.program_id(1) >= n_lat_blk
    fm = jnp.where(is_ctx, fc_ref[0], fl_ref[0].astype(BF16))
    rw = RET_WIDTH
    r = (sg_ref[:, :rw].astype(F32) * of_ref[...]
         + sg_ref[:, rw:].astype(F32) * ob_ref[...]).astype(BF16)
    a0, f0 = ATTN_WIDTH, ATTN_WIDTH + FOURIER_WIDTH
    y = (_dot(a_ref[...], w_ref[:a0, :]) + _dot(fm, w_ref[a0:f0, :])) + _dot(r, w_ref[f0:, :])
    x = x_ref[0] + m_ref[0, 2:3, :] * y
    xo_ref[0] = x
    h = _rms_mod(x, g_ref[...], m_ref[0, 3:4, :], m_ref[0, 4:5, :])
    h_ref[0] = h.astype(BF16)
    hh, hl = _split(h)
    wh, wl = _split(wr_ref[...])
    lt = _dot_nt(wh, hh) + (_dot_nt(wh, hl) + _dot_nt(wl, hh))
    e = jnp.exp(lt - jnp.max(lt, axis=0, keepdims=True))
    aff_ref[...] = e / jnp.sum(e, axis=0, keepdims=True)


def _outproj(a, fm_lat, fm_ctx, o_f, o_b, sg, w_out, x, mods, g, wr_t, n_lat_blk):
    bsz, l, d = x.shape
    per_seq = l // ROW_BLK
    n_exp = wr_t.shape[0]
    fw = fm_lat.shape[-1]

    def flat(b, t):
        return (b * per_seq + t, 0)

    return pl.pallas_call(
        functools.partial(_outproj_kernel, n_lat_blk=n_lat_blk),
        grid=(bsz, per_seq),
        in_specs=[
            pl.BlockSpec((ROW_BLK, a.shape[1]), flat),
            pl.BlockSpec((1, ROW_BLK, fw), lambda b, t: (b, jnp.minimum(t, n_lat_blk - 1), 0)),
            pl.BlockSpec((1, ROW_BLK, fw), lambda b, t: (b, 0, 0)),
            pl.BlockSpec((ROW_BLK, o_f.shape[1]), flat),
            pl.BlockSpec((ROW_BLK, o_b.shape[1]), flat),
            pl.BlockSpec((ROW_BLK, sg.shape[1]), flat),
            pl.BlockSpec(w_out.shape, lambda b, t: (0, 0)),
            pl.BlockSpec((1, ROW_BLK, d), lambda b, t: (b, t, 0)),
            pl.BlockSpec((1, 6, d), lambda b, t: (jnp.where(t >= n_lat_blk, bsz, b), 0, 0)),
            pl.BlockSpec((1, d), lambda b, t: (0, 0)),
            pl.BlockSpec(wr_t.shape, lambda b, t: (0, 0)),
        ],
        out_specs=[
            pl.BlockSpec((1, ROW_BLK, d), lambda b, t: (b, t, 0)),
            pl.BlockSpec((1, ROW_BLK, d), lambda b, t: (b, t, 0)),
            pl.BlockSpec((n_exp, ROW_BLK), lambda b, t: (0, b * per_seq + t)),
        ],
        out_shape=[
            jax.ShapeDtypeStruct((bsz, l, d), F32),
            jax.ShapeDtypeStruct((bsz, l, d), BF16),
            jax.ShapeDtypeStruct((n_exp, bsz * l), F32),
        ],
        compiler_params=_cparams(("parallel", "parallel"), 48),
        name="outproj",
    )(a, fm_lat, fm_ctx, o_f, o_b, sg, w_out, x, mods, g.reshape(1, d), wr_t)


def _cumsum_lanes(m, out_ref, fin):
    n_exp, n = m.shape
    tri = (lax.broadcasted_iota(jnp.int32, (LANES, LANES), 0)
           <= lax.broadcasted_iota(jnp.int32, (LANES, LANES), 1)).astype(BF16)
    run = jnp.zeros((n_exp, 1), F32)
    befores = []
    for k in range(n // LANES):
        befores.append(run)
        sl = slice(k * LANES, (k + 1) * LANES)
        cnt = _dot(m[:, sl].astype(BF16), tri) + run
        out_ref[0, :, sl] = fin(cnt, sl)
        run = cnt[:, LANES - 1:LANES]
    befores.append(run)
    return befores


def _select(seg, cap, slot_off, sel_ref, tmp_ref):
    bits = pltpu.bitcast(seg, jnp.int32)
    n_exp = seg.shape[0]

    def body(it, t):
        tt = t | lax.shift_left(jnp.int32(1), 30 - it)
        cnt = jnp.sum(jnp.where(bits >= tt, 1.0, 0.0), axis=1, keepdims=True)
        return jnp.where(cnt >= cap, tt, t)

    t = lax.fori_loop(0, 31, body, jnp.zeros((n_exp, 1), jnp.int32))
    gt = bits > t
    eq = bits == t
    need = cap - jnp.sum(jnp.where(gt, 1.0, 0.0), axis=1, keepdims=True)
    eqf = jnp.where(eq, 1.0, 0.0)
    _cumsum_lanes(eqf, tmp_ref, lambda cnt, sl: cnt)
    take = eq & (tmp_ref[0] - eqf < need)
    mask = gt | take
    maskf = jnp.where(mask, 1.0, 0.0)
    return _cumsum_lanes(
        maskf, sel_ref,
        lambda cnt, sl: jnp.where(maskf[:, sl] > 0.5, cnt - 1.0 + slot_off, -1.0))


def _routing_kernel(aff_ref, sel_l, gat_l, sel_c, gat_c, st_l, tmp_l, tmp_c, *, n_lat, cap_l, cap_c):
    b = pl.program_id(0)
    a = aff_ref[...]
    lat = a[:, :n_lat]
    ctx = a[:, n_lat:]
    gat_l[0] = lat
    gat_c[0] = ctx
    befores = _select(lat, float(cap_l), 0.0, sel_l, tmp_l)
    _select(ctx, float(cap_c), (b * cap_c).astype(F32), sel_c, tmp_c)
    lane = lax.broadcasted_iota(jnp.int32, (a.shape[0], LANES), 1)
    st = jnp.zeros((a.shape[0], LANES), F32)
    per_blk = ROW_BLK // LANES
    for j in range(n_lat // ROW_BLK + 1):
        st = jnp.where(lane == j, befores[j * per_blk], st)
    st_l[0] = st


def _routing(aff_t, bsz, seq_len, n_lat, cap_l, cap_c):
    n_exp = aff_t.shape[0]
    n_ctx = seq_len - n_lat
    shp = lambda n: jax.ShapeDtypeStruct((bsz, n_exp, n), F32)
    spec = lambda n: pl.BlockSpec((1, n_exp, n), lambda b: (b, 0, 0))
    return pl.pallas_call(
        functools.partial(_routing_kernel, n_lat=n_lat, cap_l=cap_l, cap_c=cap_c),
        grid=(bsz,),
        in_specs=[pl.BlockSpec((n_exp, seq_len), lambda b: (0, b))],
        out_specs=[spec(n_lat), spec(n_lat), spec(n_ctx), spec(n_ctx), spec(LANES)],
        out_shape=[shp(n_lat), shp(n_lat), shp(n_ctx), shp(n_ctx), shp(LANES)],
        scratch_shapes=[pltpu.VMEM((1, n_exp, n_lat), F32), pltpu.VMEM((1, n_exp, n_ctx), F32)],
        compiler_params=_cparams(("parallel",), 32),
        name="routing",
    )(aff_t)


EXPERT_GROUP = 8
SLOT_ALIGN = 16
SLOT_WINDOW = 64


def _windows(starts_ref, base, stride, n, j, cap, win):
    w, rounds = [], jnp.int32(0)
    for k in range(n):
        s0 = starts_ref[base + k * stride + j]
        s1 = starts_ref[base + k * stride + j + 1]
        wk = jnp.minimum((s0 // SLOT_ALIGN) * SLOT_ALIGN, cap - win)
        w.append(wk)
        rounds = jnp.maximum(rounds, (s1 - wk + win - 1) // win)
    return w, rounds


def _gather_kernel(starts_ref, h_ref, sel_ref, xg_ref, *, n_tb, cap, win, n_exp):
    v, g, j = pl.program_id(0), pl.program_id(1), pl.program_id(2)
    ng = sel_ref.shape[2]
    base = (v * n_exp + g * ng) * (n_tb + 1)

    @pl.when(j == 0)
    def _():
        xg_ref[...] = jnp.zeros_like(xg_ref)

    w, rounds = _windows(starts_ref, base, n_tb + 1, ng, j, cap, win)
    tb = h_ref.shape[1]
    row_i = lax.broadcasted_iota(jnp.int32, (win, tb), 0).astype(F32)

    def body(r, _):
        starts, pieces = [], []
        for k in range(ng):
            lo = w[k] + r * win
            c = pl.multiple_of(jnp.minimum(lo, cap - win), SLOT_ALIGN)
            sel = sel_ref[0, 0, k:k + 1, :]
            rel = jnp.where(sel >= lo.astype(F32), sel, -1.0) - c.astype(F32)
            pieces.append(jnp.where(rel == row_i, 1.0, 0.0).astype(BF16))
            starts.append(c)
        res = _dot(jnp.concatenate(pieces, axis=0), h_ref[0])
        for k in range(ng):
            rows = pl.ds(starts[k], win)
            xg_ref[k, rows, :] = (xg_ref[k, rows, :].astype(F32)
                                  + res[k * win:(k + 1) * win, :]).astype(BF16)
        return 0

    lax.fori_loop(0, rounds, body, 0)


def _moe_gather(starts, h, sel, n_tok, cap, win):
    nb, _, d = h.shape
    n_exp = sel.shape[1]
    ng = min(EXPERT_GROUP, n_exp)
    n_tb = n_tok // ROW_BLK
    sel = sel.reshape(nb, n_exp // ng, ng, n_tok)
    return pl.pallas_call(
        functools.partial(_gather_kernel, n_tb=n_tb, cap=cap, win=win, n_exp=n_exp),
        grid_spec=pltpu.PrefetchScalarGridSpec(
            num_scalar_prefetch=1,
            grid=(nb, n_exp // ng, n_tb),
            in_specs=[
                pl.BlockSpec((1, ROW_BLK, d), lambda v, g, j, s: (v, j, 0)),
                pl.BlockSpec((1, 1, ng, ROW_BLK), lambda v, g, j, s: (v, g, 0, j)),
            ],
            out_specs=pl.BlockSpec((ng, cap, d), lambda v, g, j, s: (g, v, 0)),
        ),
        out_shape=jax.ShapeDtypeStruct((n_exp, nb * cap, d), BF16),
        compiler_params=_cparams(("parallel", "parallel", "arbitrary"), 56),
        name="moe_gather",
    )(starts, h, sel)


def _ffn_kernel(xg_ref, wg_ref, wu_ref, act_ref, wgb_ref, wub_ref):
    @pl.when(pl.program_id(2) == 0)
    def _():
        wgb_ref[...] = wg_ref[0, 0].astype(BF16)
        wub_ref[...] = wu_ref[0, 0].astype(BF16)

    x = xg_ref[0]
    a = _dot(x, wgb_ref[...])
    u = _dot(x, wub_ref[...])
    act_ref[0] = (_silu(a) * u).astype(BF16)


def _moe_ffn(xg, w_gate, w_up, layer):
    n_exp, rows, d = xg.shape
    ff = w_gate.shape[3]
    fh = _largest_tile(ff, 512, LANES)
    rblk = _largest_tile(rows, 512)
    wspec = pl.BlockSpec((1, 1, d, fh), lambda e, f, r: (layer, e, 0, f))
    return pl.pallas_call(
        _ffn_kernel,
        grid=(n_exp, ff // fh, rows // rblk),
        in_specs=[pl.BlockSpec((1, rblk, d), lambda e, f, r: (e, r, 0)), wspec, wspec],
        out_specs=pl.BlockSpec((1, rblk, fh), lambda e, f, r: (e, r, f)),
        out_shape=jax.ShapeDtypeStruct((n_exp, rows, ff), BF16),
        scratch_shapes=[pltpu.VMEM((d, fh), BF16), pltpu.VMEM((d, fh), BF16)],
        compiler_params=_cparams(("parallel", "parallel", "arbitrary"), 48),
        name="moe_ffn",
    )(xg, w_gate, w_up)


def _down_kernel(act_ref, wd_ref, y_ref, wdb_ref):
    @pl.when(pl.program_id(1) == 0)
    def _():
        wdb_ref[...] = wd_ref[0, 0].astype(BF16)

    y_ref[0] = _dot(act_ref[0], wdb_ref[...]).astype(BF16)


def _moe_down(act, w_down, layer, cap):
    n_exp, rows, ff = act.shape
    d = w_down.shape[3]
    nb = rows // cap
    return pl.pallas_call(
        _down_kernel,
        grid=(n_exp, nb),
        in_specs=[
            pl.BlockSpec((1, cap, ff), lambda e, v: (e, v, 0)),
            pl.BlockSpec((1, 1, ff, d), lambda e, v: (layer, e, 0, 0)),
        ],
        out_specs=pl.BlockSpec((1, cap, d), lambda e, v: (v, e, 0)),
        out_shape=jax.ShapeDtypeStruct((nb, n_exp * cap, d), BF16),
        scratch_shapes=[pltpu.VMEM((ff, d), BF16)],
        compiler_params=_cparams(("parallel", "arbitrary"), 48),
        name="moe_down",
    )(act, w_down)


def _combine_kernel(starts_ref, y_ref, sel_ref, aff_ref, out_ref, st_ref, *, n_tb, cap, win, n_exp):
    v, j = pl.program_id(0), pl.program_id(2)
    base = v * n_exp * (n_tb + 1)
    w, rounds = _windows(starts_ref, base, n_tb + 1, n_exp, j, cap, win)
    out_ref[...] = jnp.zeros_like(out_ref)
    cols = lax.broadcasted_iota(jnp.int32, (n_exp, n_exp * win), 1)
    spread = jnp.where(cols // win == lax.broadcasted_iota(jnp.int32, (n_exp, n_exp * win), 0),
                       1.0, 0.0).astype(BF16)
    slot_in_win = (lax.broadcasted_iota(jnp.int32, (1, n_exp * win), 1) % win + 1).astype(F32)
    exp_lane = lax.broadcasted_iota(jnp.int32, (1, n_exp), 1)
    a_hi, a_lo = _split(aff_ref[0])
    gate_hi = _dot(a_hi, spread)
    gate_lo = _dot(a_lo, spread)

    def body(r, _):
        lo_vec = jnp.zeros((1, n_exp), F32)
        c_vec = jnp.zeros((1, n_exp), F32)
        for e in range(n_exp):
            lo = w[e] + r * win
            c = pl.multiple_of(jnp.minimum(lo, cap - win), SLOT_ALIGN)
            src = pl.multiple_of(e * cap + c, SLOT_ALIGN)
            st_ref[e * win:(e + 1) * win, :] = y_ref[0, pl.ds(src, win), :]
            lo_vec = jnp.where(exp_lane == e, lo.astype(F32), lo_vec)
            c_vec = jnp.where(exp_lane == e, c.astype(F32), c_vec)
        sel = sel_ref[0]
        rel = jnp.where(sel >= lo_vec, sel, -1.0) - c_vec
        rel1 = jnp.where((rel >= 0.0) & (rel < float(win)), rel + 1.0, 0.0)
        hit = _dot(rel1.astype(BF16), spread) == slot_in_win
        st = st_ref[...]
        out_ref[0] += (_dot(jnp.where(hit, gate_hi, 0.0).astype(BF16), st)
                       + _dot(jnp.where(hit, gate_lo, 0.0).astype(BF16), st))
        return 0

    lax.fori_loop(0, rounds, body, 0)


def _moe_combine(starts, y, sel_t, aff_t, n_tok, cap, win):
    nb, _, d = y.shape
    n_exp = sel_t.shape[2]
    n_tb = n_tok // ROW_BLK
    pw = _largest_tile(d, 1024, LANES)
    tok = pl.BlockSpec((1, ROW_BLK, n_exp), lambda v, p, j, s: (v, j, 0))
    return pl.pallas_call(
        functools.partial(_combine_kernel, n_tb=n_tb, cap=cap, win=win, n_exp=n_exp),
        grid_spec=pltpu.PrefetchScalarGridSpec(
            num_scalar_prefetch=1,
            grid=(nb, d // pw, n_tb),
            in_specs=[pl.BlockSpec((1, n_exp * cap, pw), lambda v, p, j, s: (v, 0, p)), tok, tok],
            out_specs=pl.BlockSpec((1, ROW_BLK, pw), lambda v, p, j, s: (v, j, p)),
            scratch_shapes=[pltpu.VMEM((n_exp * win, pw), BF16)],
        ),
        out_shape=jax.ShapeDtypeStruct((nb, n_tok, d), F32),
        compiler_params=_cparams(("parallel", "parallel", "arbitrary"), 48),
        name="moe_combine",
    )(starts, y, sel_t, aff_t)


def _rope_tables(pos_groups, half):
    freqs = ROPE_BASE ** (-jnp.arange(half, dtype=F32) / half)
    cos, sin = [], []
    for pos in pos_groups:
        ang = pos.astype(F32)[:, None] * freqs[None, :]
        c, s = jnp.cos(ang), jnp.sin(ang)
        cos += [c, c]
        sin += [-s, s]
    return jnp.stack([jnp.concatenate(t, axis=1) for t in (cos, sin)])


def _attn_tables(n_lat, n_ctx):
    s = jnp.arange(n_lat)
    tab = _rope_tables([s // GRID_W, s % GRID_W], HEAD_DIM // 4)
    ident = jnp.stack([jnp.ones((n_ctx, HEAD_DIM), F32), jnp.zeros((n_ctx, HEAD_DIM), F32)])
    return jnp.concatenate([tab, ident], axis=1)


def _ret_tables(n_lat, n_ctx):
    s = jnp.arange(n_lat)
    t = jnp.arange(n_ctx)
    fwd = jnp.concatenate([n_ctx + s, t])
    bwd = jnp.concatenate([n_ctx + (n_lat - 1 - s), n_ctx - 1 - t])
    return jnp.concatenate([_rope_tables([fwd], HEAD_DIM // 2),
                            _rope_tables([bwd], HEAD_DIM // 2)], axis=0)


def kernel(x, c, ctx, c_ctx, w_ada, b_ada, norm_mix, norm_ffn, w_in, q_norm, k_norm, ret_log_decay,
           w_out, w_router, w_gate, w_up, w_down):
    bsz, n_lat, d = x.shape
    n_ctx = ctx.shape[1]
    seq_len = n_lat + n_ctx
    depth = w_ada.shape[0]
    n_exp = w_router.shape[2]
    n_lat_blk = n_lat // ROW_BLK
    assert n_lat % ROW_BLK == 0 and n_ctx == ROW_BLK and bsz + 1 <= 8
    cap_l = CAPACITY_FACTOR * n_lat // n_exp
    cap_c = CAPACITY_FACTOR * n_ctx // n_exp

    cond8 = jnp.zeros((8, d), F32).at[:bsz].set(c).at[bsz].set(c_ctx)
    mods = _adaln(cond8, w_ada, b_ada).reshape(depth, 8, 6, d)

    attn_tab = _attn_tables(n_lat, n_ctx)
    ret_tab = _ret_tables(n_lat, n_ctx)
    wc = _chan_mats()
    two_stage = n_lat % (FFT_COLS * 2 * SUBLANES) == 0
    lat_mats = dict(fft=_fft_mats(n_lat)) if two_stage else dict(dft=_dft_mats(n_lat))
    dft_c = _dft_mats(n_ctx)

    o_q, o_k = 0, ATTN_WIDTH
    o_v = o_k + KV_WIDTH
    o_f = o_v + KV_WIDTH
    o_rq = o_f + FOURIER_WIDTH
    o_rk = o_rq + RET_WIDTH
    o_rv = o_rk + RET_WIDTH
    o_g = o_rv + RET_WIDTH
    o_end = o_g + 2 * RET_WIDTH
    q_scale = HEAD_DIM ** -0.5 * math.log2(math.e)

    sb_c = bsz * cap_c
    win_l = min(cap_l, SLOT_WINDOW)
    win_c = cap_c
    starts_c = jnp.tile(jnp.arange(bsz + 1, dtype=jnp.int32) * cap_c, n_exp)

    xs = jnp.concatenate([x, ctx], axis=1)
    delta_lat = delta_ctx = None
    for l in range(depth):
        xs, h = _norm(xs, mods[l], norm_mix[l], n_lat_blk, delta_lat, delta_ctx,
                      mods[l - 1] if l else None)
        h2d = h.reshape(bsz * seq_len, d)
        wl = w_in[l].astype(BF16)
        q = _proj(h2d, wl[:, o_q:o_k], seq_len, "qk", gain=q_norm[l], tab=attn_tab,
                  n_heads=ATTN_HEADS, scale=q_scale, half=HEAD_DIM // 4)
        k = _proj(h2d, wl[:, o_k:o_v], seq_len, "qk", gain=k_norm[l], tab=attn_tab,
                  n_heads=ATTN_KV_HEADS, scale=1.0, half=HEAD_DIM // 4)
        v, f = _proj(h2d, wl[:, o_v:o_rq], seq_len, "split", split_at=KV_WIDTH)
        rq = _proj(h2d, wl[:, o_rq:o_rk], seq_len, "ret", tab=ret_tab, n_heads=RET_HEADS,
                   scale=1.0, half=HEAD_DIM // 2)
        rk = _proj(h2d, wl[:, o_rk:o_rv], seq_len, "ret", tab=ret_tab, n_heads=RET_HEADS,
                   scale=HEAD_DIM ** -0.5, half=HEAD_DIM // 2)
        rv = _proj(h2d, wl[:, o_rv:o_g], seq_len, "plain")
        sg = _proj(h2d, wl[:, o_g:o_end], seq_len, "silu")

        a = _attention(q, k, v, bsz, seq_len, n_lat)
        fm_lat = _fourier(f, bsz, seq_len, 0, n_lat, wc, **lat_mats)
        fm_ctx = _fourier(f, bsz, seq_len, n_lat, n_ctx, wc, dft=dft_c)
        o_f, o_b = _retention(rq, rk, rv, ret_log_decay[l], bsz, seq_len, n_lat)

        xs, h2, aff_t = _outproj(a, fm_lat, fm_ctx, o_f, o_b, sg, w_out[l].astype(BF16), xs, mods[l],
                                 norm_ffn[l], w_router[l].T, n_lat_blk)

        sel_l, gat_l, sel_c, gat_c, st_l = _routing(aff_t, bsz, seq_len, n_lat, cap_l, cap_c)
        starts_l = st_l[:, :, :n_lat_blk + 1].astype(jnp.int32).reshape(-1)
        merge = lambda t: t.transpose(1, 0, 2).reshape(1, n_exp, bsz * n_ctx)
        tok_major = lambda t: t.transpose(0, 2, 1)
        sel_c, gat_c = merge(sel_c), merge(gat_c)
        h_ctx = h2[:, n_lat:, :].reshape(1, bsz * n_ctx, d)

        xg_l = _moe_gather(starts_l, h2, sel_l, n_lat, cap_l, win_l)
        xg_c = _moe_gather(starts_c, h_ctx, sel_c, bsz * n_ctx, sb_c, win_c)
        y_l = _moe_down(_moe_ffn(xg_l, w_gate, w_up, l), w_down, l, cap_l)
        y_c = _moe_down(_moe_ffn(xg_c, w_gate, w_up, l), w_down, l, sb_c)
        delta_lat = _moe_combine(starts_l, y_l, tok_major(sel_l), tok_major(gat_l), n_lat, cap_l,
                                 win_l)
        delta_ctx = _moe_combine(starts_c, y_c, tok_major(sel_c), tok_major(gat_c), bsz * n_ctx,
                                 sb_c, win_c).reshape(bsz * n_ctx, d)

    out, _ = _norm(xs, None, None, n_lat_blk, delta_lat, delta_ctx, mods[depth - 1],
                   want_h=False, lat_only=True)
    return out
```

```python
import functools
import math

import jax
import jax.numpy as jnp
from jax import lax
from jax.experimental import pallas as pl
from jax.experimental.pallas import tpu as pltpu

HEAD_DIM = 128
ATTN_HEADS = 8
ATTN_KV_HEADS = 2
ATTN_GROUP = ATTN_HEADS // ATTN_KV_HEADS
FOURIER_GROUPS = 4
RET_HEADS = 4
RET_CHUNK = 128
GRID_W = 64
ROPE_BASE = 10000.0
EPS = 1e-6
CAPACITY_FACTOR = 2

ATTN_WIDTH = ATTN_HEADS * HEAD_DIM
KV_WIDTH = ATTN_KV_HEADS * HEAD_DIM
FOURIER_WIDTH = FOURIER_GROUPS * HEAD_DIM
RET_WIDTH = RET_HEADS * HEAD_DIM

ROW_BLK = 256
LANES = 128
MIB = 1024 * 1024

F32 = jnp.float32
BF16 = jnp.bfloat16


def _cparams(sem, vmem_mib):
    return pltpu.CompilerParams(dimension_semantics=sem, vmem_limit_bytes=vmem_mib * MIB)


def _largest_tile(n, cap, mult=8):
    best = mult
    for t in range(mult, min(n, cap) + 1, mult):
        if n % t == 0:
            best = t
    return best


def _split(a):
    hi = a.astype(BF16)
    lo = (a - hi.astype(F32)).astype(BF16)
    return hi, lo


def _dot(a, b):
    return jnp.dot(a, b, preferred_element_type=F32)


def _dot_nt(a, b):
    return lax.dot_general(a, b, (((1,), (1,)), ((), ())), preferred_element_type=F32)


def _dot_tn(a, b):
    return lax.dot_general(a, b, (((0,), (0,)), ((), ())), preferred_element_type=F32)


def _silu(a):
    return a / (1.0 + jnp.exp(-a))


def _adaln_kernel(c_ref, w_ref, b_ref, o_ref):
    s = _silu(c_ref[...])
    sh, sl = _split(s)
    wh, wl = _split(w_ref[0])
    o_ref[0] = _dot(sh, wh) + (_dot(sh, wl) + _dot(sl, wh)) + b_ref[0]


def _adaln(cond8, w_ada, b_ada):
    depth, d, n = w_ada.shape
    tn = _largest_tile(n, 768, LANES)
    return pl.pallas_call(
        _adaln_kernel,
        grid=(depth, n // tn),
        in_specs=[
            pl.BlockSpec((8, d), lambda l, j: (0, 0)),
            pl.BlockSpec((1, d, tn), lambda l, j: (l, 0, j)),
            pl.BlockSpec((1, 1, tn), lambda l, j: (l, 0, j)),
        ],
        out_specs=pl.BlockSpec((1, 8, tn), lambda l, j: (l, 0, j)),
        out_shape=jax.ShapeDtypeStruct((depth, 8, n), F32),
        compiler_params=_cparams(("parallel", "parallel"), 48),
        name="adaln",
    )(cond8, w_ada, b_ada.reshape(depth, 1, n))


def _rms_mod(x, g, shift, scale):
    y = x * lax.rsqrt(jnp.mean(x * x, axis=-1, keepdims=True) + EPS)
    return (y * g) * (1.0 + scale) + shift


def _norm_kernel(*refs, has_delta, want_h, n_lat_blk):
    refs = list(refs)
    x_ref = refs.pop(0)
    x = x_ref[0]
    if has_delta:
        dl_ref, dc_ref, pm_ref = refs.pop(0), refs.pop(0), refs.pop(0)
        is_ctx = pl.program_id(1) >= n_lat_blk
        delta = jnp.where(is_ctx, dc_ref[...], dl_ref[0])
        x = x + pm_ref[0, 5:6, :] * delta
    if want_h:
        m_ref, g_ref = refs.pop(0), refs.pop(0)
    if has_delta:
        xo_ref = refs.pop(0)
        xo_ref[0] = x
    if want_h:
        h_ref = refs.pop(0)
        h_ref[0] = _rms_mod(x, g_ref[...], m_ref[0, 0:1, :], m_ref[0, 1:2, :]).astype(BF16)


def _norm(x, mods, g, n_lat_blk, delta_lat=None, delta_ctx=None, prev_mods=None,
          want_h=True, lat_only=False):
    bsz, l, d = x.shape
    n_blk = n_lat_blk if lat_only else l // ROW_BLK
    has_delta = delta_lat is not None

    def mod_row(b, t):
        return (jnp.where(t >= n_lat_blk, bsz, b), 0, 0)

    xspec = pl.BlockSpec((1, ROW_BLK, d), lambda b, t: (b, t, 0))
    in_specs, args = [xspec], [x]
    if has_delta:
        in_specs += [
            pl.BlockSpec((1, ROW_BLK, d), lambda b, t: (b, jnp.minimum(t, n_lat_blk - 1), 0)),
            pl.BlockSpec((ROW_BLK, d), lambda b, t: (b, 0)),
            pl.BlockSpec((1, 6, d), mod_row),
        ]
        args += [delta_lat, delta_ctx, prev_mods]
    if want_h:
        in_specs += [pl.BlockSpec((1, 6, d), mod_row), pl.BlockSpec((1, d), lambda b, t: (0, 0))]
        args += [mods, g.reshape(1, d)]
    out_specs, out_shape = [], []
    rows = n_blk * ROW_BLK
    if has_delta:
        out_specs.append(xspec)
        out_shape.append(jax.ShapeDtypeStruct((bsz, rows, d), F32))
    if want_h:
        out_specs.append(xspec)
        out_shape.append(jax.ShapeDtypeStruct((bsz, rows, d), BF16))
    outs = pl.pallas_call(
        functools.partial(_norm_kernel, has_delta=has_delta, want_h=want_h, n_lat_blk=n_lat_blk),
        grid=(bsz, n_blk),
        in_specs=in_specs,
        out_specs=out_specs,
        out_shape=out_shape,
        compiler_params=_cparams(("parallel", "parallel"), 32),
        name="norm",
    )(*args)
    outs = list(outs)
    x_new = outs.pop(0) if has_delta else x
    h = outs.pop(0) if want_h else None
    return x_new, h


def _dot_wide(a, m):
    hi, lo = _split(a)
    return _dot(hi, m) + _dot(lo, m)


def _partner_matrix(half):
    src = lax.broadcasted_iota(jnp.int32, (LANES, LANES), 0)
    dst = lax.broadcasted_iota(jnp.int32, (LANES, LANES), 1)
    mate = jnp.where(dst % (2 * half) < half, dst + half, dst - half)
    return jnp.where(src == mate, 1.0, 0.0).astype(BF16)


def _proj_kernel(*refs, flavor, n_heads, scale, half, split_at):
    h_ref, w_ref = refs[0], refs[1]
    acc = _dot(h_ref[...], w_ref[...])
    if flavor == "plain":
        refs[2][...] = acc.astype(BF16)
    elif flavor == "split":
        refs[2][...] = acc[:, :split_at].astype(BF16)
        refs[3][...] = acc[:, split_at:].astype(BF16)
    elif flavor == "silu":
        refs[2][...] = _silu(acc).astype(BF16)
    elif flavor == "qk":
        gain_ref, tab_ref, o_ref = refs[2], refs[3], refs[4]
        partner = _partner_matrix(half)
        ones = jnp.ones((LANES, LANES), BF16)
        for hd in range(n_heads):
            sl = slice(hd * HEAD_DIM, (hd + 1) * HEAD_DIM)
            z = acc[:, sl]
            ms = _dot_wide(z * z, ones) * (1.0 / HEAD_DIM)
            z = (z * lax.rsqrt(ms + EPS)) * gain_ref[...]
            z = z * tab_ref[0] + _dot_wide(z, partner) * tab_ref[1]
            o_ref[:, sl] = (z * scale).astype(BF16)
    elif flavor == "ret":
        tab_ref, o_ref = refs[2], refs[3]
        partner = _partner_matrix(half)
        width = n_heads * HEAD_DIM
        for hd in range(n_heads):
            sl = slice(hd * HEAD_DIM, (hd + 1) * HEAD_DIM)
            z = acc[:, sl] * scale
            pz = _dot_wide(z, partner)
            o_ref[:, sl] = (z * tab_ref[0] + pz * tab_ref[1]).astype(BF16)
            o_ref[:, width + hd * HEAD_DIM: width + (hd + 1) * HEAD_DIM] = (
                z * tab_ref[2] + pz * tab_ref[3]).astype(BF16)
    else:
        raise ValueError(flavor)


def _proj(h2d, w, seq_len, flavor, *, gain=None, tab=None, n_heads=0, scale=1.0, half=0,
          split_at=0):
    t_rows, d = h2d.shape
    n = w.shape[1]
    tm = _largest_tile(seq_len, 1088)
    per_seq = seq_len // tm
    in_specs = [pl.BlockSpec((tm, d), lambda i: (i, 0)), pl.BlockSpec((d, n), lambda i: (0, 0))]
    args = [h2d, w]
    if flavor == "qk":
        in_specs.append(pl.BlockSpec((1, HEAD_DIM), lambda i: (0, 0)))
        args.append(gain.reshape(1, HEAD_DIM))
    if flavor in ("qk", "ret"):
        ntab = tab.shape[0]
        in_specs.append(pl.BlockSpec((ntab, tm, HEAD_DIM), lambda i: (0, i % per_seq, 0)))
        args.append(tab)
    if flavor == "split":
        widths = [split_at, n - split_at]
    elif flavor == "ret":
        widths = [2 * n]
    else:
        widths = [n]
    out_specs = [pl.BlockSpec((tm, wd), lambda i: (i, 0)) for wd in widths]
    out_shape = [jax.ShapeDtypeStruct((t_rows, wd), BF16) for wd in widths]
    outs = pl.pallas_call(
        functools.partial(_proj_kernel, flavor=flavor, n_heads=n_heads, scale=scale, half=half,
                          split_at=split_at),
        grid=(t_rows // tm,),
        in_specs=in_specs,
        out_specs=out_specs,
        out_shape=out_shape,
        compiler_params=_cparams(("parallel",), 48),
        name="proj_" + flavor,
    )(*args)
    return outs if len(outs) > 1 else outs[0]


SUBLANES = 8


SAFE_SHIFT = 50.0


def _lane_fold(t, op):
    return functools.reduce(op, [t[:, i * LANES:(i + 1) * LANES]
                                 for i in range(t.shape[1] // LANES)])


def _attn_group_bounded(qs, ms, k_ref, v_ref, p_ref, k_lo, n_keys, tk):
    n_heads, tq = len(qs), qs[0].shape[0]
    q_all = jnp.concatenate(qs, axis=0)
    m_all = jnp.concatenate(ms, axis=0)
    lp = None
    for c in range(n_keys // tk):
        s = _dot_nt(q_all, k_ref[k_lo + c * tk:k_lo + (c + 1) * tk, :])
        p = jnp.exp2(s - m_all)
        part = _lane_fold(p, jnp.add)
        lp = part if lp is None else lp + part
        for hd in range(n_heads):
            p_ref[hd, :, c * tk:(c + 1) * tk] = p[hd * tq:(hd + 1) * tq, :].astype(BF16)
    l = jnp.sum(lp, axis=-1, keepdims=True)
    v = v_ref[k_lo:k_lo + n_keys, :]
    return [_dot(p_ref[hd, :, :n_keys], v) / l[hd * tq:(hd + 1) * tq, :] for hd in range(n_heads)]


def _attn_head_exact(q, k_ref, v_ref, s_ref, p_ref, k_lo, n_keys, tk):
    s_ref[:, :n_keys] = _dot_nt(q, k_ref[k_lo:k_lo + n_keys, :])
    n_chunks = n_keys // tk
    mp = None
    for c in range(n_chunks):
        part = _lane_fold(s_ref[:, c * tk:(c + 1) * tk], jnp.maximum)
        mp = part if mp is None else jnp.maximum(mp, part)
    m = jnp.max(mp, axis=-1, keepdims=True)
    lp = None
    for c in range(n_chunks):
        p = jnp.exp2(s_ref[:, c * tk:(c + 1) * tk] - m)
        part = _lane_fold(p, jnp.add)
        lp = part if lp is None else lp + part
        p_ref[:, c * tk:(c + 1) * tk] = p.astype(BF16)
    l = jnp.sum(lp, axis=-1, keepdims=True)
    return _dot(p_ref[:, :n_keys], v_ref[k_lo:k_lo + n_keys, :]) / l


def _attn_kernel(q_ref, k_ref, v_ref, o_ref, s_ref, p_ref, kmax_ref, *, tk, n_lat, n_ctx,
                 n_lat_qblk):
    i = pl.program_id(2)
    is_ctx = i >= n_lat_qblk

    @pl.when(i == 0)
    def _():
        kk = k_ref[...].astype(F32)
        kn2 = jnp.max(jnp.sum(kk * kk, axis=-1, keepdims=True), axis=0, keepdims=True)
        kmax_ref[...] = jnp.broadcast_to(jnp.sqrt(kn2), kmax_ref.shape)

    heads = [slice(hd * HEAD_DIM, (hd + 1) * HEAD_DIM) for hd in range(ATTN_GROUP)]
    bounds = []
    for sl in heads:
        qf = q_ref[:, sl].astype(F32)
        qn = jnp.sqrt(jnp.sum(qf * qf, axis=-1, keepdims=True))
        bounds.append(qn * kmax_ref[0:1, 0:1] * (1.0 + 2.0 ** -10))
    worst = functools.reduce(jnp.maximum, [jnp.max(b) for b in bounds])
    bounded_ok = worst <= SAFE_SHIFT

    def run(k_lo, n_keys, bounded):
        if bounded:
            outs = _attn_group_bounded([q_ref[:, sl] for sl in heads], bounds, k_ref, v_ref, p_ref,
                                       k_lo, n_keys, tk)
        else:
            outs = [_attn_head_exact(q_ref[:, sl], k_ref, v_ref, s_ref.at[hd], p_ref.at[hd], k_lo,
                                     n_keys, tk) for hd, sl in enumerate(heads)]
        for o, sl in zip(outs, heads):
            o_ref[:, sl] = o.astype(BF16)

    for ctx_case, (k_lo, n_keys) in ((False, (0, n_lat + n_ctx)), (True, (n_lat, n_ctx))):
        for bounded in (True, False):
            @pl.when((is_ctx == ctx_case) & (bounded_ok == bounded))
            def _(k_lo=k_lo, n_keys=n_keys, bounded=bounded):
                run(k_lo, n_keys, bounded)


def _attention(q, k, v, bsz, seq_len, n_lat):
    tq = ROW_BLK
    tk = ROW_BLK
    per_seq = seq_len // tq
    gw = ATTN_GROUP * HEAD_DIM
    return pl.pallas_call(
        functools.partial(_attn_kernel, tk=tk, n_lat=n_lat, n_ctx=seq_len - n_lat,
                          n_lat_qblk=n_lat // tq),
        grid=(bsz, ATTN_KV_HEADS, per_seq),
        in_specs=[
            pl.BlockSpec((tq, gw), lambda b, g, i: (b * per_seq + i, g)),
            pl.BlockSpec((seq_len, HEAD_DIM), lambda b, g, i: (b, g)),
            pl.BlockSpec((seq_len, HEAD_DIM), lambda b, g, i: (b, g)),
        ],
        out_specs=pl.BlockSpec((tq, gw), lambda b, g, i: (b * per_seq + i, g)),
        out_shape=jax.ShapeDtypeStruct(q.shape, BF16),
        scratch_shapes=[pltpu.VMEM((ATTN_GROUP, tq, seq_len), F32),
                        pltpu.VMEM((ATTN_GROUP, tq, seq_len), BF16),
                        pltpu.VMEM((SUBLANES, LANES), F32)],
        compiler_params=_cparams(("parallel", "parallel", "arbitrary"), 48),
        name="attention",
    )(q, k, v)


def _fourier_a_kernel(f_ref, wc_ref, o_ref):
    g = _dot(f_ref[...], wc_ref[...])
    wdt = f_ref.shape[1]
    o_ref[0, 0] = g[:, :wdt].astype(o_ref.dtype)
    o_ref[0, 1] = g[:, wdt:].astype(o_ref.dtype)


def _fourier_b_kernel(m_ref, g_ref, o_ref):
    o_ref[0] = _dot(m_ref[...], g_ref[0]).astype(BF16)


FFT_COLS = 64
FFT_PER_STEP = SUBLANES


def _fft1_kernel(g_ref, m_ref, tc_ref, ts_ref, o_ref):
    width = o_ref.shape[4]
    n1 = g_ref.shape[2]
    for s in range(g_ref.shape[3]):
        g = jnp.concatenate([g_ref[0, 0, :, s, :], g_ref[0, 1, :, s, :]], axis=0).astype(BF16)
        a = _dot(m_ref[...], g)
        ar, ai = a[:n1], a[n1:]
        tc = tc_ref[0, :, s * LANES:(s + 1) * LANES]
        ts = ts_ref[0, :, s * LANES:(s + 1) * LANES]
        lanes = [slice(q * LANES, (q + 1) * LANES) for q in range(width // LANES)]
        o_ref[0, 0, :, s, :] = jnp.concatenate([ar[:, q] * tc + ai[:, q] * ts for q in lanes], axis=1)
        o_ref[0, 1, :, s, :] = jnp.concatenate([ai[:, q] * tc - ar[:, q] * ts for q in lanes], axis=1)


def _fft2_kernel(b_ref, m_ref, o_ref):
    for s in range(b_ref.shape[2]):
        b = jnp.concatenate([b_ref[0, 0, s], b_ref[0, 1, s]], axis=0).astype(BF16)
        o_ref[0, :, s, :] = _dot(m_ref[...], b)


def _fourier_two_stage(g, fft):
    m1, tc, ts, m3 = fft
    bsz, _, n, width = g.shape
    n2 = m3.shape[0]
    n1 = n // n2
    per = FFT_PER_STEP
    b = pl.pallas_call(
        _fft1_kernel,
        grid=(bsz, n2 // per),
        in_specs=[
            pl.BlockSpec((1, 2, n1, per, width), lambda b, k: (b, 0, 0, k, 0)),
            pl.BlockSpec(m1.shape, lambda b, k: (0, 0)),
            pl.BlockSpec((1, n1, per * LANES), lambda b, k: (k, 0, 0)),
            pl.BlockSpec((1, n1, per * LANES), lambda b, k: (k, 0, 0)),
        ],
        out_specs=pl.BlockSpec((1, 2, n1, per, width), lambda b, k: (b, 0, 0, k, 0)),
        out_shape=jax.ShapeDtypeStruct((bsz, 2, n1, n2, width), F32),
        compiler_params=_cparams(("parallel", "parallel"), 32),
        name="fourier_fft1",
    )(g.reshape(bsz, 2, n1, n2, width), m1, tc, ts)
    out = pl.pallas_call(
        _fft2_kernel,
        grid=(bsz, n1 // per),
        in_specs=[
            pl.BlockSpec((1, 2, per, n2, width), lambda b, k: (b, 0, k, 0, 0)),
            pl.BlockSpec(m3.shape, lambda b, k: (0, 0)),
        ],
        out_specs=pl.BlockSpec((1, n2, per, width), lambda b, k: (b, 0, k, 0)),
        out_shape=jax.ShapeDtypeStruct((bsz, n2, n1, width), F32),
        compiler_params=_cparams(("parallel", "parallel"), 32),
        name="fourier_fft2",
    )(b, m3)
    return out.reshape(bsz, n, width)


def _fft_mats(n):
    n2 = FFT_COLS
    n1 = n // n2
    per = FFT_PER_STEP

    def cs(rows, cols, period):
        ang = ((rows[:, None] * cols[None, :]) % period).astype(F32) * (2.0 * math.pi / period)
        return jnp.cos(ang), jnp.sin(ang)

    i1, i2 = jnp.arange(n1, dtype=jnp.int32), jnp.arange(n2, dtype=jnp.int32)
    c1, s1 = cs(i1, i1, n1)
    m1 = (jnp.block([[c1, s1], [-s1, c1]]) * n1 ** -0.5).astype(BF16)
    c3, s3 = cs(i2, i2, n2)
    m3 = (jnp.concatenate([c3, s3], axis=1) * n2 ** -0.5).astype(BF16)
    tc, ts = cs(i1, i2, n)
    expand = lambda t: jnp.repeat(t.T.reshape(n2 // per, per, n1).transpose(0, 2, 1), LANES,
                                  axis=2)
    return m1, expand(tc), expand(ts), m3


def _fourier(f2d, bsz, seq_len, row_off, n, wc, dft=None, fft=None):
    per_seq = seq_len // ROW_BLK
    off_blk = row_off // ROW_BLK
    nb = n // ROW_BLK
    width = f2d.shape[1]
    g = pl.pallas_call(
        _fourier_a_kernel,
        grid=(bsz, nb),
        in_specs=[
            pl.BlockSpec((ROW_BLK, width), lambda b, t: (b * per_seq + off_blk + t, 0)),
            pl.BlockSpec((width, 2 * width), lambda b, t: (0, 0)),
        ],
        out_specs=pl.BlockSpec((1, 2, ROW_BLK, width), lambda b, t: (b, 0, t, 0)),
        out_shape=jax.ShapeDtypeStruct((bsz, 2, n, width), BF16 if fft is None else F32),
        compiler_params=_cparams(("parallel", "parallel"), 32),
        name="fourier_chan",
    )(f2d, wc)
    if fft is not None:
        return _fourier_two_stage(g, fft)
    g = g.reshape(bsz, 2 * n, width)
    return pl.pallas_call(
        _fourier_b_kernel,
        grid=(nb, bsz),
        in_specs=[
            pl.BlockSpec((ROW_BLK, 2 * n), lambda i, b: (i, 0)),
            pl.BlockSpec((1, 2 * n, width), lambda i, b: (b, 0, 0)),
        ],
        out_specs=pl.BlockSpec((1, ROW_BLK, width), lambda i, b: (b, i, 0)),
        out_shape=jax.ShapeDtypeStruct((bsz, n, width), BF16),
        compiler_params=_cparams(("parallel", "parallel"), 48),
        name="fourier_pos",
    )(dft, g)


def _dft_mats(n):
    i = jnp.arange(n, dtype=jnp.int32)
    prod = (i[:, None] * i[None, :]) % n
    ang = prod.astype(F32) * (2.0 * math.pi / n)
    s = n ** -0.5
    return jnp.concatenate([jnp.cos(ang) * s, jnp.sin(ang) * s], axis=1).astype(BF16)


def _chan_mats():
    i = jnp.arange(HEAD_DIM, dtype=jnp.int32)
    ang = ((i[:, None] * i[None, :]) % HEAD_DIM).astype(F32) * (2.0 * math.pi / HEAD_DIM)
    s = HEAD_DIM ** -0.5
    eye = jnp.eye(FOURIER_GROUPS, dtype=F32)
    c = jnp.kron(eye, jnp.cos(ang) * s)
    sn = jnp.kron(eye, -jnp.sin(ang) * s)
    return jnp.concatenate([c, sn], axis=1).astype(BF16)


RET_HEADS_PER_STEP = 2


def _ret_kernel(ld_ref, qf_ref, qb_ref, kf_ref, kb_ref, v_ref, of_ref, ob_ref, state_ref,
                *, n_lat_chunks, n_ctx_chunks):
    c = RET_CHUNK
    n_all = n_lat_chunks + n_ctx_chunks
    h0 = pl.program_id(1) * RET_HEADS_PER_STEP
    ii = lax.broadcasted_iota(jnp.int32, (c, c), 0).astype(F32)
    jj = lax.broadcasted_iota(jnp.int32, (c, c), 1).astype(F32)
    ri = lax.broadcasted_iota(jnp.int32, (c, 1), 0).astype(F32)
    state_ref[...] = jnp.zeros_like(state_ref)

    chains = []
    for hh in range(RET_HEADS_PER_STEP):
        sl = slice(hh * HEAD_DIM, (hh + 1) * HEAD_DIM)
        for d, (q_ref, k_ref, o_ref) in enumerate(((qf_ref, kf_ref, of_ref),
                                                   (qb_ref, kb_ref, ob_ref))):
            lg = ld_ref[d, h0 + hh]
            diff = ii - jj if d == 0 else jj - ii
            intra = jnp.where(diff >= 0, jnp.exp(lg * jnp.maximum(diff, 0.0)), 0.0)
            q_dec = jnp.exp(lg * (ri + 1.0 if d == 0 else c - ri))
            k_dec = jnp.exp(lg * (c - 1.0 - ri if d == 0 else ri))
            c_dec = jnp.exp(jnp.full((1, HEAD_DIM), lg * c, F32))
            chains.append((len(chains), q_ref, k_ref, o_ref, sl, (intra, q_dec, k_dec, c_dec), d))

    def body(s, _):
        in_ctx = s < n_ctx_chunks
        for slot, q_ref, k_ref, o_ref, sl, (intra, q_dec, k_dec, c_dec), d in chains:
            if d == 0:
                chunk = jnp.where(in_ctx, n_lat_chunks + s, s - n_ctx_chunks)
            else:
                chunk = n_all - 1 - s
            off = pl.multiple_of(chunk * c, c)
            q = q_ref[pl.ds(off, c), sl]
            k = k_ref[pl.ds(off, c), sl]
            v = v_ref[pl.ds(off, c), sl]
            st = state_ref[slot]
            sc = _dot_nt(q, k) * intra
            o = _dot(sc.astype(BF16), v) + _dot(q, st.astype(BF16)) * q_dec
            kd = (k.astype(F32) * k_dec).astype(BF16)
            state_ref[slot] = st * c_dec + _dot_tn(kd, v)
            mu = jnp.mean(o, axis=-1, keepdims=True)
            var = jnp.mean(jnp.square(o - mu), axis=-1, keepdims=True)
            o_ref[pl.ds(off, c), sl] = (o - mu) * lax.rsqrt(var + EPS)
        return 0

    lax.fori_loop(0, n_all, body, 0)


def _retention(rq, rk, rv, log_decay, bsz, seq_len, n_lat):
    hw = RET_HEADS_PER_STEP * HEAD_DIM
    n_hb = RET_HEADS // RET_HEADS_PER_STEP
    fwd = pl.BlockSpec((seq_len, hw), lambda b, h: (b, h))
    bwd = pl.BlockSpec((seq_len, hw), lambda b, h: (b, n_hb + h))
    return pl.pallas_call(
        functools.partial(_ret_kernel, n_lat_chunks=n_lat // RET_CHUNK,
                          n_ctx_chunks=(seq_len - n_lat) // RET_CHUNK),
        grid=(bsz, n_hb),
        in_specs=[pl.BlockSpec(memory_space=pltpu.SMEM), fwd, bwd, fwd, bwd, fwd],
        out_specs=[fwd, fwd],
        out_shape=[jax.ShapeDtypeStruct((bsz * seq_len, RET_WIDTH), F32)] * 2,
        scratch_shapes=[pltpu.VMEM((2 * RET_HEADS_PER_STEP, HEAD_DIM, HEAD_DIM), F32)],
        compiler_params=_cparams(("parallel", "parallel"), 56),
        name="retention",
    )(log_decay, rq, rq, rk, rk, rv)


def _outproj_kernel(a_ref, fl_ref, fc_ref, of_ref, ob_ref, sg_ref, w_ref, x_ref, m_ref, g_ref, wr_ref,
                    xo_ref, h_ref, aff_ref, *, n_lat_blk):
    is_ctx = pl.program_id(1) >= n_lat_blk
    fm = jnp.where(is_ctx, fc_ref[0], fl_ref[0].astype(BF16))
    rw = RET_WIDTH
    r = (sg_ref[:, :rw].astype(F32) * of_ref[...]
         + sg_ref[:, rw:].astype(F32) * ob_ref[...]).astype(BF16)
    a0, f0 = ATTN_WIDTH, ATTN_WIDTH + FOURIER_WIDTH
    y = (_dot(a_ref[...], w_ref[:a0, :]) + _dot(fm, w_ref[a0:f0, :])) + _dot(r, w_ref[f0:, :])
    x = x_ref[0] + m_ref[0, 2:3, :] * y
    xo_ref[0] = x
    h = _rms_mod(x, g_ref[...], m_ref[0, 3:4, :], m_ref[0, 4:5, :])
    h_ref[0] = h.astype(BF16)
    hh, hl = _split(h)
    wh, wl = _split(wr_ref[...])
    lt = _dot_nt(wh, hh) + (_dot_nt(wh, hl) + _dot_nt(wl, hh))
    e = jnp.exp(lt - jnp.max(lt, axis=0, keepdims=True))
    aff_ref[...] = e / jnp.sum(e, axis=0, keepdims=True)


def _outproj(a, fm_lat, fm_ctx, o_f, o_b, sg, w_out, x, mods, g, wr_t, n_lat_blk):
    bsz, l, d = x.shape
    per_seq = l // ROW_BLK
    n_exp = wr_t.shape[0]
    fw = fm_lat.shape[-1]

    def flat(b, t):
        return (b * per_seq + t, 0)

    return pl.pallas_call(
        functools.partial(_outproj_kernel, n_lat_blk=n_lat_blk),
        grid=(bsz, per_seq),
        in_specs=[
            pl.BlockSpec((ROW_BLK, a.shape[1]), flat),
            pl.BlockSpec((1, ROW_BLK, fw), lambda b, t: (b, jnp.minimum(t, n_lat_blk - 1), 0)),
            pl.BlockSpec((1, ROW_BLK, fw), lambda b, t: (b, 0, 0)),
            pl.BlockSpec((ROW_BLK, o_f.shape[1]), flat),
            pl.BlockSpec((ROW_BLK, o_b.shape[1]), flat),
            pl.BlockSpec((ROW_BLK, sg.shape[1]), flat),
            pl.BlockSpec(w_out.shape, lambda b, t: (0, 0)),
            pl.BlockSpec((1, ROW_BLK, d), lambda b, t: (b, t, 0)),
            pl.BlockSpec((1, 6, d), lambda b, t: (jnp.where(t >= n_lat_blk, bsz, b), 0, 0)),
            pl.BlockSpec((1, d), lambda b, t: (0, 0)),
            pl.BlockSpec(wr_t.shape, lambda b, t: (0, 0)),
        ],
        out_specs=[
            pl.BlockSpec((1, ROW_BLK, d), lambda b, t: (b, t, 0)),
            pl.BlockSpec((1, ROW_BLK, d), lambda b, t: (b, t, 0)),
            pl.BlockSpec((n_exp, ROW_BLK), lambda b, t: (0, b * per_seq + t)),
        ],
        out_shape=[
            jax.ShapeDtypeStruct((bsz, l, d), F32),
            jax.ShapeDtypeStruct((bsz, l, d), BF16),
            jax.ShapeDtypeStruct((n_exp, bsz * l), F32),
        ],
        compiler_params=_cparams(("parallel", "parallel"), 48),
        name="outproj",
    )(a, fm_lat, fm_ctx, o_f, o_b, sg, w_out, x, mods, g.reshape(1, d), wr_t)


def _cumsum_lanes(m, out_ref, fin):
    n_exp, n = m.shape
    tri = (lax.broadcasted_iota(jnp.int32, (LANES, LANES), 0)
           <= lax.broadcasted_iota(jnp.int32, (LANES, LANES), 1)).astype(BF16)
    run = jnp.zeros((n_exp, 1), F32)
    befores = []
    for k in range(n // LANES):
        befores.append(run)
        sl = slice(k * LANES, (k + 1) * LANES)
        cnt = _dot(m[:, sl].astype(BF16), tri) + run
        out_ref[0, :, sl] = fin(cnt, sl)
        run = cnt[:, LANES - 1:LANES]
    befores.append(run)
    return befores


def _select(seg, cap, slot_off, sel_ref, tmp_ref):
    bits = pltpu.bitcast(seg, jnp.int32)
    n_exp = seg.shape[0]

    def body(it, t):
        tt = t | lax.shift_left(jnp.int32(1), 30 - it)
        cnt = jnp.sum(jnp.where(bits >= tt, 1.0, 0.0), axis=1, keepdims=True)
        return jnp.where(cnt >= cap, tt, t)

    t = lax.fori_loop(0, 31, body, jnp.zeros((n_exp, 1), jnp.int32))
    gt = bits > t
    eq = bits == t
    need = cap - jnp.sum(jnp.where(gt, 1.0, 0.0), axis=1, keepdims=True)
    eqf = jnp.where(eq, 1.0, 0.0)
    _cumsum_lanes(eqf, tmp_ref, lambda cnt, sl: cnt)
    take = eq & (tmp_ref[0] - eqf < need)
    mask = gt | take
    maskf = jnp.where(mask, 1.0, 0.0)
    return _cumsum_lanes(
        maskf, sel_ref,
        lambda cnt, sl: jnp.where(maskf[:, sl] > 0.5, cnt - 1.0 + slot_off, -1.0))


def _routing_kernel(aff_ref, sel_l, gat_l, sel_c, gat_c, st_l, tmp_l, tmp_c, *, n_lat, cap_l, cap_c):
    b = pl.program_id(0)
    a = aff_ref[...]
    lat = a[:, :n_lat]
    ctx = a[:, n_lat:]
    gat_l[0] = lat
    gat_c[0] = ctx
    befores = _select(lat, float(cap_l), 0.0, sel_l, tmp_l)
    _select(ctx, float(cap_c), (b * cap_c).astype(F32), sel_c, tmp_c)
    lane = lax.broadcasted_iota(jnp.int32, (a.shape[0], LANES), 1)
    st = jnp.zeros((a.shape[0], LANES), F32)
    per_blk = ROW_BLK // LANES
    for j in range(n_lat // ROW_BLK + 1):
        st = jnp.where(lane == j, befores[j * per_blk], st)
    st_l[0] = st


def _routing(aff_t, bsz, seq_len, n_lat, cap_l, cap_c):
    n_exp = aff_t.shape[0]
    n_ctx = seq_len - n_lat
    shp = lambda n: jax.ShapeDtypeStruct((bsz, n_exp, n), F32)
    spec = lambda n: pl.BlockSpec((1, n_exp, n), lambda b: (b, 0, 0))
    return pl.pallas_call(
        functools.partial(_routing_kernel, n_lat=n_lat, cap_l=cap_l, cap_c=cap_c),
        grid=(bsz,),
        in_specs=[pl.BlockSpec((n_exp, seq_len), lambda b: (0, b))],
        out_specs=[spec(n_lat), spec(n_lat), spec(n_ctx), spec(n_ctx), spec(LANES)],
        out_shape=[shp(n_lat), shp(n_lat), shp(n_ctx), shp(n_ctx), shp(LANES)],
        scratch_shapes=[pltpu.VMEM((1, n_exp, n_lat), F32), pltpu.VMEM((1, n_exp, n_ctx), F32)],
        compiler_params=_cparams(("parallel",), 32),
        name="routing",
    )(aff_t)


EXPERT_GROUP = 8
SLOT_ALIGN = 16
SLOT_WINDOW = 64


def _windows(starts_ref, base, stride, n, j, cap, win):
    w, rounds = [], jnp.int32(0)
    for k in range(n):
        s0 = starts_ref[base + k * stride + j]
        s1 = starts_ref[base + k * stride + j + 1]
        wk = jnp.minimum((s0 // SLOT_ALIGN) * SLOT_ALIGN, cap - win)
        w.append(wk)
        rounds = jnp.maximum(rounds, (s1 - wk + win - 1) // win)
    return w, rounds


def _gather_kernel(starts_ref, h_ref, sel_ref, aff_ref, xg_ref, gate_ref, *, n_tb, cap, win, n_exp):
    v, g, j = pl.program_id(0), pl.program_id(1), pl.program_id(2)
    ng = sel_ref.shape[2]
    base = (v * n_exp + g * ng) * (n_tb + 1)

    @pl.when(j == 0)
    def _():
        xg_ref[...] = jnp.zeros_like(xg_ref)
        gate_ref[...] = jnp.zeros_like(gate_ref)

    w, rounds = _windows(starts_ref, base, n_tb + 1, ng, j, cap, win)
    tb = h_ref.shape[1]
    row_i = lax.broadcasted_iota(jnp.int32, (win, tb), 0).astype(F32)

    def body(r, _):
        starts, pieces, gates = [], [], []
        for k in range(ng):
            lo = w[k] + r * win
            c = pl.multiple_of(jnp.minimum(lo, cap - win), SLOT_ALIGN)
            sel = sel_ref[0, 0, k:k + 1, :]
            rel = jnp.where(sel >= lo.astype(F32), sel, -1.0) - c.astype(F32)
            hit = rel == row_i
            pieces.append(jnp.where(hit, 1.0, 0.0).astype(BF16))
            gates.append(jnp.sum(jnp.where(hit, aff_ref[0, 0, k:k + 1, :], 0.0), axis=1,
                                 keepdims=True))
            starts.append(c)
        res = _dot(jnp.concatenate(pieces, axis=0), h_ref[0])
        for k in range(ng):
            rows = pl.ds(starts[k], win)
            xg_ref[k, rows, :] = (xg_ref[k, rows, :].astype(F32)
                                  + res[k * win:(k + 1) * win, :]).astype(BF16)
            gate_ref[k, rows, :] += gates[k]
        return 0

    lax.fori_loop(0, rounds, body, 0)


def _moe_gather(starts, h, sel, aff, n_tok, cap, win):
    nb, _, d = h.shape
    n_exp = sel.shape[1]
    ng = min(EXPERT_GROUP, n_exp)
    n_tb = n_tok // ROW_BLK
    rows = pl.BlockSpec((1, 1, ng, ROW_BLK), lambda v, g, j, s: (v, g, 0, j))
    grouped = lambda t: t.reshape(nb, n_exp // ng, ng, n_tok)
    return pl.pallas_call(
        functools.partial(_gather_kernel, n_tb=n_tb, cap=cap, win=win, n_exp=n_exp),
        grid_spec=pltpu.PrefetchScalarGridSpec(
            num_scalar_prefetch=1,
            grid=(nb, n_exp // ng, n_tb),
            in_specs=[pl.BlockSpec((1, ROW_BLK, d), lambda v, g, j, s: (v, j, 0)), rows, rows],
            out_specs=[pl.BlockSpec((ng, cap, d), lambda v, g, j, s: (g, v, 0)),
                       pl.BlockSpec((ng, cap, 1), lambda v, g, j, s: (g, v, 0))],
        ),
        out_shape=[jax.ShapeDtypeStruct((n_exp, nb * cap, d), BF16),
                   jax.ShapeDtypeStruct((n_exp, nb * cap, 1), F32)],
        compiler_params=_cparams(("parallel", "parallel", "arbitrary"), 56),
        name="moe_gather",
    )(starts, h, grouped(sel), grouped(aff))


def _ffn_kernel(xl_ref, xc_ref, wg_ref, wu_ref, al_ref, ac_ref, wgb_ref, wub_ref, *, n_lat_steps):
    r = pl.program_id(2)

    @pl.when(r == 0)
    def _():
        wgb_ref[...] = wg_ref[0, 0].astype(BF16)
        wub_ref[...] = wu_ref[0, 0].astype(BF16)

    def swiglu(x_ref, o_ref):
        x = x_ref[0]
        o_ref[0] = (_silu(_dot(x, wgb_ref[...])) * _dot(x, wub_ref[...])).astype(BF16)

    @pl.when(r < n_lat_steps)
    def _():
        swiglu(xl_ref, al_ref)

    @pl.when(r == n_lat_steps)
    def _():
        swiglu(xc_ref, ac_ref)


def _moe_ffn(xg_l, xg_c, w_gate, w_up, layer):
    n_exp, rows, d = xg_l.shape
    rows_c = xg_c.shape[1]
    ff = w_gate.shape[3]
    fh = _largest_tile(ff, 512, LANES)
    rblk = _largest_tile(rows, 1024)
    nls = rows // rblk
    last = lambda r: jnp.minimum(r, nls - 1)
    wspec = pl.BlockSpec((1, 1, d, fh), lambda e, f, r: (layer, e, 0, f))
    return pl.pallas_call(
        functools.partial(_ffn_kernel, n_lat_steps=nls),
        grid=(n_exp, ff // fh, nls + 1),
        in_specs=[pl.BlockSpec((1, rblk, d), lambda e, f, r: (e, last(r), 0)),
                  pl.BlockSpec((1, rows_c, d), lambda e, f, r: (e, 0, 0)), wspec, wspec],
        out_specs=[pl.BlockSpec((1, rblk, fh), lambda e, f, r: (e, last(r), f)),
                   pl.BlockSpec((1, rows_c, fh), lambda e, f, r: (e, 0, f))],
        out_shape=[jax.ShapeDtypeStruct((n_exp, rows, ff), BF16),
                   jax.ShapeDtypeStruct((n_exp, rows_c, ff), BF16)],
        scratch_shapes=[pltpu.VMEM((d, fh), BF16), pltpu.VMEM((d, fh), BF16)],
        compiler_params=_cparams(("parallel", "parallel", "arbitrary"), 48),
        name="moe_ffn",
    )(xg_l, xg_c, w_gate, w_up)


def _down_kernel(al_ref, ac_ref, gl_ref, gc_ref, wd_ref, yl_ref, yc_ref, wdb_ref, *, nb):
    v = pl.program_id(1)

    @pl.when(v == 0)
    def _():
        wdb_ref[...] = wd_ref[0, 0].astype(BF16)

    @pl.when(v < nb)
    def _():
        yl_ref[0] = (_dot(al_ref[0], wdb_ref[...]) * gl_ref[0]).astype(BF16)

    @pl.when(v == nb)
    def _():
        yc_ref[0] = (_dot(ac_ref[0], wdb_ref[...]) * gc_ref[0]).astype(BF16)


def _moe_down(act_l, act_c, gate_l, gate_c, w_down, layer, cap):
    n_exp, rows, ff = act_l.shape
    rows_c = act_c.shape[1]
    d = w_down.shape[3]
    nb = rows // cap
    last = lambda v: jnp.minimum(v, nb - 1)
    return pl.pallas_call(
        functools.partial(_down_kernel, nb=nb),
        grid=(n_exp, nb + 1),
        in_specs=[
            pl.BlockSpec((1, cap, ff), lambda e, v: (e, last(v), 0)),
            pl.BlockSpec((1, rows_c, ff), lambda e, v: (e, 0, 0)),
            pl.BlockSpec((1, cap, 1), lambda e, v: (e, last(v), 0)),
            pl.BlockSpec((1, rows_c, 1), lambda e, v: (e, 0, 0)),
            pl.BlockSpec((1, 1, ff, d), lambda e, v: (layer, e, 0, 0)),
        ],
        out_specs=[pl.BlockSpec((1, cap, d), lambda e, v: (last(v), e, 0)),
                   pl.BlockSpec((1, rows_c, d), lambda e, v: (0, e, 0))],
        out_shape=[jax.ShapeDtypeStruct((nb, n_exp * cap, d), BF16),
                   jax.ShapeDtypeStruct((1, n_exp * rows_c, d), BF16)],
        scratch_shapes=[pltpu.VMEM((ff, d), BF16)],
        compiler_params=_cparams(("parallel", "arbitrary"), 48),
        name="moe_down",
    )(act_l, act_c, gate_l, gate_c, w_down)


def _combine_kernel(starts_ref, y_ref, sel_ref, out_ref, st_ref, *, n_tb, cap, win, n_exp):
    v, j = pl.program_id(0), pl.program_id(2)
    base = v * n_exp * (n_tb + 1)
    w, rounds = _windows(starts_ref, base, n_tb + 1, n_exp, j, cap, win)
    out_ref[...] = jnp.zeros_like(out_ref)
    cols = lax.broadcasted_iota(jnp.int32, (n_exp, n_exp * win), 1)
    spread = jnp.where(cols // win == lax.broadcasted_iota(jnp.int32, (n_exp, n_exp * win), 0),
                       1.0, 0.0).astype(BF16)
    slot_in_win = (lax.broadcasted_iota(jnp.int32, (1, n_exp * win), 1) % win + 1).astype(F32)
    exp_lane = lax.broadcasted_iota(jnp.int32, (1, n_exp), 1)

    def body(r, _):
        lo_vec = jnp.zeros((1, n_exp), F32)
        c_vec = jnp.zeros((1, n_exp), F32)
        for e in range(n_exp):
            lo = w[e] + r * win
            c = pl.multiple_of(jnp.minimum(lo, cap - win), SLOT_ALIGN)
            src = pl.multiple_of(e * cap + c, SLOT_ALIGN)
            st_ref[e * win:(e + 1) * win, :] = y_ref[0, pl.ds(src, win), :]
            lo_vec = jnp.where(exp_lane == e, lo.astype(F32), lo_vec)
            c_vec = jnp.where(exp_lane == e, c.astype(F32), c_vec)
        sel = sel_ref[0]
        rel = jnp.where(sel >= lo_vec, sel, -1.0) - c_vec
        rel1 = jnp.where((rel >= 0.0) & (rel < float(win)), rel + 1.0, 0.0)
        hit = _dot(rel1.astype(BF16), spread) == slot_in_win
        out_ref[0] += _dot(jnp.where(hit, 1.0, 0.0).astype(BF16), st_ref[...])
        return 0

    lax.fori_loop(0, rounds, body, 0)


def _moe_combine(starts, y, sel_t, n_tok, cap, win):
    nb, _, d = y.shape
    n_exp = sel_t.shape[2]
    n_tb = n_tok // ROW_BLK
    pw = _largest_tile(d, 1024, LANES)
    return pl.pallas_call(
        functools.partial(_combine_kernel, n_tb=n_tb, cap=cap, win=win, n_exp=n_exp),
        grid_spec=pltpu.PrefetchScalarGridSpec(
            num_scalar_prefetch=1,
            grid=(nb, d // pw, n_tb),
            in_specs=[pl.BlockSpec((1, n_exp * cap, pw), lambda v, p, j, s: (v, 0, p)),
                      pl.BlockSpec((1, ROW_BLK, n_exp), lambda v, p, j, s: (v, j, 0))],
            out_specs=pl.BlockSpec((1, ROW_BLK, pw), lambda v, p, j, s: (v, j, p)),
            scratch_shapes=[pltpu.VMEM((n_exp * win, pw), BF16)],
        ),
        out_shape=jax.ShapeDtypeStruct((nb, n_tok, d), F32),
        compiler_params=_cparams(("parallel", "parallel", "arbitrary"), 48),
        name="moe_combine",
    )(starts, y, sel_t)


def _rope_tables(pos_groups, half):
    freqs = ROPE_BASE ** (-jnp.arange(half, dtype=F32) / half)
    cos, sin = [], []
    for pos in pos_groups:
        ang = pos.astype(F32)[:, None] * freqs[None, :]
        c, s = jnp.cos(ang), jnp.sin(ang)
        cos += [c, c]
        sin += [-s, s]
    return jnp.stack([jnp.concatenate(t, axis=1) for t in (cos, sin)])


def _attn_tables(n_lat, n_ctx):
    s = jnp.arange(n_lat)
    tab = _rope_tables([s // GRID_W, s % GRID_W], HEAD_DIM // 4)
    ident = jnp.stack([jnp.ones((n_ctx, HEAD_DIM), F32), jnp.zeros((n_ctx, HEAD_DIM), F32)])
    return jnp.concatenate([tab, ident], axis=1)


def _ret_tables(n_lat, n_ctx):
    s = jnp.arange(n_lat)
    t = jnp.arange(n_ctx)
    fwd = jnp.concatenate([n_ctx + s, t])
    bwd = jnp.concatenate([n_ctx + (n_lat - 1 - s), n_ctx - 1 - t])
    return jnp.concatenate([_rope_tables([fwd], HEAD_DIM // 2),
                            _rope_tables([bwd], HEAD_DIM // 2)], axis=0)


def kernel(x, c, ctx, c_ctx, w_ada, b_ada, norm_mix, norm_ffn, w_in, q_norm, k_norm, ret_log_decay,
           w_out, w_router, w_gate, w_up, w_down):
    bsz, n_lat, d = x.shape
    n_ctx = ctx.shape[1]
    seq_len = n_lat + n_ctx
    depth = w_ada.shape[0]
    n_exp = w_router.shape[2]
    n_lat_blk = n_lat // ROW_BLK
    assert n_lat % ROW_BLK == 0 and n_ctx == ROW_BLK and bsz + 1 <= 8
    cap_l = CAPACITY_FACTOR * n_lat // n_exp
    cap_c = CAPACITY_FACTOR * n_ctx // n_exp

    cond8 = jnp.zeros((8, d), F32).at[:bsz].set(c).at[bsz].set(c_ctx)
    mods = _adaln(cond8, w_ada, b_ada).reshape(depth, 8, 6, d)

    attn_tab = _attn_tables(n_lat, n_ctx)
    ret_tab = _ret_tables(n_lat, n_ctx)
    wc = _chan_mats()
    two_stage = n_lat % (FFT_COLS * 2 * SUBLANES) == 0
    lat_mats = dict(fft=_fft_mats(n_lat)) if two_stage else dict(dft=_dft_mats(n_lat))
    dft_c = _dft_mats(n_ctx)

    o_q, o_k = 0, ATTN_WIDTH
    o_v = o_k + KV_WIDTH
    o_f = o_v + KV_WIDTH
    o_rq = o_f + FOURIER_WIDTH
    o_rk = o_rq + RET_WIDTH
    o_rv = o_rk + RET_WIDTH
    o_g = o_rv + RET_WIDTH
    o_end = o_g + 2 * RET_WIDTH
    q_scale = HEAD_DIM ** -0.5 * math.log2(math.e)

    sb_c = bsz * cap_c
    win_l = min(cap_l, SLOT_WINDOW)
    win_c = cap_c
    starts_c = jnp.tile(jnp.arange(bsz + 1, dtype=jnp.int32) * cap_c, n_exp)

    xs = jnp.concatenate([x, ctx], axis=1)
    delta_lat = delta_ctx = None
    for l in range(depth):
        xs, h = _norm(xs, mods[l], norm_mix[l], n_lat_blk, delta_lat, delta_ctx,
                      mods[l - 1] if l else None)
        h2d = h.reshape(bsz * seq_len, d)
        wl = w_in[l].astype(BF16)
        q = _proj(h2d, wl[:, o_q:o_k], seq_len, "qk", gain=q_norm[l], tab=attn_tab,
                  n_heads=ATTN_HEADS, scale=q_scale, half=HEAD_DIM // 4)
        k = _proj(h2d, wl[:, o_k:o_v], seq_len, "qk", gain=k_norm[l], tab=attn_tab,
                  n_heads=ATTN_KV_HEADS, scale=1.0, half=HEAD_DIM // 4)
        v, f = _proj(h2d, wl[:, o_v:o_rq], seq_len, "split", split_at=KV_WIDTH)
        rq = _proj(h2d, wl[:, o_rq:o_rk], seq_len, "ret", tab=ret_tab, n_heads=RET_HEADS,
                   scale=1.0, half=HEAD_DIM // 2)
        rk = _proj(h2d, wl[:, o_rk:o_rv], seq_len, "ret", tab=ret_tab, n_heads=RET_HEADS,
                   scale=HEAD_DIM ** -0.5, half=HEAD_DIM // 2)
        rv = _proj(h2d, wl[:, o_rv:o_g], seq_len, "plain")
        sg = _proj(h2d, wl[:, o_g:o_end], seq_len, "silu")

        a = _attention(q, k, v, bsz, seq_len, n_lat)
        fm_lat = _fourier(f, bsz, seq_len, 0, n_lat, wc, **lat_mats)
        fm_ctx = _fourier(f, bsz, seq_len, n_lat, n_ctx, wc, dft=dft_c)
        o_f, o_b = _retention(rq, rk, rv, ret_log_decay[l], bsz, seq_len, n_lat)

        xs, h2, aff_t = _outproj(a, fm_lat, fm_ctx, o_f, o_b, sg, w_out[l].astype(BF16), xs, mods[l],
                                 norm_ffn[l], w_router[l].T, n_lat_blk)

        sel_l, gat_l, sel_c, gat_c, st_l = _routing(aff_t, bsz, seq_len, n_lat, cap_l, cap_c)
        starts_l = st_l[:, :, :n_lat_blk + 1].astype(jnp.int32).reshape(-1)
        merge = lambda t: t.transpose(1, 0, 2).reshape(1, n_exp, bsz * n_ctx)
        tok_major = lambda t: t.transpose(0, 2, 1)
        sel_c, gat_c = merge(sel_c), merge(gat_c)
        h_ctx = h2[:, n_lat:, :].reshape(1, bsz * n_ctx, d)

        xg_l, gate_l = _moe_gather(starts_l, h2, sel_l, gat_l, n_lat, cap_l, win_l)
        xg_c, gate_c = _moe_gather(starts_c, h_ctx, sel_c, gat_c, bsz * n_ctx, sb_c, win_c)
        act_l, act_c = _moe_ffn(xg_l, xg_c, w_gate, w_up, l)
        y_l, y_c = _moe_down(act_l, act_c, gate_l, gate_c, w_down, l, cap_l)
        delta_lat = _moe_combine(starts_l, y_l, tok_major(sel_l), n_lat, cap_l, win_l)
        delta_ctx = _moe_combine(starts_c, y_c, tok_major(sel_c), bsz * n_ctx, sb_c,
                                 win_c).reshape(bsz * n_ctx, d)

    out, _ = _norm(xs, None, None, n_lat_blk, delta_lat, delta_ctx, mods[depth - 1],
                   want_h=False, lat_only=True)
    return out
```

```python
import functools
import math

import jax
import jax.numpy as jnp
from jax import lax
from jax.experimental import pallas as pl
from jax.experimental.pallas import tpu as pltpu

HEAD_DIM = 128
ATTN_HEADS = 8
ATTN_KV_HEADS = 2
ATTN_GROUP = ATTN_HEADS // ATTN_KV_HEADS
FOURIER_GROUPS = 4
RET_HEADS = 4
RET_CHUNK = 128
GRID_W = 64
ROPE_BASE = 10000.0
EPS = 1e-6
CAPACITY_FACTOR = 2

ATTN_WIDTH = ATTN_HEADS * HEAD_DIM
KV_WIDTH = ATTN_KV_HEADS * HEAD_DIM
FOURIER_WIDTH = FOURIER_GROUPS * HEAD_DIM
RET_WIDTH = RET_HEADS * HEAD_DIM

ROW_BLK = 256
LANES = 128
MIB = 1024 * 1024

F32 = jnp.float32
BF16 = jnp.bfloat16


def _cparams(sem, vmem_mib):
    return pltpu.CompilerParams(dimension_semantics=sem, vmem_limit_bytes=vmem_mib * MIB)


def _largest_tile(n, cap, mult=8):
    best = mult
    for t in range(mult, min(n, cap) + 1, mult):
        if n % t == 0:
            best = t
    return best


def _split(a):
    hi = a.astype(BF16)
    lo = (a - hi.astype(F32)).astype(BF16)
    return hi, lo


def _dot(a, b):
    return jnp.dot(a, b, preferred_element_type=F32)


def _dot_nt(a, b):
    return lax.dot_general(a, b, (((1,), (1,)), ((), ())), preferred_element_type=F32)


def _dot_tn(a, b):
    return lax.dot_general(a, b, (((0,), (0,)), ((), ())), preferred_element_type=F32)


def _silu(a):
    return a / (1.0 + jnp.exp(-a))


def _adaln_kernel(c_ref, w_ref, b_ref, o_ref):
    s = _silu(c_ref[...])
    sh, sl = _split(s)
    wh, wl = _split(w_ref[0])
    o_ref[0] = _dot(sh, wh) + (_dot(sh, wl) + _dot(sl, wh)) + b_ref[0]


def _adaln(cond8, w_ada, b_ada):
    depth, d, n = w_ada.shape
    tn = _largest_tile(n, 768, LANES)
    return pl.pallas_call(
        _adaln_kernel,
        grid=(depth, n // tn),
        in_specs=[
            pl.BlockSpec((8, d), lambda l, j: (0, 0)),
            pl.BlockSpec((1, d, tn), lambda l, j: (l, 0, j)),
            pl.BlockSpec((1, 1, tn), lambda l, j: (l, 0, j)),
        ],
        out_specs=pl.BlockSpec((1, 8, tn), lambda l, j: (l, 0, j)),
        out_shape=jax.ShapeDtypeStruct((depth, 8, n), F32),
        compiler_params=_cparams(("parallel", "parallel"), 48),
        name="adaln",
    )(cond8, w_ada, b_ada.reshape(depth, 1, n))


def _rms_mod(x, g, shift, scale):
    y = x * lax.rsqrt(jnp.mean(x * x, axis=-1, keepdims=True) + EPS)
    return (y * g) * (1.0 + scale) + shift


def _norm_kernel(*refs, has_delta, want_h, n_lat_blk):
    refs = list(refs)
    x_ref = refs.pop(0)
    x = x_ref[0]
    if has_delta:
        dl_ref, dc_ref, pm_ref = refs.pop(0), refs.pop(0), refs.pop(0)
        is_ctx = pl.program_id(1) >= n_lat_blk
        delta = jnp.where(is_ctx, dc_ref[...], dl_ref[0])
        x = x + pm_ref[0, 5:6, :] * delta
    if want_h:
        m_ref, g_ref = refs.pop(0), refs.pop(0)
    if has_delta:
        xo_ref = refs.pop(0)
        xo_ref[0] = x
    if want_h:
        h_ref = refs.pop(0)
        h_ref[0] = _rms_mod(x, g_ref[...], m_ref[0, 0:1, :], m_ref[0, 1:2, :]).astype(BF16)


def _norm(x, mods, g, n_lat_blk, delta_lat=None, delta_ctx=None, prev_mods=None,
          want_h=True, lat_only=False):
    bsz, l, d = x.shape
    n_blk = n_lat_blk if lat_only else l // ROW_BLK
    has_delta = delta_lat is not None

    def mod_row(b, t):
        return (jnp.where(t >= n_lat_blk, bsz, b), 0, 0)

    xspec = pl.BlockSpec((1, ROW_BLK, d), lambda b, t: (b, t, 0))
    in_specs, args = [xspec], [x]
    if has_delta:
        in_specs += [
            pl.BlockSpec((1, ROW_BLK, d), lambda b, t: (b, jnp.minimum(t, n_lat_blk - 1), 0)),
            pl.BlockSpec((ROW_BLK, d), lambda b, t: (b, 0)),
            pl.BlockSpec((1, 6, d), mod_row),
        ]
        args += [delta_lat, delta_ctx, prev_mods]
    if want_h:
        in_specs += [pl.BlockSpec((1, 6, d), mod_row), pl.BlockSpec((1, d), lambda b, t: (0, 0))]
        args += [mods, g.reshape(1, d)]
    out_specs, out_shape = [], []
    rows = n_blk * ROW_BLK
    if has_delta:
        out_specs.append(xspec)
        out_shape.append(jax.ShapeDtypeStruct((bsz, rows, d), F32))
    if want_h:
        out_specs.append(xspec)
        out_shape.append(jax.ShapeDtypeStruct((bsz, rows, d), BF16))
    outs = pl.pallas_call(
        functools.partial(_norm_kernel, has_delta=has_delta, want_h=want_h, n_lat_blk=n_lat_blk),
        grid=(bsz, n_blk),
        in_specs=in_specs,
        out_specs=out_specs,
        out_shape=out_shape,
        compiler_params=_cparams(("parallel", "parallel"), 32),
        name="norm",
    )(*args)
    outs = list(outs)
    x_new = outs.pop(0) if has_delta else x
    h = outs.pop(0) if want_h else None
    return x_new, h


def _dot_wide(a, m):
    hi, lo = _split(a)
    return _dot(hi, m) + _dot(lo, m)


def _partner_matrix(half):
    src = lax.broadcasted_iota(jnp.int32, (LANES, LANES), 0)
    dst = lax.broadcasted_iota(jnp.int32, (LANES, LANES), 1)
    mate = jnp.where(dst % (2 * half) < half, dst + half, dst - half)
    return jnp.where(src == mate, 1.0, 0.0).astype(BF16)


def _proj_kernel(*refs, flavor, n_heads, scale, half, split_at):
    h_ref, w_ref = refs[0], refs[1]
    acc = _dot(h_ref[...], w_ref[...])
    if flavor == "plain":
        refs[2][...] = acc.astype(BF16)
    elif flavor == "split":
        refs[2][...] = acc[:, :split_at].astype(BF16)
        refs[3][...] = acc[:, split_at:].astype(BF16)
    elif flavor == "silu":
        refs[2][...] = _silu(acc).astype(BF16)
    elif flavor == "qk":
        gain_ref, tab_ref, o_ref = refs[2], refs[3], refs[4]
        partner = _partner_matrix(half)
        ones = jnp.ones((LANES, LANES), BF16)
        for hd in range(n_heads):
            sl = slice(hd * HEAD_DIM, (hd + 1) * HEAD_DIM)
            z = acc[:, sl]
            ms = _dot_wide(z * z, ones) * (1.0 / HEAD_DIM)
            z = (z * lax.rsqrt(ms + EPS)) * gain_ref[...]
            z = z * tab_ref[0] + _dot_wide(z, partner) * tab_ref[1]
            o_ref[:, sl] = (z * scale).astype(BF16)
    elif flavor == "ret":
        tab_ref, o_ref = refs[2], refs[3]
        partner = _partner_matrix(half)
        width = n_heads * HEAD_DIM
        for hd in range(n_heads):
            sl = slice(hd * HEAD_DIM, (hd + 1) * HEAD_DIM)
            z = acc[:, sl] * scale
            pz = _dot_wide(z, partner)
            o_ref[:, sl] = (z * tab_ref[0] + pz * tab_ref[1]).astype(BF16)
            o_ref[:, width + hd * HEAD_DIM: width + (hd + 1) * HEAD_DIM] = (
                z * tab_ref[2] + pz * tab_ref[3]).astype(BF16)
    else:
        raise ValueError(flavor)


def _proj(h2d, w, seq_len, flavor, *, gain=None, tab=None, n_heads=0, scale=1.0, half=0,
          split_at=0):
    t_rows, d = h2d.shape
    n = w.shape[1]
    tm = _largest_tile(seq_len, 1088)
    per_seq = seq_len // tm
    in_specs = [pl.BlockSpec((tm, d), lambda i: (i, 0)), pl.BlockSpec((d, n), lambda i: (0, 0))]
    args = [h2d, w]
    if flavor == "qk":
        in_specs.append(pl.BlockSpec((1, HEAD_DIM), lambda i: (0, 0)))
        args.append(gain.reshape(1, HEAD_DIM))
    if flavor in ("qk", "ret"):
        ntab = tab.shape[0]
        in_specs.append(pl.BlockSpec((ntab, tm, HEAD_DIM), lambda i: (0, i % per_seq, 0)))
        args.append(tab)
    if flavor == "split":
        widths = [split_at, n - split_at]
    elif flavor == "ret":
        widths = [2 * n]
    else:
        widths = [n]
    out_specs = [pl.BlockSpec((tm, wd), lambda i: (i, 0)) for wd in widths]
    out_shape = [jax.ShapeDtypeStruct((t_rows, wd), BF16) for wd in widths]
    outs = pl.pallas_call(
        functools.partial(_proj_kernel, flavor=flavor, n_heads=n_heads, scale=scale, half=half,
                          split_at=split_at),
        grid=(t_rows // tm,),
        in_specs=in_specs,
        out_specs=out_specs,
        out_shape=out_shape,
        compiler_params=_cparams(("parallel",), 48),
        name="proj_" + flavor,
    )(*args)
    return outs if len(outs) > 1 else outs[0]


SUBLANES = 8


SAFE_SHIFT = 50.0


def _lane_fold(t, op):
    return functools.reduce(op, [t[:, i * LANES:(i + 1) * LANES]
                                 for i in range(t.shape[1] // LANES)])


def _fold_rows(t, op):
    return functools.reduce(op, [t[r:r + SUBLANES, :] for r in range(0, t.shape[0], SUBLANES)])


def _attn_group_bounded(q_all, m_row, k_ref, vt_ref, pt_ref, k_lo, n_keys, tk):
    lp = None
    for c in range(n_keys // tk):
        st = _dot_nt(k_ref[k_lo + c * tk:k_lo + (c + 1) * tk, :], q_all)
        p = jnp.exp2(st - m_row)
        part = _fold_rows(p, jnp.add)
        lp = part if lp is None else lp + part
        pt_ref[c * tk:(c + 1) * tk, :] = p.astype(BF16)
    l = jnp.sum(lp, axis=0, keepdims=True)
    return _dot(vt_ref[:, k_lo:k_lo + n_keys], pt_ref[:n_keys, :]) / l


def _attn_head_exact(q, k_ref, v_ref, s_ref, p_ref, k_lo, n_keys, tk):
    s_ref[:, :n_keys] = _dot_nt(q, k_ref[k_lo:k_lo + n_keys, :])
    n_chunks = n_keys // tk
    mp = None
    for c in range(n_chunks):
        part = _lane_fold(s_ref[:, c * tk:(c + 1) * tk], jnp.maximum)
        mp = part if mp is None else jnp.maximum(mp, part)
    m = jnp.max(mp, axis=-1, keepdims=True)
    lp = None
    for c in range(n_chunks):
        p = jnp.exp2(s_ref[:, c * tk:(c + 1) * tk] - m)
        part = _lane_fold(p, jnp.add)
        lp = part if lp is None else lp + part
        p_ref[:, c * tk:(c + 1) * tk] = p.astype(BF16)
    l = jnp.sum(lp, axis=-1, keepdims=True)
    return _dot(p_ref[:, :n_keys], v_ref[k_lo:k_lo + n_keys, :]) / l


def _attn_kernel(q_ref, k_ref, v_ref, o_ref, s_ref, p_ref, pt_ref, vt_ref, kmax_ref, *, tk, n_lat,
                 n_ctx, n_lat_qblk):
    i = pl.program_id(2)
    is_ctx = i >= n_lat_qblk

    @pl.when(i == 0)
    def _():
        kk = k_ref[...].astype(F32)
        kn2 = jnp.max(jnp.sum(kk * kk, axis=-1, keepdims=True), axis=0, keepdims=True)
        kmax_ref[...] = jnp.broadcast_to(jnp.sqrt(kn2), kmax_ref.shape)
        vt_ref[...] = v_ref[...].astype(F32).T.astype(BF16)

    heads = [slice(hd * HEAD_DIM, (hd + 1) * HEAD_DIM) for hd in range(ATTN_GROUP)]
    tq = q_ref.shape[0]
    q_all = jnp.concatenate([q_ref[:, sl] for sl in heads], axis=0)
    qf = q_all.astype(F32)
    qn2 = _dot_nt(jnp.ones((SUBLANES, HEAD_DIM), BF16), (qf * qf).astype(BF16))[0:1, :]
    m_row = jnp.sqrt(qn2) * kmax_ref[0:1, 0:1] * (1.0 + 2.0 ** -6)
    worst = jnp.max(m_row)
    bounded_ok = worst <= SAFE_SHIFT

    def run(k_lo, n_keys, bounded):
        if bounded:
            ot = _attn_group_bounded(q_all, m_row, k_ref, vt_ref, pt_ref, k_lo, n_keys, tk)
            outs = [ot[:, hd * tq:(hd + 1) * tq].T for hd in range(len(heads))]
        else:
            outs = [_attn_head_exact(q_ref[:, sl], k_ref, v_ref, s_ref.at[hd], p_ref.at[hd], k_lo,
                                     n_keys, tk) for hd, sl in enumerate(heads)]
        for o, sl in zip(outs, heads):
            o_ref[:, sl] = o.astype(BF16)

    for ctx_case, (k_lo, n_keys) in ((False, (0, n_lat + n_ctx)), (True, (n_lat, n_ctx))):
        for bounded in (True, False):
            @pl.when((is_ctx == ctx_case) & (bounded_ok == bounded))
            def _(k_lo=k_lo, n_keys=n_keys, bounded=bounded):
                run(k_lo, n_keys, bounded)


def _attention(q, k, v, bsz, seq_len, n_lat):
    tq = ROW_BLK
    tk = ROW_BLK
    per_seq = seq_len // tq
    gw = ATTN_GROUP * HEAD_DIM
    return pl.pallas_call(
        functools.partial(_attn_kernel, tk=tk, n_lat=n_lat, n_ctx=seq_len - n_lat,
                          n_lat_qblk=n_lat // tq),
        grid=(bsz, ATTN_KV_HEADS, per_seq),
        in_specs=[
            pl.BlockSpec((tq, gw), lambda b, g, i: (b * per_seq + i, g)),
            pl.BlockSpec((seq_len, HEAD_DIM), lambda b, g, i: (b, g)),
            pl.BlockSpec((seq_len, HEAD_DIM), lambda b, g, i: (b, g)),
        ],
        out_specs=pl.BlockSpec((tq, gw), lambda b, g, i: (b * per_seq + i, g)),
        out_shape=jax.ShapeDtypeStruct(q.shape, BF16),
        scratch_shapes=[pltpu.VMEM((ATTN_GROUP, tq, seq_len), F32),
                        pltpu.VMEM((ATTN_GROUP, tq, seq_len), BF16),
                        pltpu.VMEM((seq_len, ATTN_GROUP * tq), BF16),
                        pltpu.VMEM((HEAD_DIM, seq_len), BF16),
                        pltpu.VMEM((SUBLANES, LANES), F32)],
        compiler_params=_cparams(("parallel", "parallel", "arbitrary"), 58),
        name="attention",
    )(q, k, v)


def _fourier_a_kernel(f_ref, wc_ref, o_ref):
    g = _dot(f_ref[...], wc_ref[...])
    wdt = f_ref.shape[1]
    o_ref[0, 0] = g[:, :wdt].astype(o_ref.dtype)
    o_ref[0, 1] = g[:, wdt:].astype(o_ref.dtype)


def _fourier_b_kernel(m_ref, g_ref, o_ref):
    o_ref[0] = _dot(m_ref[...], g_ref[0]).astype(BF16)


FFT_COLS = 64
FFT_PER_STEP = SUBLANES


def _fft1_kernel(g_ref, m_ref, tc_ref, ts_ref, o_ref):
    width = o_ref.shape[4]
    n1 = g_ref.shape[2]
    for s in range(g_ref.shape[3]):
        g = jnp.concatenate([g_ref[0, 0, :, s, :], g_ref[0, 1, :, s, :]], axis=0).astype(BF16)
        a = _dot(m_ref[...], g)
        ar, ai = a[:n1], a[n1:]
        tc = tc_ref[0, :, s * LANES:(s + 1) * LANES]
        ts = ts_ref[0, :, s * LANES:(s + 1) * LANES]
        lanes = [slice(q * LANES, (q + 1) * LANES) for q in range(width // LANES)]
        o_ref[0, 0, :, s, :] = jnp.concatenate([ar[:, q] * tc + ai[:, q] * ts for q in lanes], axis=1)
        o_ref[0, 1, :, s, :] = jnp.concatenate([ai[:, q] * tc - ar[:, q] * ts for q in lanes], axis=1)


def _fft2_kernel(b_ref, m_ref, o_ref):
    for s in range(b_ref.shape[2]):
        b = jnp.concatenate([b_ref[0, 0, s], b_ref[0, 1, s]], axis=0).astype(BF16)
        o_ref[0, :, s, :] = _dot(m_ref[...], b)


def _fourier_two_stage(g, fft):
    m1, tc, ts, m3 = fft
    bsz, _, n, width = g.shape
    n2 = m3.shape[0]
    n1 = n // n2
    per = FFT_PER_STEP
    b = pl.pallas_call(
        _fft1_kernel,
        grid=(bsz, n2 // per),
        in_specs=[
            pl.BlockSpec((1, 2, n1, per, width), lambda b, k: (b, 0, 0, k, 0)),
            pl.BlockSpec(m1.shape, lambda b, k: (0, 0)),
            pl.BlockSpec((1, n1, per * LANES), lambda b, k: (k, 0, 0)),
            pl.BlockSpec((1, n1, per * LANES), lambda b, k: (k, 0, 0)),
        ],
        out_specs=pl.BlockSpec((1, 2, n1, per, width), lambda b, k: (b, 0, 0, k, 0)),
        out_shape=jax.ShapeDtypeStruct((bsz, 2, n1, n2, width), F32),
        compiler_params=_cparams(("parallel", "parallel"), 32),
        name="fourier_fft1",
    )(g.reshape(bsz, 2, n1, n2, width), m1, tc, ts)
    out = pl.pallas_call(
        _fft2_kernel,
        grid=(bsz, n1 // per),
        in_specs=[
            pl.BlockSpec((1, 2, per, n2, width), lambda b, k: (b, 0, k, 0, 0)),
            pl.BlockSpec(m3.shape, lambda b, k: (0, 0)),
        ],
        out_specs=pl.BlockSpec((1, n2, per, width), lambda b, k: (b, 0, k, 0)),
        out_shape=jax.ShapeDtypeStruct((bsz, n2, n1, width), F32),
        compiler_params=_cparams(("parallel", "parallel"), 32),
        name="fourier_fft2",
    )(b, m3)
    return out.reshape(bsz, n, width)


def _fft_mats(n):
    n2 = FFT_COLS
    n1 = n // n2
    per = FFT_PER_STEP

    def cs(rows, cols, period):
        ang = ((rows[:, None] * cols[None, :]) % period).astype(F32) * (2.0 * math.pi / period)
        return jnp.cos(ang), jnp.sin(ang)

    i1, i2 = jnp.arange(n1, dtype=jnp.int32), jnp.arange(n2, dtype=jnp.int32)
    c1, s1 = cs(i1, i1, n1)
    m1 = (jnp.block([[c1, s1], [-s1, c1]]) * n1 ** -0.5).astype(BF16)
    c3, s3 = cs(i2, i2, n2)
    m3 = (jnp.concatenate([c3, s3], axis=1) * n2 ** -0.5).astype(BF16)
    tc, ts = cs(i1, i2, n)
    expand = lambda t: jnp.repeat(t.T.reshape(n2 // per, per, n1).transpose(0, 2, 1), LANES,
                                  axis=2)
    return m1, expand(tc), expand(ts), m3


def _fourier(f2d, bsz, seq_len, row_off, n, wc, dft=None, fft=None):
    per_seq = seq_len // ROW_BLK
    off_blk = row_off // ROW_BLK
    nb = n // ROW_BLK
    width = f2d.shape[1]
    g = pl.pallas_call(
        _fourier_a_kernel,
        grid=(bsz, nb),
        in_specs=[
            pl.BlockSpec((ROW_BLK, width), lambda b, t: (b * per_seq + off_blk + t, 0)),
            pl.BlockSpec((width, 2 * width), lambda b, t: (0, 0)),
        ],
        out_specs=pl.BlockSpec((1, 2, ROW_BLK, width), lambda b, t: (b, 0, t, 0)),
        out_shape=jax.ShapeDtypeStruct((bsz, 2, n, width), BF16 if fft is None else F32),
        compiler_params=_cparams(("parallel", "parallel"), 32),
        name="fourier_chan",
    )(f2d, wc)
    if fft is not None:
        return _fourier_two_stage(g, fft)
    g = g.reshape(bsz, 2 * n, width)
    return pl.pallas_call(
        _fourier_b_kernel,
        grid=(nb, bsz),
        in_specs=[
            pl.BlockSpec((ROW_BLK, 2 * n), lambda i, b: (i, 0)),
            pl.BlockSpec((1, 2 * n, width), lambda i, b: (b, 0, 0)),
        ],
        out_specs=pl.BlockSpec((1, ROW_BLK, width), lambda i, b: (b, i, 0)),
        out_shape=jax.ShapeDtypeStruct((bsz, n, width), BF16),
        compiler_params=_cparams(("parallel", "parallel"), 48),
        name="fourier_pos",
    )(dft, g)


def _dft_mats(n):
    i = jnp.arange(n, dtype=jnp.int32)
    prod = (i[:, None] * i[None, :]) % n
    ang = prod.astype(F32) * (2.0 * math.pi / n)
    s = n ** -0.5
    return jnp.concatenate([jnp.cos(ang) * s, jnp.sin(ang) * s], axis=1).astype(BF16)


def _chan_mats():
    i = jnp.arange(HEAD_DIM, dtype=jnp.int32)
    ang = ((i[:, None] * i[None, :]) % HEAD_DIM).astype(F32) * (2.0 * math.pi / HEAD_DIM)
    s = HEAD_DIM ** -0.5
    eye = jnp.eye(FOURIER_GROUPS, dtype=F32)
    c = jnp.kron(eye, jnp.cos(ang) * s)
    sn = jnp.kron(eye, -jnp.sin(ang) * s)
    return jnp.concatenate([c, sn], axis=1).astype(BF16)


RET_HEADS_PER_STEP = 2


def _ret_kernel(ld_ref, qf_ref, qb_ref, kf_ref, kb_ref, v_ref, of_ref, ob_ref, state_ref,
                *, n_lat_chunks, n_ctx_chunks):
    c = RET_CHUNK
    n_all = n_lat_chunks + n_ctx_chunks
    h0 = pl.program_id(1) * RET_HEADS_PER_STEP
    ii = lax.broadcasted_iota(jnp.int32, (c, c), 0).astype(F32)
    jj = lax.broadcasted_iota(jnp.int32, (c, c), 1).astype(F32)
    ri = lax.broadcasted_iota(jnp.int32, (c, 1), 0).astype(F32)
    state_ref[...] = jnp.zeros_like(state_ref)

    chains = []
    for hh in range(RET_HEADS_PER_STEP):
        sl = slice(hh * HEAD_DIM, (hh + 1) * HEAD_DIM)
        for d, (q_ref, k_ref, o_ref) in enumerate(((qf_ref, kf_ref, of_ref),
                                                   (qb_ref, kb_ref, ob_ref))):
            lg = ld_ref[d, h0 + hh]
            diff = ii - jj if d == 0 else jj - ii
            intra = jnp.where(diff >= 0, jnp.exp(lg * jnp.maximum(diff, 0.0)), 0.0)
            q_dec = jnp.exp(lg * (ri + 1.0 if d == 0 else c - ri))
            k_dec = jnp.exp(lg * (c - 1.0 - ri if d == 0 else ri))
            c_dec = jnp.exp(jnp.full((1, HEAD_DIM), lg * c, F32))
            chains.append((len(chains), q_ref, k_ref, o_ref, sl, (intra, q_dec, k_dec, c_dec), d))

    def body(s, _):
        in_ctx = s < n_ctx_chunks
        for slot, q_ref, k_ref, o_ref, sl, (intra, q_dec, k_dec, c_dec), d in chains:
            if d == 0:
                chunk = jnp.where(in_ctx, n_lat_chunks + s, s - n_ctx_chunks)
            else:
                chunk = n_all - 1 - s
            off = pl.multiple_of(chunk * c, c)
            q = q_ref[pl.ds(off, c), sl]
            k = k_ref[pl.ds(off, c), sl]
            v = v_ref[pl.ds(off, c), sl]
            st = state_ref[slot]
            sc = _dot_nt(q, k) * intra
            o = _dot(sc.astype(BF16), v) + _dot(q, st.astype(BF16)) * q_dec
            kd = (k.astype(F32) * k_dec).astype(BF16)
            state_ref[slot] = st * c_dec + _dot_tn(kd, v)
            mu = jnp.mean(o, axis=-1, keepdims=True)
            var = jnp.mean(jnp.square(o - mu), axis=-1, keepdims=True)
            o_ref[pl.ds(off, c), sl] = (o - mu) * lax.rsqrt(var + EPS)
        return 0

    lax.fori_loop(0, n_all, body, 0)


def _retention(rq, rk, rv, log_decay, bsz, seq_len, n_lat):
    hw = RET_HEADS_PER_STEP * HEAD_DIM
    n_hb = RET_HEADS // RET_HEADS_PER_STEP
    fwd = pl.BlockSpec((seq_len, hw), lambda b, h: (b, h))
    bwd = pl.BlockSpec((seq_len, hw), lambda b, h: (b, n_hb + h))
    return pl.pallas_call(
        functools.partial(_ret_kernel, n_lat_chunks=n_lat // RET_CHUNK,
                          n_ctx_chunks=(seq_len - n_lat) // RET_CHUNK),
        grid=(bsz, n_hb),
        in_specs=[pl.BlockSpec(memory_space=pltpu.SMEM), fwd, bwd, fwd, bwd, fwd],
        out_specs=[fwd, fwd],
        out_shape=[jax.ShapeDtypeStruct((bsz * seq_len, RET_WIDTH), F32)] * 2,
        scratch_shapes=[pltpu.VMEM((2 * RET_HEADS_PER_STEP, HEAD_DIM, HEAD_DIM), F32)],
        compiler_params=_cparams(("parallel", "parallel"), 56),
        name="retention",
    )(log_decay, rq, rq, rk, rk, rv)


def _outproj_kernel(a_ref, fl_ref, fc_ref, of_ref, ob_ref, sg_ref, w_ref, x_ref, m_ref, g_ref, wr_ref,
                    xo_ref, h_ref, aff_ref, *, n_lat_blk):
    is_ctx = pl.program_id(1) >= n_lat_blk
    fm = jnp.where(is_ctx, fc_ref[0], fl_ref[0].astype(BF16))
    rw = RET_WIDTH
    r = (sg_ref[:, :rw].astype(F32) * of_ref[...]
         + sg_ref[:, rw:].astype(F32) * ob_ref[...]).astype(BF16)
    a0, f0 = ATTN_WIDTH, ATTN_WIDTH + FOURIER_WIDTH
    y = (_dot(a_ref[...], w_ref[:a0, :]) + _dot(fm, w_ref[a0:f0, :])) + _dot(r, w_ref[f0:, :])
    x = x_ref[0] + m_ref[0, 2:3, :] * y
    xo_ref[0] = x
    h = _rms_mod(x, g_ref[...], m_ref[0, 3:4, :], m_ref[0, 4:5, :])
    h_ref[0] = h.astype(BF16)
    hh, hl = _split(h)
    wh, wl = _split(wr_ref[...])
    lt = _dot_nt(wh, hh) + (_dot_nt(wh, hl) + _dot_nt(wl, hh))
    e = jnp.exp(lt - jnp.max(lt, axis=0, keepdims=True))
    aff_ref[...] = e / jnp.sum(e, axis=0, keepdims=True)


def _outproj(a, fm_lat, fm_ctx, o_f, o_b, sg, w_out, x, mods, g, wr_t, n_lat_blk):
    bsz, l, d = x.shape
    per_seq = l // ROW_BLK
    n_exp = wr_t.shape[0]
    fw = fm_lat.shape[-1]

    def flat(b, t):
        return (b * per_seq + t, 0)

    return pl.pallas_call(
        functools.partial(_outproj_kernel, n_lat_blk=n_lat_blk),
        grid=(bsz, per_seq),
        in_specs=[
            pl.BlockSpec((ROW_BLK, a.shape[1]), flat),
            pl.BlockSpec((1, ROW_BLK, fw), lambda b, t: (b, jnp.minimum(t, n_lat_blk - 1), 0)),
            pl.BlockSpec((1, ROW_BLK, fw), lambda b, t: (b, 0, 0)),
            pl.BlockSpec((ROW_BLK, o_f.shape[1]), flat),
            pl.BlockSpec((ROW_BLK, o_b.shape[1]), flat),
            pl.BlockSpec((ROW_BLK, sg.shape[1]), flat),
            pl.BlockSpec(w_out.shape, lambda b, t: (0, 0)),
            pl.BlockSpec((1, ROW_BLK, d), lambda b, t: (b, t, 0)),
            pl.BlockSpec((1, 6, d), lambda b, t: (jnp.where(t >= n_lat_blk, bsz, b), 0, 0)),
            pl.BlockSpec((1, d), lambda b, t: (0, 0)),
            pl.BlockSpec(wr_t.shape, lambda b, t: (0, 0)),
        ],
        out_specs=[
            pl.BlockSpec((1, ROW_BLK, d), lambda b, t: (b, t, 0)),
            pl.BlockSpec((1, ROW_BLK, d), lambda b, t: (b, t, 0)),
            pl.BlockSpec((n_exp, ROW_BLK), lambda b, t: (0, b * per_seq + t)),
        ],
        out_shape=[
            jax.ShapeDtypeStruct((bsz, l, d), F32),
            jax.ShapeDtypeStruct((bsz, l, d), BF16),
            jax.ShapeDtypeStruct((n_exp, bsz * l), F32),
        ],
        compiler_params=_cparams(("parallel", "parallel"), 48),
        name="outproj",
    )(a, fm_lat, fm_ctx, o_f, o_b, sg, w_out, x, mods, g.reshape(1, d), wr_t)


def _cumsum_lanes(m, out_ref, fin):
    n_exp, n = m.shape
    tri = (lax.broadcasted_iota(jnp.int32, (LANES, LANES), 0)
           <= lax.broadcasted_iota(jnp.int32, (LANES, LANES), 1)).astype(BF16)
    run = jnp.zeros((n_exp, 1), F32)
    befores = []
    for k in range(n // LANES):
        befores.append(run)
        sl = slice(k * LANES, (k + 1) * LANES)
        cnt = _dot(m[:, sl].astype(BF16), tri) + run
        out_ref[0, :, sl] = fin(cnt, sl)
        run = cnt[:, LANES - 1:LANES]
    befores.append(run)
    return befores


def _select(seg, cap, slot_off, sel_ref, tmp_ref):
    bits = pltpu.bitcast(seg, jnp.int32)
    n_exp = seg.shape[0]

    def body(it, t):
        tt = t | lax.shift_left(jnp.int32(1), 30 - it)
        cnt = jnp.sum(jnp.where(bits >= tt, 1.0, 0.0), axis=1, keepdims=True)
        return jnp.where(cnt >= cap, tt, t)

    t = lax.fori_loop(0, 31, body, jnp.zeros((n_exp, 1), jnp.int32))
    gt = bits > t
    eq = bits == t
    need = cap - jnp.sum(jnp.where(gt, 1.0, 0.0), axis=1, keepdims=True)
    eqf = jnp.where(eq, 1.0, 0.0)
    _cumsum_lanes(eqf, tmp_ref, lambda cnt, sl: cnt)
    take = eq & (tmp_ref[0] - eqf < need)
    mask = gt | take
    maskf = jnp.where(mask, 1.0, 0.0)
    return _cumsum_lanes(
        maskf, sel_ref,
        lambda cnt, sl: jnp.where(maskf[:, sl] > 0.5, cnt - 1.0 + slot_off, -1.0))


def _routing_kernel(aff_ref, sel_l, gat_l, sel_c, gat_c, st_l, tmp_l, tmp_c, *, n_lat, cap_l, cap_c):
    b = pl.program_id(0)
    a = aff_ref[...]
    lat = a[:, :n_lat]
    ctx = a[:, n_lat:]
    gat_l[0] = lat
    gat_c[0] = ctx
    befores = _select(lat, float(cap_l), 0.0, sel_l, tmp_l)
    _select(ctx, float(cap_c), (b * cap_c).astype(F32), sel_c, tmp_c)
    lane = lax.broadcasted_iota(jnp.int32, (a.shape[0], LANES), 1)
    st = jnp.zeros((a.shape[0], LANES), F32)
    per_blk = ROW_BLK // LANES
    for j in range(n_lat // ROW_BLK + 1):
        st = jnp.where(lane == j, befores[j * per_blk], st)
    st_l[0] = st


def _routing(aff_t, bsz, seq_len, n_lat, cap_l, cap_c):
    n_exp = aff_t.shape[0]
    n_ctx = seq_len - n_lat
    shp = lambda n: jax.ShapeDtypeStruct((bsz, n_exp, n), F32)
    spec = lambda n: pl.BlockSpec((1, n_exp, n), lambda b: (b, 0, 0))
    return pl.pallas_call(
        functools.partial(_routing_kernel, n_lat=n_lat, cap_l=cap_l, cap_c=cap_c),
        grid=(bsz,),
        in_specs=[pl.BlockSpec((n_exp, seq_len), lambda b: (0, b))],
        out_specs=[spec(n_lat), spec(n_lat), spec(n_ctx), spec(n_ctx), spec(LANES)],
        out_shape=[shp(n_lat), shp(n_lat), shp(n_ctx), shp(n_ctx), shp(LANES)],
        scratch_shapes=[pltpu.VMEM((1, n_exp, n_lat), F32), pltpu.VMEM((1, n_exp, n_ctx), F32)],
        compiler_params=_cparams(("parallel",), 32),
        name="routing",
    )(aff_t)


EXPERT_GROUP = 8
SLOT_ALIGN = 16
SLOT_WINDOW = 64


def _windows(starts_ref, base, stride, n, j, cap, win):
    w, rounds = [], jnp.int32(0)
    for k in range(n):
        s0 = starts_ref[base + k * stride + j]
        s1 = starts_ref[base + k * stride + j + 1]
        wk = jnp.minimum((s0 // SLOT_ALIGN) * SLOT_ALIGN, cap - win)
        w.append(wk)
        rounds = jnp.maximum(rounds, (s1 - wk + win - 1) // win)
    return w, rounds


def _gather_kernel(starts_ref, h_ref, sel_ref, aff_ref, xg_ref, gate_ref, *, n_tb, cap, win, n_exp):
    v, g, j = pl.program_id(0), pl.program_id(1), pl.program_id(2)
    ng = sel_ref.shape[2]
    base = (v * n_exp + g * ng) * (n_tb + 1)

    @pl.when(j == 0)
    def _():
        xg_ref[...] = jnp.zeros_like(xg_ref)
        gate_ref[...] = jnp.zeros_like(gate_ref)

    w, rounds = _windows(starts_ref, base, n_tb + 1, ng, j, cap, win)
    tb = h_ref.shape[1]
    row_i = lax.broadcasted_iota(jnp.int32, (win, tb), 0).astype(F32)

    def body(r, _):
        starts, pieces, gates = [], [], []
        for k in range(ng):
            lo = w[k] + r * win
            c = pl.multiple_of(jnp.minimum(lo, cap - win), SLOT_ALIGN)
            sel = sel_ref[0, 0, k:k + 1, :]
            rel = jnp.where(sel >= lo.astype(F32), sel, -1.0) - c.astype(F32)
            hit = rel == row_i
            pieces.append(jnp.where(hit, 1.0, 0.0).astype(BF16))
            gates.append(jnp.sum(jnp.where(hit, aff_ref[0, 0, k:k + 1, :], 0.0), axis=1,
                                 keepdims=True))
            starts.append(c)
        res = _dot(jnp.concatenate(pieces, axis=0), h_ref[0])
        for k in range(ng):
            rows = pl.ds(starts[k], win)
            xg_ref[k, rows, :] = (xg_ref[k, rows, :].astype(F32)
                                  + res[k * win:(k + 1) * win, :]).astype(BF16)
            gate_ref[k, rows, :] += gates[k]
        return 0

    lax.fori_loop(0, rounds, body, 0)


def _moe_gather(starts, h, sel, aff, n_tok, cap, win):
    nb, _, d = h.shape
    n_exp = sel.shape[1]
    ng = min(EXPERT_GROUP, n_exp)
    n_tb = n_tok // ROW_BLK
    rows = pl.BlockSpec((1, 1, ng, ROW_BLK), lambda v, g, j, s: (v, g, 0, j))
    grouped = lambda t: t.reshape(nb, n_exp // ng, ng, n_tok)
    return pl.pallas_call(
        functools.partial(_gather_kernel, n_tb=n_tb, cap=cap, win=win, n_exp=n_exp),
        grid_spec=pltpu.PrefetchScalarGridSpec(
            num_scalar_prefetch=1,
            grid=(nb, n_exp // ng, n_tb),
            in_specs=[pl.BlockSpec((1, ROW_BLK, d), lambda v, g, j, s: (v, j, 0)), rows, rows],
            out_specs=[pl.BlockSpec((ng, cap, d), lambda v, g, j, s: (g, v, 0)),
                       pl.BlockSpec((ng, cap, 1), lambda v, g, j, s: (g, v, 0))],
        ),
        out_shape=[jax.ShapeDtypeStruct((n_exp, nb * cap, d), BF16),
                   jax.ShapeDtypeStruct((n_exp, nb * cap, 1), F32)],
        compiler_params=_cparams(("parallel", "parallel", "arbitrary"), 56),
        name="moe_gather",
    )(starts, h, grouped(sel), grouped(aff))


def _ffn_kernel(xl_ref, xc_ref, wg_ref, wu_ref, al_ref, ac_ref, wgb_ref, wub_ref):
    r = pl.program_id(2)

    @pl.when(r == 0)
    def _():
        wgb_ref[...] = wg_ref[0, 0].astype(BF16)
        wub_ref[...] = wu_ref[0, 0].astype(BF16)

    def swiglu(x_ref, o_ref):
        x = x_ref[0]
        o_ref[0] = (_silu(_dot(x, wgb_ref[...])) * _dot(x, wub_ref[...])).astype(BF16)

    @pl.when(r == 0)
    def _():
        swiglu(xc_ref, ac_ref)

    @pl.when(r > 0)
    def _():
        swiglu(xl_ref, al_ref)


def _moe_ffn(xg_l, xg_c, w_gate, w_up, layer):
    n_exp, rows, d = xg_l.shape
    rows_c = xg_c.shape[1]
    ff = w_gate.shape[3]
    fh = _largest_tile(ff, 512, LANES)
    rblk = _largest_tile(rows, 1024)
    nls = rows // rblk
    prev = lambda r: jnp.maximum(r - 1, 0)
    wspec = pl.BlockSpec((1, 1, d, fh), lambda e, f, r: (layer, e, 0, f))
    return pl.pallas_call(
        _ffn_kernel,
        grid=(n_exp, ff // fh, nls + 1),
        in_specs=[pl.BlockSpec((1, rblk, d), lambda e, f, r: (e, prev(r), 0)),
                  pl.BlockSpec((1, rows_c, d), lambda e, f, r: (e, 0, 0)), wspec, wspec],
        out_specs=[pl.BlockSpec((1, rblk, fh), lambda e, f, r: (e, prev(r), f)),
                   pl.BlockSpec((1, rows_c, fh), lambda e, f, r: (e, 0, f))],
        out_shape=[jax.ShapeDtypeStruct((n_exp, rows, ff), BF16),
                   jax.ShapeDtypeStruct((n_exp, rows_c, ff), BF16)],
        scratch_shapes=[pltpu.VMEM((d, fh), BF16), pltpu.VMEM((d, fh), BF16)],
        compiler_params=_cparams(("parallel", "parallel", "arbitrary"), 48),
        name="moe_ffn",
    )(xg_l, xg_c, w_gate, w_up)


def _down_kernel(al_ref, ac_ref, gl_ref, gc_ref, wd_ref, yl_ref, yc_ref, wdb_ref):
    v = pl.program_id(1)

    @pl.when(v == 0)
    def _():
        wdb_ref[...] = wd_ref[0, 0].astype(BF16)

    @pl.when(v == 0)
    def _():
        yc_ref[0] = (_dot(ac_ref[0], wdb_ref[...]) * gc_ref[0]).astype(BF16)

    @pl.when(v > 0)
    def _():
        yl_ref[0] = (_dot(al_ref[0], wdb_ref[...]) * gl_ref[0]).astype(BF16)


def _moe_down(act_l, act_c, gate_l, gate_c, w_down, layer, cap):
    n_exp, rows, ff = act_l.shape
    rows_c = act_c.shape[1]
    d = w_down.shape[3]
    nb = rows // cap
    prev = lambda v: jnp.maximum(v - 1, 0)
    return pl.pallas_call(
        _down_kernel,
        grid=(n_exp, nb + 1),
        in_specs=[
            pl.BlockSpec((1, cap, ff), lambda e, v: (e, prev(v), 0)),
            pl.BlockSpec((1, rows_c, ff), lambda e, v: (e, 0, 0)),
            pl.BlockSpec((1, cap, 1), lambda e, v: (e, prev(v), 0)),
            pl.BlockSpec((1, rows_c, 1), lambda e, v: (e, 0, 0)),
            pl.BlockSpec((1, 1, ff, d), lambda e, v: (layer, e, 0, 0)),
        ],
        out_specs=[pl.BlockSpec((1, cap, d), lambda e, v: (prev(v), e, 0)),
                   pl.BlockSpec((1, rows_c, d), lambda e, v: (0, e, 0))],
        out_shape=[jax.ShapeDtypeStruct((nb, n_exp * cap, d), BF16),
                   jax.ShapeDtypeStruct((1, n_exp * rows_c, d), BF16)],
        scratch_shapes=[pltpu.VMEM((ff, d), BF16)],
        compiler_params=_cparams(("parallel", "arbitrary"), 48),
        name="moe_down",
    )(act_l, act_c, gate_l, gate_c, w_down)


def _combine_kernel(starts_ref, y_ref, sel_ref, out_ref, st_ref, *, n_tb, cap, win, n_exp):
    v, j = pl.program_id(0), pl.program_id(2)
    base = v * n_exp * (n_tb + 1)
    w, rounds = _windows(starts_ref, base, n_tb + 1, n_exp, j, cap, win)
    out_ref[...] = jnp.zeros_like(out_ref)
    cols = lax.broadcasted_iota(jnp.int32, (n_exp, n_exp * win), 1)
    spread = jnp.where(cols // win == lax.broadcasted_iota(jnp.int32, (n_exp, n_exp * win), 0),
                       1.0, 0.0).astype(BF16)
    slot_in_win = (lax.broadcasted_iota(jnp.int32, (1, n_exp * win), 1) % win + 1).astype(F32)
    exp_lane = lax.broadcasted_iota(jnp.int32, (1, n_exp), 1)

    def body(r, _):
        lo_vec = jnp.zeros((1, n_exp), F32)
        c_vec = jnp.zeros((1, n_exp), F32)
        for e in range(n_exp):
            lo = w[e] + r * win
            c = pl.multiple_of(jnp.minimum(lo, cap - win), SLOT_ALIGN)
            src = pl.multiple_of(e * cap + c, SLOT_ALIGN)
            st_ref[e * win:(e + 1) * win, :] = y_ref[0, pl.ds(src, win), :]
            lo_vec = jnp.where(exp_lane == e, lo.astype(F32), lo_vec)
            c_vec = jnp.where(exp_lane == e, c.astype(F32), c_vec)
        sel = sel_ref[0]
        rel = jnp.where(sel >= lo_vec, sel, -1.0) - c_vec
        rel1 = jnp.where((rel >= 0.0) & (rel < float(win)), rel + 1.0, 0.0)
        hit = _dot(rel1.astype(BF16), spread) == slot_in_win
        out_ref[0] += _dot(jnp.where(hit, 1.0, 0.0).astype(BF16), st_ref[...])
        return 0

    lax.fori_loop(0, rounds, body, 0)


def _moe_combine(starts, y, sel_t, n_tok, cap, win):
    nb, _, d = y.shape
    n_exp = sel_t.shape[2]
    n_tb = n_tok // ROW_BLK
    pw = _largest_tile(d, 1024, LANES)
    return pl.pallas_call(
        functools.partial(_combine_kernel, n_tb=n_tb, cap=cap, win=win, n_exp=n_exp),
        grid_spec=pltpu.PrefetchScalarGridSpec(
            num_scalar_prefetch=1,
            grid=(nb, d // pw, n_tb),
            in_specs=[pl.BlockSpec((1, n_exp * cap, pw), lambda v, p, j, s: (v, 0, p)),
                      pl.BlockSpec((1, ROW_BLK, n_exp), lambda v, p, j, s: (v, j, 0))],
            out_specs=pl.BlockSpec((1, ROW_BLK, pw), lambda v, p, j, s: (v, j, p)),
            scratch_shapes=[pltpu.VMEM((n_exp * win, pw), BF16)],
        ),
        out_shape=jax.ShapeDtypeStruct((nb, n_tok, d), F32),
        compiler_params=_cparams(("parallel", "parallel", "arbitrary"), 48),
        name="moe_combine",
    )(starts, y, sel_t)


def _rope_tables(pos_groups, half):
    freqs = ROPE_BASE ** (-jnp.arange(half, dtype=F32) / half)
    cos, sin = [], []
    for pos in pos_groups:
        ang = pos.astype(F32)[:, None] * freqs[None, :]
        c, s = jnp.cos(ang), jnp.sin(ang)
        cos += [c, c]
        sin += [-s, s]
    return jnp.stack([jnp.concatenate(t, axis=1) for t in (cos, sin)])


def _attn_tables(n_lat, n_ctx):
    s = jnp.arange(n_lat)
    tab = _rope_tables([s // GRID_W, s % GRID_W], HEAD_DIM // 4)
    ident = jnp.stack([jnp.ones((n_ctx, HEAD_DIM), F32), jnp.zeros((n_ctx, HEAD_DIM), F32)])
    return jnp.concatenate([tab, ident], axis=1)


def _ret_tables(n_lat, n_ctx):
    s = jnp.arange(n_lat)
    t = jnp.arange(n_ctx)
    fwd = jnp.concatenate([n_ctx + s, t])
    bwd = jnp.concatenate([n_ctx + (n_lat - 1 - s), n_ctx - 1 - t])
    return jnp.concatenate([_rope_tables([fwd], HEAD_DIM // 2),
                            _rope_tables([bwd], HEAD_DIM // 2)], axis=0)


def kernel(x, c, ctx, c_ctx, w_ada, b_ada, norm_mix, norm_ffn, w_in, q_norm, k_norm, ret_log_decay,
           w_out, w_router, w_gate, w_up, w_down):
    bsz, n_lat, d = x.shape
    n_ctx = ctx.shape[1]
    seq_len = n_lat + n_ctx
    depth = w_ada.shape[0]
    n_exp = w_router.shape[2]
    n_lat_blk = n_lat // ROW_BLK
    assert n_lat % ROW_BLK == 0 and n_ctx == ROW_BLK and bsz + 1 <= 8
    cap_l = CAPACITY_FACTOR * n_lat // n_exp
    cap_c = CAPACITY_FACTOR * n_ctx // n_exp

    cond8 = jnp.zeros((8, d), F32).at[:bsz].set(c).at[bsz].set(c_ctx)
    mods = _adaln(cond8, w_ada, b_ada).reshape(depth, 8, 6, d)

    attn_tab = _attn_tables(n_lat, n_ctx)
    ret_tab = _ret_tables(n_lat, n_ctx)
    wc = _chan_mats()
    two_stage = n_lat % (FFT_COLS * 2 * SUBLANES) == 0
    lat_mats = dict(fft=_fft_mats(n_lat)) if two_stage else dict(dft=_dft_mats(n_lat))
    dft_c = _dft_mats(n_ctx)

    o_q, o_k = 0, ATTN_WIDTH
    o_v = o_k + KV_WIDTH
    o_f = o_v + KV_WIDTH
    o_rq = o_f + FOURIER_WIDTH
    o_rk = o_rq + RET_WIDTH
    o_rv = o_rk + RET_WIDTH
    o_g = o_rv + RET_WIDTH
    o_end = o_g + 2 * RET_WIDTH
    q_scale = HEAD_DIM ** -0.5 * math.log2(math.e)

    sb_c = bsz * cap_c
    win_l = min(cap_l, SLOT_WINDOW)
    win_c = cap_c
    starts_c = jnp.tile(jnp.arange(bsz + 1, dtype=jnp.int32) * cap_c, n_exp)

    xs = jnp.concatenate([x, ctx], axis=1)
    delta_lat = delta_ctx = None
    for l in range(depth):
        xs, h = _norm(xs, mods[l], norm_mix[l], n_lat_blk, delta_lat, delta_ctx,
                      mods[l - 1] if l else None)
        h2d = h.reshape(bsz * seq_len, d)
        wl = w_in[l].astype(BF16)
        q = _proj(h2d, wl[:, o_q:o_k], seq_len, "qk", gain=q_norm[l], tab=attn_tab,
                  n_heads=ATTN_HEADS, scale=q_scale, half=HEAD_DIM // 4)
        k = _proj(h2d, wl[:, o_k:o_v], seq_len, "qk", gain=k_norm[l], tab=attn_tab,
                  n_heads=ATTN_KV_HEADS, scale=1.0, half=HEAD_DIM // 4)
        v, f = _proj(h2d, wl[:, o_v:o_rq], seq_len, "split", split_at=KV_WIDTH)
        rq = _proj(h2d, wl[:, o_rq:o_rk], seq_len, "ret", tab=ret_tab, n_heads=RET_HEADS,
                   scale=1.0, half=HEAD_DIM // 2)
        rk = _proj(h2d, wl[:, o_rk:o_rv], seq_len, "ret", tab=ret_tab, n_heads=RET_HEADS,
                   scale=HEAD_DIM ** -0.5, half=HEAD_DIM // 2)
        rv = _proj(h2d, wl[:, o_rv:o_g], seq_len, "plain")
        sg = _proj(h2d, wl[:, o_g:o_end], seq_len, "silu")

        a = _attention(q, k, v, bsz, seq_len, n_lat)
        fm_lat = _fourier(f, bsz, seq_len, 0, n_lat, wc, **lat_mats)
        fm_ctx = _fourier(f, bsz, seq_len, n_lat, n_ctx, wc, dft=dft_c)
        o_f, o_b = _retention(rq, rk, rv, ret_log_decay[l], bsz, seq_len, n_lat)

        xs, h2, aff_t = _outproj(a, fm_lat, fm_ctx, o_f, o_b, sg, w_out[l].astype(BF16), xs, mods[l],
                                 norm_ffn[l], w_router[l].T, n_lat_blk)

        sel_l, gat_l, sel_c, gat_c, st_l = _routing(aff_t, bsz, seq_len, n_lat, cap_l, cap_c)
        starts_l = st_l[:, :, :n_lat_blk + 1].astype(jnp.int32).reshape(-1)
        merge = lambda t: t.transpose(1, 0, 2).reshape(1, n_exp, bsz * n_ctx)
        tok_major = lambda t: t.transpose(0, 2, 1)
        sel_c, gat_c = merge(sel_c), merge(gat_c)
        h_ctx = h2[:, n_lat:, :].reshape(1, bsz * n_ctx, d)

        xg_l, gate_l = _moe_gather(starts_l, h2, sel_l, gat_l, n_lat, cap_l, win_l)
        xg_c, gate_c = _moe_gather(starts_c, h_ctx, sel_c, gat_c, bsz * n_ctx, sb_c, win_c)
        act_l, act_c = _moe_ffn(xg_l, xg_c, w_gate, w_up, l)
        y_l, y_c = _moe_down(act_l, act_c, gate_l, gate_c, w_down, l, cap_l)
        delta_lat = _moe_combine(starts_l, y_l, tok_major(sel_l), n_lat, cap_l, win_l)
        delta_ctx = _moe_combine(starts_c, y_c, tok_major(sel_c), bsz * n_ctx, sb_c,
                                 win_c).reshape(bsz * n_ctx, d)

    out, _ = _norm(xs, None, None, n_lat_blk, delta_lat, delta_ctx, mods[depth - 1],
                   want_h=False, lat_only=True)
    return out
```

```python
import functools
import math

import jax
import jax.numpy as jnp
from jax import lax
from jax.experimental import pallas as pl
from jax.experimental.pallas import tpu as pltpu

HEAD_DIM = 128
ATTN_HEADS = 8
ATTN_KV_HEADS = 2
ATTN_GROUP = ATTN_HEADS // ATTN_KV_HEADS
FOURIER_GROUPS = 4
RET_HEADS = 4
RET_CHUNK = 128
GRID_W = 64
ROPE_BASE = 10000.0
EPS = 1e-6
CAPACITY_FACTOR = 2

ATTN_WIDTH = ATTN_HEADS * HEAD_DIM
KV_WIDTH = ATTN_KV_HEADS * HEAD_DIM
FOURIER_WIDTH = FOURIER_GROUPS * HEAD_DIM
RET_WIDTH = RET_HEADS * HEAD_DIM

ROW_BLK = 256
LANES = 128
MIB = 1024 * 1024

F32 = jnp.float32
BF16 = jnp.bfloat16


def _cparams(sem, vmem_mib):
    return pltpu.CompilerParams(dimension_semantics=sem, vmem_limit_bytes=vmem_mib * MIB)


def _largest_tile(n, cap, mult=8):
    best = mult
    for t in range(mult, min(n, cap) + 1, mult):
        if n % t == 0:
            best = t
    return best


def _split(a):
    hi = a.astype(BF16)
    lo = (a - hi.astype(F32)).astype(BF16)
    return hi, lo


def _dot(a, b):
    return jnp.dot(a, b, preferred_element_type=F32)


def _dot_nt(a, b):
    return lax.dot_general(a, b, (((1,), (1,)), ((), ())), preferred_element_type=F32)


def _dot_tn(a, b):
    return lax.dot_general(a, b, (((0,), (0,)), ((), ())), preferred_element_type=F32)


def _silu(a):
    return a / (1.0 + jnp.exp(-a))


def _adaln_kernel(c_ref, w_ref, b_ref, o_ref):
    s = _silu(c_ref[...])
    sh, sl = _split(s)
    wh, wl = _split(w_ref[0])
    o_ref[0] = _dot(sh, wh) + (_dot(sh, wl) + _dot(sl, wh)) + b_ref[0]


def _adaln(cond8, w_ada, b_ada):
    depth, d, n = w_ada.shape
    tn = _largest_tile(n, 768, LANES)
    return pl.pallas_call(
        _adaln_kernel,
        grid=(depth, n // tn),
        in_specs=[
            pl.BlockSpec((8, d), lambda l, j: (0, 0)),
            pl.BlockSpec((1, d, tn), lambda l, j: (l, 0, j)),
            pl.BlockSpec((1, 1, tn), lambda l, j: (l, 0, j)),
        ],
        out_specs=pl.BlockSpec((1, 8, tn), lambda l, j: (l, 0, j)),
        out_shape=jax.ShapeDtypeStruct((depth, 8, n), F32),
        compiler_params=_cparams(("parallel", "parallel"), 48),
        name="adaln",
    )(cond8, w_ada, b_ada.reshape(depth, 1, n))


def _rms_mod(x, g, shift, scale):
    y = x * lax.rsqrt(jnp.mean(x * x, axis=-1, keepdims=True) + EPS)
    return (y * g) * (1.0 + scale) + shift


def _norm_kernel(*refs, has_delta, want_h, n_lat_blk):
    refs = list(refs)
    x_ref = refs.pop(0)
    x = x_ref[0]
    if has_delta:
        dl_ref, dc_ref, pm_ref = refs.pop(0), refs.pop(0), refs.pop(0)
        is_ctx = pl.program_id(1) >= n_lat_blk
        delta = jnp.where(is_ctx, dc_ref[...], dl_ref[0])
        x = x + pm_ref[0, 5:6, :] * delta
    if want_h:
        m_ref, g_ref = refs.pop(0), refs.pop(0)
    if has_delta:
        xo_ref = refs.pop(0)
        xo_ref[0] = x
    if want_h:
        h_ref = refs.pop(0)
        h_ref[0] = _rms_mod(x, g_ref[...], m_ref[0, 0:1, :], m_ref[0, 1:2, :]).astype(BF16)


def _norm(x, mods, g, n_lat_blk, delta_lat=None, delta_ctx=None, prev_mods=None,
          want_h=True, lat_only=False):
    bsz, l, d = x.shape
    n_blk = n_lat_blk if lat_only else l // ROW_BLK
    has_delta = delta_lat is not None

    def mod_row(b, t):
        return (jnp.where(t >= n_lat_blk, bsz, b), 0, 0)

    xspec = pl.BlockSpec((1, ROW_BLK, d), lambda b, t: (b, t, 0))
    in_specs, args = [xspec], [x]
    if has_delta:
        in_specs += [
            pl.BlockSpec((1, ROW_BLK, d), lambda b, t: (b, jnp.minimum(t, n_lat_blk - 1), 0)),
            pl.BlockSpec((ROW_BLK, d), lambda b, t: (b, 0)),
            pl.BlockSpec((1, 6, d), mod_row),
        ]
        args += [delta_lat, delta_ctx, prev_mods]
    if want_h:
        in_specs += [pl.BlockSpec((1, 6, d), mod_row), pl.BlockSpec((1, d), lambda b, t: (0, 0))]
        args += [mods, g.reshape(1, d)]
    out_specs, out_shape = [], []
    rows = n_blk * ROW_BLK
    if has_delta:
        out_specs.append(xspec)
        out_shape.append(jax.ShapeDtypeStruct((bsz, rows, d), F32))
    if want_h:
        out_specs.append(xspec)
        out_shape.append(jax.ShapeDtypeStruct((bsz, rows, d), BF16))
    outs = pl.pallas_call(
        functools.partial(_norm_kernel, has_delta=has_delta, want_h=want_h, n_lat_blk=n_lat_blk),
        grid=(bsz, n_blk),
        in_specs=in_specs,
        out_specs=out_specs,
        out_shape=out_shape,
        compiler_params=_cparams(("parallel", "parallel"), 32),
        name="norm",
    )(*args)
    outs = list(outs)
    x_new = outs.pop(0) if has_delta else x
    h = outs.pop(0) if want_h else None
    return x_new, h


def _dot_wide(a, m):
    hi, lo = _split(a)
    return _dot(hi, m) + _dot(lo, m)


def _partner_matrix(half):
    src = lax.broadcasted_iota(jnp.int32, (LANES, LANES), 0)
    dst = lax.broadcasted_iota(jnp.int32, (LANES, LANES), 1)
    mate = jnp.where(dst % (2 * half) < half, dst + half, dst - half)
    return jnp.where(src == mate, 1.0, 0.0).astype(BF16)


def _proj_kernel(*refs, flavor, n_heads, scale, half, split_at):
    h_ref, w_ref = refs[0], refs[1]
    acc = _dot(h_ref[...], w_ref[...])
    if flavor == "plain":
        refs[2][...] = acc.astype(BF16)
    elif flavor == "split":
        refs[2][...] = acc[:, :split_at].astype(BF16)
        refs[3][...] = acc[:, split_at:].astype(BF16)
    elif flavor == "silu":
        refs[2][...] = _silu(acc).astype(BF16)
    elif flavor == "qk":
        gain_ref, tab_ref, o_ref = refs[2], refs[3], refs[4]
        partner = _partner_matrix(half)
        ones = jnp.ones((LANES, LANES), BF16)
        for hd in range(n_heads):
            sl = slice(hd * HEAD_DIM, (hd + 1) * HEAD_DIM)
            z = acc[:, sl]
            ms = _dot_wide(z * z, ones) * (1.0 / HEAD_DIM)
            z = (z * lax.rsqrt(ms + EPS)) * gain_ref[...]
            z = z * tab_ref[0] + _dot_wide(z, partner) * tab_ref[1]
            o_ref[:, sl] = (z * scale).astype(BF16)
    elif flavor == "ret":
        tab_ref, o_ref = refs[2], refs[3]
        partner = _partner_matrix(half)
        width = n_heads * HEAD_DIM
        for hd in range(n_heads):
            sl = slice(hd * HEAD_DIM, (hd + 1) * HEAD_DIM)
            z = acc[:, sl] * scale
            pz = _dot_wide(z, partner)
            o_ref[:, sl] = (z * tab_ref[0] + pz * tab_ref[1]).astype(BF16)
            o_ref[:, width + hd * HEAD_DIM: width + (hd + 1) * HEAD_DIM] = (
                z * tab_ref[2] + pz * tab_ref[3]).astype(BF16)
    else:
        raise ValueError(flavor)


def _proj(h2d, w, seq_len, flavor, *, gain=None, tab=None, n_heads=0, scale=1.0, half=0,
          split_at=0):
    t_rows, d = h2d.shape
    n = w.shape[1]
    tm = _largest_tile(seq_len, 1088)
    per_seq = seq_len // tm
    in_specs = [pl.BlockSpec((tm, d), lambda i: (i, 0)), pl.BlockSpec((d, n), lambda i: (0, 0))]
    args = [h2d, w]
    if flavor == "qk":
        in_specs.append(pl.BlockSpec((1, HEAD_DIM), lambda i: (0, 0)))
        args.append(gain.reshape(1, HEAD_DIM))
    if flavor in ("qk", "ret"):
        ntab = tab.shape[0]
        in_specs.append(pl.BlockSpec((ntab, tm, HEAD_DIM), lambda i: (0, i % per_seq, 0)))
        args.append(tab)
    if flavor == "split":
        widths = [split_at, n - split_at]
    elif flavor == "ret":
        widths = [2 * n]
    else:
        widths = [n]
    out_specs = [pl.BlockSpec((tm, wd), lambda i: (i, 0)) for wd in widths]
    out_shape = [jax.ShapeDtypeStruct((t_rows, wd), BF16) for wd in widths]
    outs = pl.pallas_call(
        functools.partial(_proj_kernel, flavor=flavor, n_heads=n_heads, scale=scale, half=half,
                          split_at=split_at),
        grid=(t_rows // tm,),
        in_specs=in_specs,
        out_specs=out_specs,
        out_shape=out_shape,
        compiler_params=_cparams(("parallel",), 48),
        name="proj_" + flavor,
    )(*args)
    return outs if len(outs) > 1 else outs[0]


SUBLANES = 8


SAFE_SHIFT = 50.0


def _lane_fold(t, op):
    return functools.reduce(op, [t[:, i * LANES:(i + 1) * LANES]
                                 for i in range(t.shape[1] // LANES)])


def _fold_rows(t, op):
    return functools.reduce(op, [t[r:r + SUBLANES, :] for r in range(0, t.shape[0], SUBLANES)])


def _attn_group_bounded(q_all, m_row, k_ref, vt_ref, pt_ref, k_lo, n_keys, tk):
    lp = None
    for c in range(n_keys // tk):
        st = _dot_nt(k_ref[k_lo + c * tk:k_lo + (c + 1) * tk, :], q_all)
        p = jnp.exp2(st - m_row)
        part = _fold_rows(p, jnp.add)
        lp = part if lp is None else lp + part
        pt_ref[c * tk:(c + 1) * tk, :] = p.astype(BF16)
    l = jnp.sum(lp, axis=0, keepdims=True)
    return _dot(vt_ref[:, k_lo:k_lo + n_keys], pt_ref[:n_keys, :]) / l


def _attn_head_exact(q, k_ref, v_ref, s_ref, p_ref, k_lo, n_keys, tk):
    s_ref[:, :n_keys] = _dot_nt(q, k_ref[k_lo:k_lo + n_keys, :])
    n_chunks = n_keys // tk
    mp = None
    for c in range(n_chunks):
        part = _lane_fold(s_ref[:, c * tk:(c + 1) * tk], jnp.maximum)
        mp = part if mp is None else jnp.maximum(mp, part)
    m = jnp.max(mp, axis=-1, keepdims=True)
    lp = None
    for c in range(n_chunks):
        p = jnp.exp2(s_ref[:, c * tk:(c + 1) * tk] - m)
        part = _lane_fold(p, jnp.add)
        lp = part if lp is None else lp + part
        p_ref[:, c * tk:(c + 1) * tk] = p.astype(BF16)
    l = jnp.sum(lp, axis=-1, keepdims=True)
    return _dot(p_ref[:, :n_keys], v_ref[k_lo:k_lo + n_keys, :]) / l


def _attn_kernel(q_ref, k_ref, v_ref, o_ref, s_ref, p_ref, pt_ref, vt_ref, kmax_ref, *, tk, n_lat,
                 n_ctx, n_lat_qblk):
    i = pl.program_id(2)
    is_ctx = i >= n_lat_qblk

    @pl.when(i == 0)
    def _():
        kk = k_ref[...].astype(F32)
        kn2 = jnp.max(jnp.sum(kk * kk, axis=-1, keepdims=True), axis=0, keepdims=True)
        kmax_ref[...] = jnp.broadcast_to(jnp.sqrt(kn2), kmax_ref.shape)
        vt_ref[...] = v_ref[...].astype(F32).T.astype(BF16)

    heads = [slice(hd * HEAD_DIM, (hd + 1) * HEAD_DIM) for hd in range(ATTN_GROUP)]
    tq = q_ref.shape[0]
    q_all = jnp.concatenate([q_ref[:, sl] for sl in heads], axis=0)
    qf = q_all.astype(F32)
    qn2 = _dot_nt(jnp.ones((SUBLANES, HEAD_DIM), BF16), (qf * qf).astype(BF16))[0:1, :]
    m_row = jnp.sqrt(qn2) * kmax_ref[0:1, 0:1] * (1.0 + 2.0 ** -6)
    worst = jnp.max(m_row)
    bounded_ok = worst <= SAFE_SHIFT

    def run(k_lo, n_keys, bounded):
        if bounded:
            ot = _attn_group_bounded(q_all, m_row, k_ref, vt_ref, pt_ref, k_lo, n_keys, tk)
            outs = [ot[:, hd * tq:(hd + 1) * tq].T for hd in range(len(heads))]
        else:
            outs = [_attn_head_exact(q_ref[:, sl], k_ref, v_ref, s_ref.at[hd], p_ref.at[hd], k_lo,
                                     n_keys, tk) for hd, sl in enumerate(heads)]
        for o, sl in zip(outs, heads):
            o_ref[:, sl] = o.astype(BF16)

    for ctx_case, (k_lo, n_keys) in ((False, (0, n_lat + n_ctx)), (True, (n_lat, n_ctx))):
        for bounded in (True, False):
            @pl.when((is_ctx == ctx_case) & (bounded_ok == bounded))
            def _(k_lo=k_lo, n_keys=n_keys, bounded=bounded):
                run(k_lo, n_keys, bounded)


def _attention(q, k, v, bsz, seq_len, n_lat):
    tq = ROW_BLK
    tk = ROW_BLK
    per_seq = seq_len // tq
    gw = ATTN_GROUP * HEAD_DIM
    return pl.pallas_call(
        functools.partial(_attn_kernel, tk=tk, n_lat=n_lat, n_ctx=seq_len - n_lat,
                          n_lat_qblk=n_lat // tq),
        grid=(bsz, ATTN_KV_HEADS, per_seq),
        in_specs=[
            pl.BlockSpec((tq, gw), lambda b, g, i: (b * per_seq + i, g)),
            pl.BlockSpec((seq_len, HEAD_DIM), lambda b, g, i: (b, g)),
            pl.BlockSpec((seq_len, HEAD_DIM), lambda b, g, i: (b, g)),
        ],
        out_specs=pl.BlockSpec((tq, gw), lambda b, g, i: (b * per_seq + i, g)),
        out_shape=jax.ShapeDtypeStruct(q.shape, BF16),
        scratch_shapes=[pltpu.VMEM((ATTN_GROUP, tq, seq_len), F32),
                        pltpu.VMEM((ATTN_GROUP, tq, seq_len), BF16),
                        pltpu.VMEM((seq_len, ATTN_GROUP * tq), BF16),
                        pltpu.VMEM((HEAD_DIM, seq_len), BF16),
                        pltpu.VMEM((SUBLANES, LANES), F32)],
        compiler_params=_cparams(("parallel", "parallel", "arbitrary"), 58),
        name="attention",
    )(q, k, v)


def _fourier_a_kernel(f_ref, wc_ref, o_ref):
    g = _dot(f_ref[...], wc_ref[...])
    wdt = f_ref.shape[1]
    o_ref[0, 0] = g[:, :wdt].astype(o_ref.dtype)
    o_ref[0, 1] = g[:, wdt:].astype(o_ref.dtype)


def _fourier_b_kernel(m_ref, g_ref, o_ref):
    o_ref[0] = _dot(m_ref[...], g_ref[0]).astype(BF16)


FFT_COLS = 64
FFT_PER_STEP = SUBLANES


def _fft1_kernel(g_ref, m_ref, tc_ref, ts_ref, o_ref):
    width = o_ref.shape[4]
    n1 = g_ref.shape[2]
    for s in range(g_ref.shape[3]):
        g = jnp.concatenate([g_ref[0, 0, :, s, :], g_ref[0, 1, :, s, :]], axis=0).astype(BF16)
        a = _dot(m_ref[...], g)
        ar, ai = a[:n1], a[n1:]
        tc = tc_ref[0, :, s * LANES:(s + 1) * LANES]
        ts = ts_ref[0, :, s * LANES:(s + 1) * LANES]
        lanes = [slice(q * LANES, (q + 1) * LANES) for q in range(width // LANES)]
        o_ref[0, 0, :, s, :] = jnp.concatenate([ar[:, q] * tc + ai[:, q] * ts for q in lanes], axis=1)
        o_ref[0, 1, :, s, :] = jnp.concatenate([ai[:, q] * tc - ar[:, q] * ts for q in lanes], axis=1)


def _fft2_kernel(b_ref, m_ref, o_ref):
    for s in range(b_ref.shape[2]):
        b = jnp.concatenate([b_ref[0, 0, s], b_ref[0, 1, s]], axis=0).astype(BF16)
        o_ref[0, :, s, :] = _dot(m_ref[...], b)


def _fourier_two_stage(g, fft):
    m1, tc, ts, m3 = fft
    bsz, _, n, width = g.shape
    n2 = m3.shape[0]
    n1 = n // n2
    per = FFT_PER_STEP
    b = pl.pallas_call(
        _fft1_kernel,
        grid=(bsz, n2 // per),
        in_specs=[
            pl.BlockSpec((1, 2, n1, per, width), lambda b, k: (b, 0, 0, k, 0)),
            pl.BlockSpec(m1.shape, lambda b, k: (0, 0)),
            pl.BlockSpec((1, n1, per * LANES), lambda b, k: (k, 0, 0)),
            pl.BlockSpec((1, n1, per * LANES), lambda b, k: (k, 0, 0)),
        ],
        out_specs=pl.BlockSpec((1, 2, n1, per, width), lambda b, k: (b, 0, 0, k, 0)),
        out_shape=jax.ShapeDtypeStruct((bsz, 2, n1, n2, width), F32),
        compiler_params=_cparams(("parallel", "parallel"), 32),
        name="fourier_fft1",
    )(g.reshape(bsz, 2, n1, n2, width), m1, tc, ts)
    out = pl.pallas_call(
        _fft2_kernel,
        grid=(bsz, n1 // per),
        in_specs=[
            pl.BlockSpec((1, 2, per, n2, width), lambda b, k: (b, 0, k, 0, 0)),
            pl.BlockSpec(m3.shape, lambda b, k: (0, 0)),
        ],
        out_specs=pl.BlockSpec((1, n2, per, width), lambda b, k: (b, 0, k, 0)),
        out_shape=jax.ShapeDtypeStruct((bsz, n2, n1, width), F32),
        compiler_params=_cparams(("parallel", "parallel"), 32),
        name="fourier_fft2",
    )(b, m3)
    return out.reshape(bsz, n, width)


def _fft_mats(n):
    n2 = FFT_COLS
    n1 = n // n2
    per = FFT_PER_STEP

    def cs(rows, cols, period):
        ang = ((rows[:, None] * cols[None, :]) % period).astype(F32) * (2.0 * math.pi / period)
        return jnp.cos(ang), jnp.sin(ang)

    i1, i2 = jnp.arange(n1, dtype=jnp.int32), jnp.arange(n2, dtype=jnp.int32)
    c1, s1 = cs(i1, i1, n1)
    m1 = (jnp.block([[c1, s1], [-s1, c1]]) * n1 ** -0.5).astype(BF16)
    c3, s3 = cs(i2, i2, n2)
    m3 = (jnp.concatenate([c3, s3], axis=1) * n2 ** -0.5).astype(BF16)
    tc, ts = cs(i1, i2, n)
    expand = lambda t: jnp.repeat(t.T.reshape(n2 // per, per, n1).transpose(0, 2, 1), LANES,
                                  axis=2)
    return m1, expand(tc), expand(ts), m3


def _fourier(f2d, bsz, seq_len, row_off, n, wc, dft=None, fft=None):
    per_seq = seq_len // ROW_BLK
    off_blk = row_off // ROW_BLK
    nb = n // ROW_BLK
    width = f2d.shape[1]
    g = pl.pallas_call(
        _fourier_a_kernel,
        grid=(bsz, nb),
        in_specs=[
            pl.BlockSpec((ROW_BLK, width), lambda b, t: (b * per_seq + off_blk + t, 0)),
            pl.BlockSpec((width, 2 * width), lambda b, t: (0, 0)),
        ],
        out_specs=pl.BlockSpec((1, 2, ROW_BLK, width), lambda b, t: (b, 0, t, 0)),
        out_shape=jax.ShapeDtypeStruct((bsz, 2, n, width), BF16 if fft is None else F32),
        compiler_params=_cparams(("parallel", "parallel"), 32),
        name="fourier_chan",
    )(f2d, wc)
    if fft is not None:
        return _fourier_two_stage(g, fft)
    g = g.reshape(bsz, 2 * n, width)
    return pl.pallas_call(
        _fourier_b_kernel,
        grid=(nb, bsz),
        in_specs=[
            pl.BlockSpec((ROW_BLK, 2 * n), lambda i, b: (i, 0)),
            pl.BlockSpec((1, 2 * n, width), lambda i, b: (b, 0, 0)),
        ],
        out_specs=pl.BlockSpec((1, ROW_BLK, width), lambda i, b: (b, i, 0)),
        out_shape=jax.ShapeDtypeStruct((bsz, n, width), BF16),
        compiler_params=_cparams(("parallel", "parallel"), 48),
        name="fourier_pos",
    )(dft, g)


def _dft_mats(n):
    i = jnp.arange(n, dtype=jnp.int32)
    prod = (i[:, None] * i[None, :]) % n
    ang = prod.astype(F32) * (2.0 * math.pi / n)
    s = n ** -0.5
    return jnp.concatenate([jnp.cos(ang) * s, jnp.sin(ang) * s], axis=1).astype(BF16)


def _chan_mats():
    i = jnp.arange(HEAD_DIM, dtype=jnp.int32)
    ang = ((i[:, None] * i[None, :]) % HEAD_DIM).astype(F32) * (2.0 * math.pi / HEAD_DIM)
    s = HEAD_DIM ** -0.5
    eye = jnp.eye(FOURIER_GROUPS, dtype=F32)
    c = jnp.kron(eye, jnp.cos(ang) * s)
    sn = jnp.kron(eye, -jnp.sin(ang) * s)
    return jnp.concatenate([c, sn], axis=1).astype(BF16)


RET_HEADS_PER_STEP = 2


def _ret_kernel(ld_ref, qf_ref, qb_ref, kf_ref, kb_ref, v_ref, of_ref, ob_ref, state_ref,
                *, n_lat_chunks, n_ctx_chunks):
    c = RET_CHUNK
    n_all = n_lat_chunks + n_ctx_chunks
    h0 = pl.program_id(1) * RET_HEADS_PER_STEP
    ii = lax.broadcasted_iota(jnp.int32, (c, c), 0).astype(F32)
    jj = lax.broadcasted_iota(jnp.int32, (c, c), 1).astype(F32)
    ri = lax.broadcasted_iota(jnp.int32, (c, 1), 0).astype(F32)
    state_ref[...] = jnp.zeros_like(state_ref)

    chains = []
    for hh in range(RET_HEADS_PER_STEP):
        sl = slice(hh * HEAD_DIM, (hh + 1) * HEAD_DIM)
        for d, (q_ref, k_ref, o_ref) in enumerate(((qf_ref, kf_ref, of_ref),
                                                   (qb_ref, kb_ref, ob_ref))):
            lg = ld_ref[d, h0 + hh]
            diff = ii - jj if d == 0 else jj - ii
            intra = jnp.where(diff >= 0, jnp.exp(lg * jnp.maximum(diff, 0.0)), 0.0)
            q_dec = jnp.exp(lg * (ri + 1.0 if d == 0 else c - ri))
            k_dec = jnp.exp(lg * (c - 1.0 - ri if d == 0 else ri))
            c_dec = jnp.exp(jnp.full((1, HEAD_DIM), lg * c, F32))
            chains.append((len(chains), q_ref, k_ref, o_ref, sl, (intra, q_dec, k_dec, c_dec), d))

    def body(s, _):
        in_ctx = s < n_ctx_chunks
        for slot, q_ref, k_ref, o_ref, sl, (intra, q_dec, k_dec, c_dec), d in chains:
            if d == 0:
                chunk = jnp.where(in_ctx, n_lat_chunks + s, s - n_ctx_chunks)
            else:
                chunk = n_all - 1 - s
            off = pl.multiple_of(chunk * c, c)
            q = q_ref[pl.ds(off, c), sl]
            k = k_ref[pl.ds(off, c), sl]
            v = v_ref[pl.ds(off, c), sl]
            st = state_ref[slot]
            sc = _dot_nt(q, k) * intra
            o = _dot(sc.astype(BF16), v) + _dot(q, st.astype(BF16)) * q_dec
            kd = (k.astype(F32) * k_dec).astype(BF16)
            state_ref[slot] = st * c_dec + _dot_tn(kd, v)
            mu = jnp.mean(o, axis=-1, keepdims=True)
            var = jnp.mean(jnp.square(o - mu), axis=-1, keepdims=True)
            o_ref[pl.ds(off, c), sl] = (o - mu) * lax.rsqrt(var + EPS)
        return 0

    lax.fori_loop(0, n_all, body, 0)


def _retention(rq, rk, rv, log_decay, bsz, seq_len, n_lat):
    hw = RET_HEADS_PER_STEP * HEAD_DIM
    n_hb = RET_HEADS // RET_HEADS_PER_STEP
    fwd = pl.BlockSpec((seq_len, hw), lambda b, h: (b, h))
    bwd = pl.BlockSpec((seq_len, hw), lambda b, h: (b, n_hb + h))
    return pl.pallas_call(
        functools.partial(_ret_kernel, n_lat_chunks=n_lat // RET_CHUNK,
                          n_ctx_chunks=(seq_len - n_lat) // RET_CHUNK),
        grid=(bsz, n_hb),
        in_specs=[pl.BlockSpec(memory_space=pltpu.SMEM), fwd, bwd, fwd, bwd, fwd],
        out_specs=[fwd, fwd],
        out_shape=[jax.ShapeDtypeStruct((bsz * seq_len, RET_WIDTH), F32)] * 2,
        scratch_shapes=[pltpu.VMEM((2 * RET_HEADS_PER_STEP, HEAD_DIM, HEAD_DIM), F32)],
        compiler_params=_cparams(("parallel", "parallel"), 56),
        name="retention",
    )(log_decay, rq, rq, rk, rk, rv)


def _outproj_kernel(a_ref, fl_ref, fc_ref, of_ref, ob_ref, sg_ref, w_ref, x_ref, m_ref, g_ref, wr_ref,
                    xo_ref, h_ref, aff_ref, *, n_lat_blk):
    is_ctx = pl.program_id(1) >= n_lat_blk
    fm = jnp.where(is_ctx, fc_ref[0], fl_ref[0].astype(BF16))
    rw = RET_WIDTH
    r = (sg_ref[:, :rw].astype(F32) * of_ref[...]
         + sg_ref[:, rw:].astype(F32) * ob_ref[...]).astype(BF16)
    a0, f0 = ATTN_WIDTH, ATTN_WIDTH + FOURIER_WIDTH
    y = (_dot(a_ref[...], w_ref[:a0, :]) + _dot(fm, w_ref[a0:f0, :])) + _dot(r, w_ref[f0:, :])
    x = x_ref[0] + m_ref[0, 2:3, :] * y
    xo_ref[0] = x
    h = _rms_mod(x, g_ref[...], m_ref[0, 3:4, :], m_ref[0, 4:5, :])
    h_ref[0] = h.astype(BF16)
    n_exp = aff_ref.shape[0]
    hh, hl = _split(h)
    wh, wl = _split(wr_ref[...])
    lg = _dot(hh, wh) + (_dot(hh, wl) + _dot(hl, wh))
    lg = jnp.where(lax.broadcasted_iota(jnp.int32, lg.shape, 1) < n_exp, lg, -jnp.inf)
    e = jnp.exp(lg - jnp.max(lg, axis=-1, keepdims=True))
    aff = e / jnp.sum(e, axis=-1, keepdims=True)
    aff_ref[...] = aff.T[:n_exp, :]


def _outproj(a, fm_lat, fm_ctx, o_f, o_b, sg, w_out, x, mods, g, w_router, n_lat_blk):
    bsz, l, d = x.shape
    per_seq = l // ROW_BLK
    n_exp = w_router.shape[1]
    wr = jnp.pad(w_router, ((0, 0), (0, LANES - n_exp)))
    fw = fm_lat.shape[-1]

    def flat(b, t):
        return (b * per_seq + t, 0)

    return pl.pallas_call(
        functools.partial(_outproj_kernel, n_lat_blk=n_lat_blk),
        grid=(bsz, per_seq),
        in_specs=[
            pl.BlockSpec((ROW_BLK, a.shape[1]), flat),
            pl.BlockSpec((1, ROW_BLK, fw), lambda b, t: (b, jnp.minimum(t, n_lat_blk - 1), 0)),
            pl.BlockSpec((1, ROW_BLK, fw), lambda b, t: (b, 0, 0)),
            pl.BlockSpec((ROW_BLK, o_f.shape[1]), flat),
            pl.BlockSpec((ROW_BLK, o_b.shape[1]), flat),
            pl.BlockSpec((ROW_BLK, sg.shape[1]), flat),
            pl.BlockSpec(w_out.shape, lambda b, t: (0, 0)),
            pl.BlockSpec((1, ROW_BLK, d), lambda b, t: (b, t, 0)),
            pl.BlockSpec((1, 6, d), lambda b, t: (jnp.where(t >= n_lat_blk, bsz, b), 0, 0)),
            pl.BlockSpec((1, d), lambda b, t: (0, 0)),
            pl.BlockSpec(wr.shape, lambda b, t: (0, 0)),
        ],
        out_specs=[
            pl.BlockSpec((1, ROW_BLK, d), lambda b, t: (b, t, 0)),
            pl.BlockSpec((1, ROW_BLK, d), lambda b, t: (b, t, 0)),
            pl.BlockSpec((n_exp, ROW_BLK), lambda b, t: (0, b * per_seq + t)),
        ],
        out_shape=[
            jax.ShapeDtypeStruct((bsz, l, d), F32),
            jax.ShapeDtypeStruct((bsz, l, d), BF16),
            jax.ShapeDtypeStruct((n_exp, bsz * l), F32),
        ],
        compiler_params=_cparams(("parallel", "parallel"), 48),
        name="outproj",
    )(a, fm_lat, fm_ctx, o_f, o_b, sg, w_out, x, mods, g.reshape(1, d), wr)


def _cumsum_lanes(m, out_ref, fin):
    n_exp, n = m.shape
    tri = (lax.broadcasted_iota(jnp.int32, (LANES, LANES), 0)
           <= lax.broadcasted_iota(jnp.int32, (LANES, LANES), 1)).astype(BF16)
    run = jnp.zeros((n_exp, 1), F32)
    befores = []
    for k in range(n // LANES):
        befores.append(run)
        sl = slice(k * LANES, (k + 1) * LANES)
        cnt = _dot(m[:, sl].astype(BF16), tri) + run
        out_ref[0, :, sl] = fin(cnt, sl)
        run = cnt[:, LANES - 1:LANES]
    befores.append(run)
    return befores


def _select(seg, cap, slot_off, sel_ref, tmp_ref):
    bits = pltpu.bitcast(seg, jnp.int32)
    n_exp = seg.shape[0]

    def body(it, t):
        tt = t | lax.shift_left(jnp.int32(1), 30 - it)
        cnt = jnp.sum(jnp.where(bits >= tt, 1.0, 0.0), axis=1, keepdims=True)
        return jnp.where(cnt >= cap, tt, t)

    t = lax.fori_loop(0, 31, body, jnp.zeros((n_exp, 1), jnp.int32))
    gt = bits > t
    eq = bits == t
    need = cap - jnp.sum(jnp.where(gt, 1.0, 0.0), axis=1, keepdims=True)
    eqf = jnp.where(eq, 1.0, 0.0)
    _cumsum_lanes(eqf, tmp_ref, lambda cnt, sl: cnt)
    take = eq & (tmp_ref[0] - eqf < need)
    mask = gt | take
    maskf = jnp.where(mask, 1.0, 0.0)
    return _cumsum_lanes(
        maskf, sel_ref,
        lambda cnt, sl: jnp.where(maskf[:, sl] > 0.5, cnt - 1.0 + slot_off, -1.0))


def _routing_kernel(aff_ref, sel_l, gat_l, sel_c, gat_c, st_l, tmp_l, tmp_c, *, n_lat, cap_l, cap_c):
    b = pl.program_id(0)
    a = aff_ref[...]
    lat = a[:, :n_lat]
    ctx = a[:, n_lat:]
    gat_l[0] = lat
    gat_c[0] = ctx
    befores = _select(lat, float(cap_l), 0.0, sel_l, tmp_l)
    _select(ctx, float(cap_c), (b * cap_c).astype(F32), sel_c, tmp_c)
    lane = lax.broadcasted_iota(jnp.int32, (a.shape[0], LANES), 1)
    st = jnp.zeros((a.shape[0], LANES), F32)
    per_blk = ROW_BLK // LANES
    for j in range(n_lat // ROW_BLK + 1):
        st = jnp.where(lane == j, befores[j * per_blk], st)
    st_l[0] = st


def _routing(aff_t, bsz, seq_len, n_lat, cap_l, cap_c):
    n_exp = aff_t.shape[0]
    n_ctx = seq_len - n_lat
    shp = lambda n: jax.ShapeDtypeStruct((bsz, n_exp, n), F32)
    spec = lambda n: pl.BlockSpec((1, n_exp, n), lambda b: (b, 0, 0))
    return pl.pallas_call(
        functools.partial(_routing_kernel, n_lat=n_lat, cap_l=cap_l, cap_c=cap_c),
        grid=(bsz,),
        in_specs=[pl.BlockSpec((n_exp, seq_len), lambda b: (0, b))],
        out_specs=[spec(n_lat), spec(n_lat), spec(n_ctx), spec(n_ctx), spec(LANES)],
        out_shape=[shp(n_lat), shp(n_lat), shp(n_ctx), shp(n_ctx), shp(LANES)],
        scratch_shapes=[pltpu.VMEM((1, n_exp, n_lat), F32), pltpu.VMEM((1, n_exp, n_ctx), F32)],
        compiler_params=_cparams(("parallel",), 32),
        name="routing",
    )(aff_t)


EXPERT_GROUP = 8
SLOT_ALIGN = 16
SLOT_WINDOW = 64


def _windows(starts_ref, base, stride, n, j, cap, win):
    w, rounds = [], jnp.int32(0)
    for k in range(n):
        s0 = starts_ref[base + k * stride + j]
        s1 = starts_ref[base + k * stride + j + 1]
        wk = jnp.minimum((s0 // SLOT_ALIGN) * SLOT_ALIGN, cap - win)
        w.append(wk)
        rounds = jnp.maximum(rounds, (s1 - wk + win - 1) // win)
    return w, rounds


def _gather_kernel(starts_ref, h_ref, sel_ref, aff_ref, xg_ref, gate_ref, *, n_tb, cap, win, n_exp):
    v, g, j = pl.program_id(0), pl.program_id(1), pl.program_id(2)
    ng = sel_ref.shape[2]
    base = (v * n_exp + g * ng) * (n_tb + 1)

    @pl.when(j == 0)
    def _():
        xg_ref[...] = jnp.zeros_like(xg_ref)
        gate_ref[...] = jnp.zeros_like(gate_ref)

    w, rounds = _windows(starts_ref, base, n_tb + 1, ng, j, cap, win)
    tb = h_ref.shape[1]
    row_i = lax.broadcasted_iota(jnp.int32, (win, tb), 0).astype(F32)

    def body(r, _):
        starts, pieces, gates = [], [], []
        for k in range(ng):
            lo = w[k] + r * win
            c = pl.multiple_of(jnp.minimum(lo, cap - win), SLOT_ALIGN)
            sel = sel_ref[0, 0, k:k + 1, :]
            rel = jnp.where(sel >= lo.astype(F32), sel, -1.0) - c.astype(F32)
            hit = rel == row_i
            pieces.append(jnp.where(hit, 1.0, 0.0).astype(BF16))
            gates.append(jnp.sum(jnp.where(hit, aff_ref[0, 0, k:k + 1, :], 0.0), axis=1,
                                 keepdims=True))
            starts.append(c)
        res = _dot(jnp.concatenate(pieces, axis=0), h_ref[0])
        for k in range(ng):
            rows = pl.ds(starts[k], win)
            xg_ref[k, rows, :] = (xg_ref[k, rows, :].astype(F32)
                                  + res[k * win:(k + 1) * win, :]).astype(BF16)
            gate_ref[k, rows, :] += gates[k]
        return 0

    lax.fori_loop(0, rounds, body, 0)


def _moe_gather(starts, h, sel, aff, n_tok, cap, win):
    nb, _, d = h.shape
    n_exp = sel.shape[1]
    ng = min(EXPERT_GROUP, n_exp)
    n_tb = n_tok // ROW_BLK
    rows = pl.BlockSpec((1, 1, ng, ROW_BLK), lambda v, g, j, s: (v, g, 0, j))
    grouped = lambda t: t.reshape(nb, n_exp // ng, ng, n_tok)
    return pl.pallas_call(
        functools.partial(_gather_kernel, n_tb=n_tb, cap=cap, win=win, n_exp=n_exp),
        grid_spec=pltpu.PrefetchScalarGridSpec(
            num_scalar_prefetch=1,
            grid=(nb, n_exp // ng, n_tb),
            in_specs=[pl.BlockSpec((1, ROW_BLK, d), lambda v, g, j, s: (v, j, 0)), rows, rows],
            out_specs=[pl.BlockSpec((ng, cap, d), lambda v, g, j, s: (g, v, 0)),
                       pl.BlockSpec((ng, cap, 1), lambda v, g, j, s: (g, v, 0))],
        ),
        out_shape=[jax.ShapeDtypeStruct((n_exp, nb * cap, d), BF16),
                   jax.ShapeDtypeStruct((n_exp, nb * cap, 1), F32)],
        compiler_params=_cparams(("parallel", "parallel", "arbitrary"), 56),
        name="moe_gather",
    )(starts, h, grouped(sel), grouped(aff))


def _ffn_kernel(xl_ref, xc_ref, wg_ref, wu_ref, al_ref, ac_ref, wgb_ref, wub_ref):
    r = pl.program_id(2)

    @pl.when(r == 0)
    def _():
        wgb_ref[...] = wg_ref[0, 0].astype(BF16)
        wub_ref[...] = wu_ref[0, 0].astype(BF16)

    def swiglu(x_ref, o_ref):
        x = x_ref[0]
        o_ref[0] = (_silu(_dot(x, wgb_ref[...])) * _dot(x, wub_ref[...])).astype(BF16)

    @pl.when(r == 0)
    def _():
        swiglu(xc_ref, ac_ref)

    @pl.when(r > 0)
    def _():
        swiglu(xl_ref, al_ref)


def _moe_ffn(xg_l, xg_c, w_gate, w_up, layer):
    n_exp, rows, d = xg_l.shape
    rows_c = xg_c.shape[1]
    ff = w_gate.shape[3]
    fh = _largest_tile(ff, 512, LANES)
    rblk = _largest_tile(rows, 1024)
    nls = rows // rblk
    prev = lambda r: jnp.maximum(r - 1, 0)
    wspec = pl.BlockSpec((1, 1, d, fh), lambda e, f, r: (layer, e, 0, f))
    return pl.pallas_call(
        _ffn_kernel,
        grid=(n_exp, ff // fh, nls + 1),
        in_specs=[pl.BlockSpec((1, rblk, d), lambda e, f, r: (e, prev(r), 0)),
                  pl.BlockSpec((1, rows_c, d), lambda e, f, r: (e, 0, 0)), wspec, wspec],
        out_specs=[pl.BlockSpec((1, rblk, fh), lambda e, f, r: (e, prev(r), f)),
                   pl.BlockSpec((1, rows_c, fh), lambda e, f, r: (e, 0, f))],
        out_shape=[jax.ShapeDtypeStruct((n_exp, rows, ff), BF16),
                   jax.ShapeDtypeStruct((n_exp, rows_c, ff), BF16)],
        scratch_shapes=[pltpu.VMEM((d, fh), BF16), pltpu.VMEM((d, fh), BF16)],
        compiler_params=_cparams(("parallel", "parallel", "arbitrary"), 48),
        name="moe_ffn",
    )(xg_l, xg_c, w_gate, w_up)


def _down_kernel(al_ref, ac_ref, gl_ref, gc_ref, wd_ref, yl_ref, yc_ref, wdb_ref):
    v = pl.program_id(1)

    @pl.when(v == 0)
    def _():
        wdb_ref[...] = wd_ref[0, 0].astype(BF16)

    @pl.when(v == 0)
    def _():
        yc_ref[0] = (_dot(ac_ref[0], wdb_ref[...]) * gc_ref[0]).astype(BF16)

    @pl.when(v > 0)
    def _():
        yl_ref[0] = (_dot(al_ref[0], wdb_ref[...]) * gl_ref[0]).astype(BF16)


def _moe_down(act_l, act_c, gate_l, gate_c, w_down, layer, cap):
    n_exp, rows, ff = act_l.shape
    rows_c = act_c.shape[1]
    d = w_down.shape[3]
    nb = rows // cap
    prev = lambda v: jnp.maximum(v - 1, 0)
    return pl.pallas_call(
        _down_kernel,
        grid=(n_exp, nb + 1),
        in_specs=[
            pl.BlockSpec((1, cap, ff), lambda e, v: (e, prev(v), 0)),
            pl.BlockSpec((1, rows_c, ff), lambda e, v: (e, 0, 0)),
            pl.BlockSpec((1, cap, 1), lambda e, v: (e, prev(v), 0)),
            pl.BlockSpec((1, rows_c, 1), lambda e, v: (e, 0, 0)),
            pl.BlockSpec((1, 1, ff, d), lambda e, v: (layer, e, 0, 0)),
        ],
        out_specs=[pl.BlockSpec((1, cap, d), lambda e, v: (prev(v), e, 0)),
                   pl.BlockSpec((1, rows_c, d), lambda e, v: (0, e, 0))],
        out_shape=[jax.ShapeDtypeStruct((nb, n_exp * cap, d), BF16),
                   jax.ShapeDtypeStruct((1, n_exp * rows_c, d), BF16)],
        scratch_shapes=[pltpu.VMEM((ff, d), BF16)],
        compiler_params=_cparams(("parallel", "arbitrary"), 48),
        name="moe_down",
    )(act_l, act_c, gate_l, gate_c, w_down)


def _combine_kernel(starts_ref, y_ref, sel_ref, out_ref, st_ref, *, n_tb, cap, win, n_exp):
    v, j = pl.program_id(0), pl.program_id(2)
    base = v * n_exp * (n_tb + 1)
    w, rounds = _windows(starts_ref, base, n_tb + 1, n_exp, j, cap, win)
    out_ref[...] = jnp.zeros_like(out_ref)
    cols = lax.broadcasted_iota(jnp.int32, (n_exp, n_exp * win), 1)
    spread = jnp.where(cols // win == lax.broadcasted_iota(jnp.int32, (n_exp, n_exp * win), 0),
                       1.0, 0.0).astype(BF16)
    slot_in_win = (lax.broadcasted_iota(jnp.int32, (1, n_exp * win), 1) % win + 1).astype(F32)
    exp_lane = lax.broadcasted_iota(jnp.int32, (1, n_exp), 1)

    def body(r, _):
        lo_vec = jnp.zeros((1, n_exp), F32)
        c_vec = jnp.zeros((1, n_exp), F32)
        for e in range(n_exp):
            lo = w[e] + r * win
            c = pl.multiple_of(jnp.minimum(lo, cap - win), SLOT_ALIGN)
            src = pl.multiple_of(e * cap + c, SLOT_ALIGN)
            st_ref[e * win:(e + 1) * win, :] = y_ref[0, pl.ds(src, win), :]
            lo_vec = jnp.where(exp_lane == e, lo.astype(F32), lo_vec)
            c_vec = jnp.where(exp_lane == e, c.astype(F32), c_vec)
        sel = sel_ref[0]
        rel = jnp.where(sel >= lo_vec, sel, -1.0) - c_vec
        rel1 = jnp.where((rel >= 0.0) & (rel < float(win)), rel + 1.0, 0.0)
        hit = _dot(rel1.astype(BF16), spread) == slot_in_win
        out_ref[0] += _dot(jnp.where(hit, 1.0, 0.0).astype(BF16), st_ref[...])
        return 0

    lax.fori_loop(0, rounds, body, 0)


def _moe_combine(starts, y, sel_t, n_tok, cap, win):
    nb, _, d = y.shape
    n_exp = sel_t.shape[2]
    n_tb = n_tok // ROW_BLK
    pw = _largest_tile(d, 1024, LANES)
    return pl.pallas_call(
        functools.partial(_combine_kernel, n_tb=n_tb, cap=cap, win=win, n_exp=n_exp),
        grid_spec=pltpu.PrefetchScalarGridSpec(
            num_scalar_prefetch=1,
            grid=(nb, d // pw, n_tb),
            in_specs=[pl.BlockSpec((1, n_exp * cap, pw), lambda v, p, j, s: (v, 0, p)),
                      pl.BlockSpec((1, ROW_BLK, n_exp), lambda v, p, j, s: (v, j, 0))],
            out_specs=pl.BlockSpec((1, ROW_BLK, pw), lambda v, p, j, s: (v, j, p)),
            scratch_shapes=[pltpu.VMEM((n_exp * win, pw), BF16)],
        ),
        out_shape=jax.ShapeDtypeStruct((nb, n_tok, d), F32),
        compiler_params=_cparams(("parallel", "parallel", "arbitrary"), 48),
        name="moe_combine",
    )(starts, y, sel_t)


def _rope_tables(pos_groups, half):
    freqs = ROPE_BASE ** (-jnp.arange(half, dtype=F32) / half)
    cos, sin = [], []
    for pos in pos_groups:
        ang = pos.astype(F32)[:, None] * freqs[None, :]
        c, s = jnp.cos(ang), jnp.sin(ang)
        cos += [c, c]
        sin += [-s, s]
    return jnp.stack([jnp.concatenate(t, axis=1) for t in (cos, sin)])


def _attn_tables(n_lat, n_ctx):
    s = jnp.arange(n_lat)
    tab = _rope_tables([s // GRID_W, s % GRID_W], HEAD_DIM // 4)
    ident = jnp.stack([jnp.ones((n_ctx, HEAD_DIM), F32), jnp.zeros((n_ctx, HEAD_DIM), F32)])
    return jnp.concatenate([tab, ident], axis=1)


def _ret_tables(n_lat, n_ctx):
    s = jnp.arange(n_lat)
    t = jnp.arange(n_ctx)
    fwd = jnp.concatenate([n_ctx + s, t])
    bwd = jnp.concatenate([n_ctx + (n_lat - 1 - s), n_ctx - 1 - t])
    return jnp.concatenate([_rope_tables([fwd], HEAD_DIM // 2),
                            _rope_tables([bwd], HEAD_DIM // 2)], axis=0)


def kernel(x, c, ctx, c_ctx, w_ada, b_ada, norm_mix, norm_ffn, w_in, q_norm, k_norm, ret_log_decay,
           w_out, w_router, w_gate, w_up, w_down):
    bsz, n_lat, d = x.shape
    n_ctx = ctx.shape[1]
    seq_len = n_lat + n_ctx
    depth = w_ada.shape[0]
    n_exp = w_router.shape[2]
    n_lat_blk = n_lat // ROW_BLK
    assert n_lat % ROW_BLK == 0 and n_ctx == ROW_BLK and bsz + 1 <= 8
    cap_l = CAPACITY_FACTOR * n_lat // n_exp
    cap_c = CAPACITY_FACTOR * n_ctx // n_exp

    cond8 = jnp.zeros((8, d), F32).at[:bsz].set(c).at[bsz].set(c_ctx)
    mods = _adaln(cond8, w_ada, b_ada).reshape(depth, 8, 6, d)

    attn_tab = _attn_tables(n_lat, n_ctx)
    ret_tab = _ret_tables(n_lat, n_ctx)
    wc = _chan_mats()
    two_stage = n_lat % (FFT_COLS * 2 * SUBLANES) == 0
    lat_mats = dict(fft=_fft_mats(n_lat)) if two_stage else dict(dft=_dft_mats(n_lat))
    dft_c = _dft_mats(n_ctx)

    o_q, o_k = 0, ATTN_WIDTH
    o_v = o_k + KV_WIDTH
    o_f = o_v + KV_WIDTH
    o_rq = o_f + FOURIER_WIDTH
    o_rk = o_rq + RET_WIDTH
    o_rv = o_rk + RET_WIDTH
    o_g = o_rv + RET_WIDTH
    o_end = o_g + 2 * RET_WIDTH
    q_scale = HEAD_DIM ** -0.5 * math.log2(math.e)

    sb_c = bsz * cap_c
    win_l = min(cap_l, SLOT_WINDOW)
    win_c = cap_c
    starts_c = jnp.tile(jnp.arange(bsz + 1, dtype=jnp.int32) * cap_c, n_exp)

    xs = jnp.concatenate([x, ctx], axis=1)
    delta_lat = delta_ctx = None
    for l in range(depth):
        xs, h = _norm(xs, mods[l], norm_mix[l], n_lat_blk, delta_lat, delta_ctx,
                      mods[l - 1] if l else None)
        h2d = h.reshape(bsz * seq_len, d)
        wl = w_in[l].astype(BF16)
        q = _proj(h2d, wl[:, o_q:o_k], seq_len, "qk", gain=q_norm[l], tab=attn_tab,
                  n_heads=ATTN_HEADS, scale=q_scale, half=HEAD_DIM // 4)
        k = _proj(h2d, wl[:, o_k:o_v], seq_len, "qk", gain=k_norm[l], tab=attn_tab,
                  n_heads=ATTN_KV_HEADS, scale=1.0, half=HEAD_DIM // 4)
        v, f = _proj(h2d, wl[:, o_v:o_rq], seq_len, "split", split_at=KV_WIDTH)
        rq = _proj(h2d, wl[:, o_rq:o_rk], seq_len, "ret", tab=ret_tab, n_heads=RET_HEADS,
                   scale=1.0, half=HEAD_DIM // 2)
        rk = _proj(h2d, wl[:, o_rk:o_rv], seq_len, "ret", tab=ret_tab, n_heads=RET_HEADS,
                   scale=HEAD_DIM ** -0.5, half=HEAD_DIM // 2)
        rv = _proj(h2d, wl[:, o_rv:o_g], seq_len, "plain")
        sg = _proj(h2d, wl[:, o_g:o_end], seq_len, "silu")

        a = _attention(q, k, v, bsz, seq_len, n_lat)
        fm_lat = _fourier(f, bsz, seq_len, 0, n_lat, wc, **lat_mats)
        fm_ctx = _fourier(f, bsz, seq_len, n_lat, n_ctx, wc, dft=dft_c)
        o_f, o_b = _retention(rq, rk, rv, ret_log_decay[l], bsz, seq_len, n_lat)

        xs, h2, aff_t = _outproj(a, fm_lat, fm_ctx, o_f, o_b, sg, w_out[l].astype(BF16), xs, mods[l],
                                 norm_ffn[l], w_router[l], n_lat_blk)

        sel_l, gat_l, sel_c, gat_c, st_l = _routing(aff_t, bsz, seq_len, n_lat, cap_l, cap_c)
        starts_l = st_l[:, :, :n_lat_blk + 1].astype(jnp.int32).reshape(-1)
        merge = lambda t: t.transpose(1, 0, 2).reshape(1, n_exp, bsz * n_ctx)
        tok_major = lambda t: t.transpose(0, 2, 1)
        sel_c, gat_c = merge(sel_c), merge(gat_c)
        h_ctx = h2[:, n_lat:, :].reshape(1, bsz * n_ctx, d)

        xg_l, gate_l = _moe_gather(starts_l, h2, sel_l, gat_l, n_lat, cap_l, win_l)
        xg_c, gate_c = _moe_gather(starts_c, h_ctx, sel_c, gat_c, bsz * n_ctx, sb_c, win_c)
        act_l, act_c = _moe_ffn(xg_l, xg_c, w_gate, w_up, l)
        y_l, y_c = _moe_down(act_l, act_c, gate_l, gate_c, w_down, l, cap_l)
        delta_lat = _moe_combine(starts_l, y_l, tok_major(sel_l), n_lat, cap_l, win_l)
        delta_ctx = _moe_combine(starts_c, y_c, tok_major(sel_c), bsz * n_ctx, sb_c,
                                 win_c).reshape(bsz * n_ctx, d)

    out, _ = _norm(xs, None, None, n_lat_blk, delta_lat, delta_ctx, mods[depth - 1],
                   want_h=False, lat_only=True)
    return out
```

```python
import functools
import math

import jax
import jax.numpy as jnp
from jax import lax
from jax.experimental import pallas as pl
from jax.experimental.pallas import tpu as pltpu

HEAD_DIM = 128
ATTN_HEADS = 8
ATTN_KV_HEADS = 2
ATTN_GROUP = ATTN_HEADS // ATTN_KV_HEADS
FOURIER_GROUPS = 4
RET_HEADS = 4
RET_CHUNK = 128
GRID_W = 64
ROPE_BASE = 10000.0
EPS = 1e-6
CAPACITY_FACTOR = 2

ATTN_WIDTH = ATTN_HEADS * HEAD_DIM
KV_WIDTH = ATTN_KV_HEADS * HEAD_DIM
FOURIER_WIDTH = FOURIER_GROUPS * HEAD_DIM
RET_WIDTH = RET_HEADS * HEAD_DIM

ROW_BLK = 256
LANES = 128
MIB = 1024 * 1024

F32 = jnp.float32
BF16 = jnp.bfloat16


def _cparams(sem, vmem_mib):
    return pltpu.CompilerParams(dimension_semantics=sem, vmem_limit_bytes=vmem_mib * MIB)


def _largest_tile(n, cap, mult=8):
    best = mult
    for t in range(mult, min(n, cap) + 1, mult):
        if n % t == 0:
            best = t
    return best


def _split(a):
    hi = a.astype(BF16)
    lo = (a - hi.astype(F32)).astype(BF16)
    return hi, lo


def _dot(a, b):
    return jnp.dot(a, b, preferred_element_type=F32)


def _dot_nt(a, b):
    return lax.dot_general(a, b, (((1,), (1,)), ((), ())), preferred_element_type=F32)


def _dot_tn(a, b):
    return lax.dot_general(a, b, (((0,), (0,)), ((), ())), preferred_element_type=F32)


def _silu(a):
    return a / (1.0 + jnp.exp(-a))


def _adaln_kernel(c_ref, w_ref, b_ref, o_ref):
    s = _silu(c_ref[...])
    sh, sl = _split(s)
    wh, wl = _split(w_ref[0])
    o_ref[0] = _dot(sh, wh) + (_dot(sh, wl) + _dot(sl, wh)) + b_ref[0]


def _adaln(cond8, w_ada, b_ada):
    depth, d, n = w_ada.shape
    tn = _largest_tile(n, 768, LANES)
    return pl.pallas_call(
        _adaln_kernel,
        grid=(depth, n // tn),
        in_specs=[
            pl.BlockSpec((8, d), lambda l, j: (0, 0)),
            pl.BlockSpec((1, d, tn), lambda l, j: (l, 0, j)),
            pl.BlockSpec((1, 1, tn), lambda l, j: (l, 0, j)),
        ],
        out_specs=pl.BlockSpec((1, 8, tn), lambda l, j: (l, 0, j)),
        out_shape=jax.ShapeDtypeStruct((depth, 8, n), F32),
        compiler_params=_cparams(("parallel", "parallel"), 48),
        name="adaln",
    )(cond8, w_ada, b_ada.reshape(depth, 1, n))


def _rms_mod(x, g, shift, scale):
    y = x * lax.rsqrt(jnp.mean(x * x, axis=-1, keepdims=True) + EPS)
    return (y * g) * (1.0 + scale) + shift


def _norm_kernel(*refs, has_delta, want_h, n_lat_blk):
    refs = list(refs)
    x_ref = refs.pop(0)
    x = x_ref[0]
    if has_delta:
        dl_ref, dc_ref, pm_ref = refs.pop(0), refs.pop(0), refs.pop(0)
        is_ctx = pl.program_id(1) >= n_lat_blk
        delta = jnp.where(is_ctx, dc_ref[...], dl_ref[0])
        x = x + pm_ref[0, 5:6, :] * delta
    if want_h:
        m_ref, g_ref = refs.pop(0), refs.pop(0)
    if has_delta:
        xo_ref = refs.pop(0)
        xo_ref[0] = x
    if want_h:
        h_ref = refs.pop(0)
        h_ref[0] = _rms_mod(x, g_ref[...], m_ref[0, 0:1, :], m_ref[0, 1:2, :]).astype(BF16)


def _norm(x, mods, g, n_lat_blk, delta_lat=None, delta_ctx=None, prev_mods=None,
          want_h=True, lat_only=False):
    bsz, l, d = x.shape
    n_blk = n_lat_blk if lat_only else l // ROW_BLK
    has_delta = delta_lat is not None

    def mod_row(b, t):
        return (jnp.where(t >= n_lat_blk, bsz, b), 0, 0)

    xspec = pl.BlockSpec((1, ROW_BLK, d), lambda b, t: (b, t, 0))
    in_specs, args = [xspec], [x]
    if has_delta:
        in_specs += [
            pl.BlockSpec((1, ROW_BLK, d), lambda b, t: (b, jnp.minimum(t, n_lat_blk - 1), 0)),
            pl.BlockSpec((ROW_BLK, d), lambda b, t: (b, 0)),
            pl.BlockSpec((1, 6, d), mod_row),
        ]
        args += [delta_lat, delta_ctx, prev_mods]
    if want_h:
        in_specs += [pl.BlockSpec((1, 6, d), mod_row), pl.BlockSpec((1, d), lambda b, t: (0, 0))]
        args += [mods, g.reshape(1, d)]
    out_specs, out_shape = [], []
    rows = n_blk * ROW_BLK
    if has_delta:
        out_specs.append(xspec)
        out_shape.append(jax.ShapeDtypeStruct((bsz, rows, d), F32))
    if want_h:
        out_specs.append(xspec)
        out_shape.append(jax.ShapeDtypeStruct((bsz, rows, d), BF16))
    outs = pl.pallas_call(
        functools.partial(_norm_kernel, has_delta=has_delta, want_h=want_h, n_lat_blk=n_lat_blk),
        grid=(bsz, n_blk),
        in_specs=in_specs,
        out_specs=out_specs,
        out_shape=out_shape,
        compiler_params=_cparams(("parallel", "parallel"), 32),
        name="norm",
    )(*args)
    outs = list(outs)
    x_new = outs.pop(0) if has_delta else x
    h = outs.pop(0) if want_h else None
    return x_new, h


def _dot_wide(a, m):
    hi, lo = _split(a)
    return _dot(hi, m) + _dot(lo, m)


def _partner_matrix(half):
    src = lax.broadcasted_iota(jnp.int32, (LANES, LANES), 0)
    dst = lax.broadcasted_iota(jnp.int32, (LANES, LANES), 1)
    mate = jnp.where(dst % (2 * half) < half, dst + half, dst - half)
    return jnp.where(src == mate, 1.0, 0.0).astype(BF16)


def _proj_kernel(*refs, flavor, n_heads, scale, half, split_at):
    h_ref, w_ref = refs[0], refs[1]
    acc = _dot(h_ref[...], w_ref[...])
    if flavor == "plain":
        refs[2][...] = acc.astype(BF16)
    elif flavor == "split":
        refs[2][...] = acc[:, :split_at].astype(BF16)
        refs[3][...] = acc[:, split_at:].astype(BF16)
    elif flavor == "silu":
        refs[2][...] = _silu(acc).astype(BF16)
    elif flavor == "qk":
        gain_ref, tab_ref, o_ref = refs[2], refs[3], refs[4]
        partner = _partner_matrix(half)
        ones = jnp.ones((LANES, LANES), BF16)
        for hd in range(n_heads):
            sl = slice(hd * HEAD_DIM, (hd + 1) * HEAD_DIM)
            z = acc[:, sl]
            ms = _dot_wide(z * z, ones) * (1.0 / HEAD_DIM)
            z = (z * lax.rsqrt(ms + EPS)) * gain_ref[...]
            z = z * tab_ref[0] + _dot_wide(z, partner) * tab_ref[1]
            o_ref[:, sl] = (z * scale).astype(BF16)
    elif flavor == "ret":
        tab_ref, o_ref = refs[2], refs[3]
        partner = _partner_matrix(half)
        width = n_heads * HEAD_DIM
        for hd in range(n_heads):
            sl = slice(hd * HEAD_DIM, (hd + 1) * HEAD_DIM)
            z = acc[:, sl] * scale
            pz = _dot_wide(z, partner)
            o_ref[:, sl] = (z * tab_ref[0] + pz * tab_ref[1]).astype(BF16)
            o_ref[:, width + hd * HEAD_DIM: width + (hd + 1) * HEAD_DIM] = (
                z * tab_ref[2] + pz * tab_ref[3]).astype(BF16)
    else:
        raise ValueError(flavor)


def _proj(h2d, w, seq_len, flavor, *, gain=None, tab=None, n_heads=0, scale=1.0, half=0,
          split_at=0):
    t_rows, d = h2d.shape
    n = w.shape[1]
    tm = _largest_tile(seq_len, 1088)
    per_seq = seq_len // tm
    in_specs = [pl.BlockSpec((tm, d), lambda i: (i, 0)), pl.BlockSpec((d, n), lambda i: (0, 0))]
    args = [h2d, w]
    if flavor == "qk":
        in_specs.append(pl.BlockSpec((1, HEAD_DIM), lambda i: (0, 0)))
        args.append(gain.reshape(1, HEAD_DIM))
    if flavor in ("qk", "ret"):
        ntab = tab.shape[0]
        in_specs.append(pl.BlockSpec((ntab, tm, HEAD_DIM), lambda i: (0, i % per_seq, 0)))
        args.append(tab)
    if flavor == "split":
        widths = [split_at, n - split_at]
    elif flavor == "ret":
        widths = [2 * n]
    else:
        widths = [n]
    out_specs = [pl.BlockSpec((tm, wd), lambda i: (i, 0)) for wd in widths]
    out_shape = [jax.ShapeDtypeStruct((t_rows, wd), BF16) for wd in widths]
    outs = pl.pallas_call(
        functools.partial(_proj_kernel, flavor=flavor, n_heads=n_heads, scale=scale, half=half,
                          split_at=split_at),
        grid=(t_rows // tm,),
        in_specs=in_specs,
        out_specs=out_specs,
        out_shape=out_shape,
        compiler_params=_cparams(("parallel",), 48),
        name="proj_" + flavor,
    )(*args)
    return outs if len(outs) > 1 else outs[0]


SUBLANES = 8


SAFE_SHIFT = 50.0


def _lane_fold(t, op):
    return functools.reduce(op, [t[:, i * LANES:(i + 1) * LANES]
                                 for i in range(t.shape[1] // LANES)])


def _fold_rows(t, op):
    return functools.reduce(op, [t[r:r + SUBLANES, :] for r in range(0, t.shape[0], SUBLANES)])


def _attn_group_bounded(q_all, m_row, k_ref, vt_ref, pt_ref, k_lo, n_keys, tk):
    lp = None
    for c in range(n_keys // tk):
        st = _dot_nt(k_ref[k_lo + c * tk:k_lo + (c + 1) * tk, :], q_all)
        p = jnp.exp2(st - m_row)
        part = _fold_rows(p, jnp.add)
        lp = part if lp is None else lp + part
        pt_ref[c * tk:(c + 1) * tk, :] = p.astype(BF16)
    l = jnp.sum(lp, axis=0, keepdims=True)
    return _dot(vt_ref[:, k_lo:k_lo + n_keys], pt_ref[:n_keys, :]) / l


def _attn_head_exact(q, k_ref, v_ref, s_ref, p_ref, k_lo, n_keys, tk):
    s_ref[:, :n_keys] = _dot_nt(q, k_ref[k_lo:k_lo + n_keys, :])
    n_chunks = n_keys // tk
    mp = None
    for c in range(n_chunks):
        part = _lane_fold(s_ref[:, c * tk:(c + 1) * tk], jnp.maximum)
        mp = part if mp is None else jnp.maximum(mp, part)
    m = jnp.max(mp, axis=-1, keepdims=True)
    lp = None
    for c in range(n_chunks):
        p = jnp.exp2(s_ref[:, c * tk:(c + 1) * tk] - m)
        part = _lane_fold(p, jnp.add)
        lp = part if lp is None else lp + part
        p_ref[:, c * tk:(c + 1) * tk] = p.astype(BF16)
    l = jnp.sum(lp, axis=-1, keepdims=True)
    return _dot(p_ref[:, :n_keys], v_ref[k_lo:k_lo + n_keys, :]) / l


def _attn_kernel(q_ref, k_ref, v_ref, o_ref, s_ref, p_ref, pt_ref, vt_ref, kmax_ref, *, tk, n_lat,
                 n_ctx, n_lat_qblk):
    i = pl.program_id(2)
    is_ctx = i >= n_lat_qblk

    @pl.when(i == 0)
    def _():
        kk = k_ref[...].astype(F32)
        kn2 = jnp.max(jnp.sum(kk * kk, axis=-1, keepdims=True), axis=0, keepdims=True)
        kmax_ref[...] = jnp.broadcast_to(jnp.sqrt(kn2), kmax_ref.shape)
        vt_ref[...] = v_ref[...].astype(F32).T.astype(BF16)

    heads = [slice(hd * HEAD_DIM, (hd + 1) * HEAD_DIM) for hd in range(ATTN_GROUP)]
    tq = q_ref.shape[0]
    q_all = jnp.concatenate([q_ref[:, sl] for sl in heads], axis=0)
    qf = q_all.astype(F32)
    qn2 = _dot_nt(jnp.ones((SUBLANES, HEAD_DIM), BF16), (qf * qf).astype(BF16))[0:1, :]
    m_row = jnp.sqrt(qn2) * kmax_ref[0:1, 0:1] * (1.0 + 2.0 ** -6)
    worst = jnp.max(m_row)
    bounded_ok = worst <= SAFE_SHIFT

    def run(k_lo, n_keys, bounded):
        if bounded:
            ot = _attn_group_bounded(q_all, m_row, k_ref, vt_ref, pt_ref, k_lo, n_keys, tk)
            outs = [ot[:, hd * tq:(hd + 1) * tq].T for hd in range(len(heads))]
        else:
            outs = [_attn_head_exact(q_ref[:, sl], k_ref, v_ref, s_ref.at[hd], p_ref.at[hd], k_lo,
                                     n_keys, tk) for hd, sl in enumerate(heads)]
        for o, sl in zip(outs, heads):
            o_ref[:, sl] = o.astype(BF16)

    for ctx_case, (k_lo, n_keys) in ((False, (0, n_lat + n_ctx)), (True, (n_lat, n_ctx))):
        for bounded in (True, False):
            @pl.when((is_ctx == ctx_case) & (bounded_ok == bounded))
            def _(k_lo=k_lo, n_keys=n_keys, bounded=bounded):
                run(k_lo, n_keys, bounded)


def _attention(q, k, v, bsz, seq_len, n_lat):
    tq = ROW_BLK
    tk = ROW_BLK
    per_seq = seq_len // tq
    gw = ATTN_GROUP * HEAD_DIM
    return pl.pallas_call(
        functools.partial(_attn_kernel, tk=tk, n_lat=n_lat, n_ctx=seq_len - n_lat,
                          n_lat_qblk=n_lat // tq),
        grid=(bsz, ATTN_KV_HEADS, per_seq),
        in_specs=[
            pl.BlockSpec((tq, gw), lambda b, g, i: (b * per_seq + i, g)),
            pl.BlockSpec((seq_len, HEAD_DIM), lambda b, g, i: (b, g)),
            pl.BlockSpec((seq_len, HEAD_DIM), lambda b, g, i: (b, g)),
        ],
        out_specs=pl.BlockSpec((tq, gw), lambda b, g, i: (b * per_seq + i, g)),
        out_shape=jax.ShapeDtypeStruct(q.shape, BF16),
        scratch_shapes=[pltpu.VMEM((ATTN_GROUP, tq, seq_len), F32),
                        pltpu.VMEM((ATTN_GROUP, tq, seq_len), BF16),
                        pltpu.VMEM((seq_len, ATTN_GROUP * tq), BF16),
                        pltpu.VMEM((HEAD_DIM, seq_len), BF16),
                        pltpu.VMEM((SUBLANES, LANES), F32)],
        compiler_params=_cparams(("parallel", "parallel", "arbitrary"), 58),
        name="attention",
    )(q, k, v)


def _fourier_a_kernel(f_ref, wc_ref, o_ref):
    g = _dot(f_ref[...], wc_ref[...])
    wdt = f_ref.shape[1]
    o_ref[0, 0] = g[:, :wdt].astype(o_ref.dtype)
    o_ref[0, 1] = g[:, wdt:].astype(o_ref.dtype)


def _fourier_b_kernel(m_ref, g_ref, o_ref):
    o_ref[0] = _dot(m_ref[...], g_ref[0]).astype(BF16)


FFT_COLS = 64
FFT_PER_STEP = SUBLANES


def _fft1_kernel(g_ref, m_ref, tc_ref, ts_ref, o_ref):
    width = o_ref.shape[4]
    n1 = g_ref.shape[2]
    for s in range(g_ref.shape[3]):
        g = jnp.concatenate([g_ref[0, 0, :, s, :], g_ref[0, 1, :, s, :]], axis=0).astype(BF16)
        a = _dot(m_ref[...], g)
        ar, ai = a[:n1], a[n1:]
        tc = tc_ref[0, :, s * LANES:(s + 1) * LANES]
        ts = ts_ref[0, :, s * LANES:(s + 1) * LANES]
        lanes = [slice(q * LANES, (q + 1) * LANES) for q in range(width // LANES)]
        o_ref[0, 0, :, s, :] = jnp.concatenate([ar[:, q] * tc + ai[:, q] * ts for q in lanes], axis=1)
        o_ref[0, 1, :, s, :] = jnp.concatenate([ai[:, q] * tc - ar[:, q] * ts for q in lanes], axis=1)


def _fft2_kernel(b_ref, m_ref, o_ref):
    for s in range(b_ref.shape[2]):
        b = jnp.concatenate([b_ref[0, 0, s], b_ref[0, 1, s]], axis=0).astype(BF16)
        o_ref[0, :, s, :] = _dot(m_ref[...], b)


def _fourier_two_stage(g, fft):
    m1, tc, ts, m3 = fft
    bsz, _, n, width = g.shape
    n2 = m3.shape[0]
    n1 = n // n2
    per = FFT_PER_STEP
    b = pl.pallas_call(
        _fft1_kernel,
        grid=(bsz, n2 // per),
        in_specs=[
            pl.BlockSpec((1, 2, n1, per, width), lambda b, k: (b, 0, 0, k, 0)),
            pl.BlockSpec(m1.shape, lambda b, k: (0, 0)),
            pl.BlockSpec((1, n1, per * LANES), lambda b, k: (k, 0, 0)),
            pl.BlockSpec((1, n1, per * LANES), lambda b, k: (k, 0, 0)),
        ],
        out_specs=pl.BlockSpec((1, 2, n1, per, width), lambda b, k: (b, 0, 0, k, 0)),
        out_shape=jax.ShapeDtypeStruct((bsz, 2, n1, n2, width), F32),
        compiler_params=_cparams(("parallel", "parallel"), 32),
        name="fourier_fft1",
    )(g.reshape(bsz, 2, n1, n2, width), m1, tc, ts)
    out = pl.pallas_call(
        _fft2_kernel,
        grid=(bsz, n1 // per),
        in_specs=[
            pl.BlockSpec((1, 2, per, n2, width), lambda b, k: (b, 0, k, 0, 0)),
            pl.BlockSpec(m3.shape, lambda b, k: (0, 0)),
        ],
        out_specs=pl.BlockSpec((1, n2, per, width), lambda b, k: (b, 0, k, 0)),
        out_shape=jax.ShapeDtypeStruct((bsz, n2, n1, width), F32),
        compiler_params=_cparams(("parallel", "parallel"), 32),
        name="fourier_fft2",
    )(b, m3)
    return out.reshape(bsz, n, width)


def _fft_mats(n):
    n2 = FFT_COLS
    n1 = n // n2
    per = FFT_PER_STEP

    def cs(rows, cols, period):
        ang = ((rows[:, None] * cols[None, :]) % period).astype(F32) * (2.0 * math.pi / period)
        return jnp.cos(ang), jnp.sin(ang)

    i1, i2 = jnp.arange(n1, dtype=jnp.int32), jnp.arange(n2, dtype=jnp.int32)
    c1, s1 = cs(i1, i1, n1)
    m1 = (jnp.block([[c1, s1], [-s1, c1]]) * n1 ** -0.5).astype(BF16)
    c3, s3 = cs(i2, i2, n2)
    m3 = (jnp.concatenate([c3, s3], axis=1) * n2 ** -0.5).astype(BF16)
    tc, ts = cs(i1, i2, n)
    expand = lambda t: jnp.repeat(t.T.reshape(n2 // per, per, n1).transpose(0, 2, 1), LANES,
                                  axis=2)
    return m1, expand(tc), expand(ts), m3


def _fourier(f2d, bsz, seq_len, row_off, n, wc, dft=None, fft=None):
    per_seq = seq_len // ROW_BLK
    off_blk = row_off // ROW_BLK
    nb = n // ROW_BLK
    width = f2d.shape[1]
    g = pl.pallas_call(
        _fourier_a_kernel,
        grid=(bsz, nb),
        in_specs=[
            pl.BlockSpec((ROW_BLK, width), lambda b, t: (b * per_seq + off_blk + t, 0)),
            pl.BlockSpec((width, 2 * width), lambda b, t: (0, 0)),
        ],
        out_specs=pl.BlockSpec((1, 2, ROW_BLK, width), lambda b, t: (b, 0, t, 0)),
        out_shape=jax.ShapeDtypeStruct((bsz, 2, n, width), BF16 if fft is None else F32),
        compiler_params=_cparams(("parallel", "parallel"), 32),
        name="fourier_chan",
    )(f2d, wc)
    if fft is not None:
        return _fourier_two_stage(g, fft)
    g = g.reshape(bsz, 2 * n, width)
    return pl.pallas_call(
        _fourier_b_kernel,
        grid=(nb, bsz),
        in_specs=[
            pl.BlockSpec((ROW_BLK, 2 * n), lambda i, b: (i, 0)),
            pl.BlockSpec((1, 2 * n, width), lambda i, b: (b, 0, 0)),
        ],
        out_specs=pl.BlockSpec((1, ROW_BLK, width), lambda i, b: (b, i, 0)),
        out_shape=jax.ShapeDtypeStruct((bsz, n, width), BF16),
        compiler_params=_cparams(("parallel", "parallel"), 48),
        name="fourier_pos",
    )(dft, g)


def _dft_mats(n):
    i = jnp.arange(n, dtype=jnp.int32)
    prod = (i[:, None] * i[None, :]) % n
    ang = prod.astype(F32) * (2.0 * math.pi / n)
    s = n ** -0.5
    return jnp.concatenate([jnp.cos(ang) * s, jnp.sin(ang) * s], axis=1).astype(BF16)


def _chan_mats():
    i = jnp.arange(HEAD_DIM, dtype=jnp.int32)
    ang = ((i[:, None] * i[None, :]) % HEAD_DIM).astype(F32) * (2.0 * math.pi / HEAD_DIM)
    s = HEAD_DIM ** -0.5
    eye = jnp.eye(FOURIER_GROUPS, dtype=F32)
    c = jnp.kron(eye, jnp.cos(ang) * s)
    sn = jnp.kron(eye, -jnp.sin(ang) * s)
    return jnp.concatenate([c, sn], axis=1).astype(BF16)


RET_HEADS_PER_STEP = 2


def _ret_kernel(ld_ref, qf_ref, qb_ref, kf_ref, kb_ref, v_ref, of_ref, ob_ref, state_ref,
                *, n_lat_chunks, n_ctx_chunks):
    c = RET_CHUNK
    n_all = n_lat_chunks + n_ctx_chunks
    h0 = pl.program_id(1) * RET_HEADS_PER_STEP
    ii = lax.broadcasted_iota(jnp.int32, (c, c), 0).astype(F32)
    jj = lax.broadcasted_iota(jnp.int32, (c, c), 1).astype(F32)
    ri = lax.broadcasted_iota(jnp.int32, (c, 1), 0).astype(F32)
    state_ref[...] = jnp.zeros_like(state_ref)

    chains = []
    for hh in range(RET_HEADS_PER_STEP):
        sl = slice(hh * HEAD_DIM, (hh + 1) * HEAD_DIM)
        for d, (q_ref, k_ref, o_ref) in enumerate(((qf_ref, kf_ref, of_ref),
                                                   (qb_ref, kb_ref, ob_ref))):
            lg = ld_ref[d, h0 + hh]
            diff = ii - jj if d == 0 else jj - ii
            intra = jnp.where(diff >= 0, jnp.exp(lg * jnp.maximum(diff, 0.0)), 0.0)
            q_dec = jnp.exp(lg * (ri + 1.0 if d == 0 else c - ri))
            k_dec = jnp.exp(lg * (c - 1.0 - ri if d == 0 else ri))
            c_dec = jnp.exp(jnp.full((1, HEAD_DIM), lg * c, F32))
            chains.append((len(chains), q_ref, k_ref, o_ref, sl, (intra, q_dec, k_dec, c_dec), d))

    def body(s, _):
        in_ctx = s < n_ctx_chunks
        for slot, q_ref, k_ref, o_ref, sl, (intra, q_dec, k_dec, c_dec), d in chains:
            if d == 0:
                chunk = jnp.where(in_ctx, n_lat_chunks + s, s - n_ctx_chunks)
            else:
                chunk = n_all - 1 - s
            off = pl.multiple_of(chunk * c, c)
            q = q_ref[pl.ds(off, c), sl]
            k = k_ref[pl.ds(off, c), sl]
            v = v_ref[pl.ds(off, c), sl]
            st = state_ref[slot]
            sc = _dot_nt(q, k) * intra
            o = _dot(sc.astype(BF16), v) + _dot(q, st.astype(BF16)) * q_dec
            kd = (k.astype(F32) * k_dec).astype(BF16)
            state_ref[slot] = st * c_dec + _dot_tn(kd, v)
            mu = jnp.mean(o, axis=-1, keepdims=True)
            var = jnp.mean(jnp.square(o - mu), axis=-1, keepdims=True)
            o_ref[pl.ds(off, c), sl] = (o - mu) * lax.rsqrt(var + EPS)
        return 0

    lax.fori_loop(0, n_all, body, 0, unroll=True)


def _retention(rq, rk, rv, log_decay, bsz, seq_len, n_lat):
    hw = RET_HEADS_PER_STEP * HEAD_DIM
    n_hb = RET_HEADS // RET_HEADS_PER_STEP
    fwd = pl.BlockSpec((seq_len, hw), lambda b, h: (b, h))
    bwd = pl.BlockSpec((seq_len, hw), lambda b, h: (b, n_hb + h))
    return pl.pallas_call(
        functools.partial(_ret_kernel, n_lat_chunks=n_lat // RET_CHUNK,
                          n_ctx_chunks=(seq_len - n_lat) // RET_CHUNK),
        grid=(bsz, n_hb),
        in_specs=[pl.BlockSpec(memory_space=pltpu.SMEM), fwd, bwd, fwd, bwd, fwd],
        out_specs=[fwd, fwd],
        out_shape=[jax.ShapeDtypeStruct((bsz * seq_len, RET_WIDTH), F32)] * 2,
        scratch_shapes=[pltpu.VMEM((2 * RET_HEADS_PER_STEP, HEAD_DIM, HEAD_DIM), F32)],
        compiler_params=_cparams(("parallel", "parallel"), 56),
        name="retention",
    )(log_decay, rq, rq, rk, rk, rv)


def _outproj_kernel(a_ref, fl_ref, fc_ref, of_ref, ob_ref, sg_ref, w_ref, x_ref, m_ref, g_ref, wr_ref,
                    xo_ref, h_ref, aff_ref, *, n_lat_blk):
    is_ctx = pl.program_id(1) >= n_lat_blk
    fm = jnp.where(is_ctx, fc_ref[0], fl_ref[0].astype(BF16))
    rw = RET_WIDTH
    r = (sg_ref[:, :rw].astype(F32) * of_ref[...]
         + sg_ref[:, rw:].astype(F32) * ob_ref[...]).astype(BF16)
    a0, f0 = ATTN_WIDTH, ATTN_WIDTH + FOURIER_WIDTH
    y = (_dot(a_ref[...], w_ref[:a0, :]) + _dot(fm, w_ref[a0:f0, :])) + _dot(r, w_ref[f0:, :])
    x = x_ref[0] + m_ref[0, 2:3, :] * y
    xo_ref[0] = x
    h = _rms_mod(x, g_ref[...], m_ref[0, 3:4, :], m_ref[0, 4:5, :])
    h_ref[0] = h.astype(BF16)
    n_exp = aff_ref.shape[0]
    hh, hl = _split(h)
    wh, wl = _split(wr_ref[...])
    lg = _dot(hh, wh) + (_dot(hh, wl) + _dot(hl, wh))
    lg = jnp.where(lax.broadcasted_iota(jnp.int32, lg.shape, 1) < n_exp, lg, -jnp.inf)
    e = jnp.exp(lg - jnp.max(lg, axis=-1, keepdims=True))
    aff = e / jnp.sum(e, axis=-1, keepdims=True)
    aff_ref[...] = aff.T[:n_exp, :]


def _outproj(a, fm_lat, fm_ctx, o_f, o_b, sg, w_out, x, mods, g, w_router, n_lat_blk):
    bsz, l, d = x.shape
    per_seq = l // ROW_BLK
    n_exp = w_router.shape[1]
    wr = jnp.pad(w_router, ((0, 0), (0, LANES - n_exp)))
    fw = fm_lat.shape[-1]

    def flat(b, t):
        return (b * per_seq + t, 0)

    return pl.pallas_call(
        functools.partial(_outproj_kernel, n_lat_blk=n_lat_blk),
        grid=(bsz, per_seq),
        in_specs=[
            pl.BlockSpec((ROW_BLK, a.shape[1]), flat),
            pl.BlockSpec((1, ROW_BLK, fw), lambda b, t: (b, jnp.minimum(t, n_lat_blk - 1), 0)),
            pl.BlockSpec((1, ROW_BLK, fw), lambda b, t: (b, 0, 0)),
            pl.BlockSpec((ROW_BLK, o_f.shape[1]), flat),
            pl.BlockSpec((ROW_BLK, o_b.shape[1]), flat),
            pl.BlockSpec((ROW_BLK, sg.shape[1]), flat),
            pl.BlockSpec(w_out.shape, lambda b, t: (0, 0)),
            pl.BlockSpec((1, ROW_BLK, d), lambda b, t: (b, t, 0)),
            pl.BlockSpec((1, 6, d), lambda b, t: (jnp.where(t >= n_lat_blk, bsz, b), 0, 0)),
            pl.BlockSpec((1, d), lambda b, t: (0, 0)),
            pl.BlockSpec(wr.shape, lambda b, t: (0, 0)),
        ],
        out_specs=[
            pl.BlockSpec((1, ROW_BLK, d), lambda b, t: (b, t, 0)),
            pl.BlockSpec((1, ROW_BLK, d), lambda b, t: (b, t, 0)),
            pl.BlockSpec((n_exp, ROW_BLK), lambda b, t: (0, b * per_seq + t)),
        ],
        out_shape=[
            jax.ShapeDtypeStruct((bsz, l, d), F32),
            jax.ShapeDtypeStruct((bsz, l, d), BF16),
            jax.ShapeDtypeStruct((n_exp, bsz * l), F32),
        ],
        compiler_params=_cparams(("parallel", "parallel"), 48),
        name="outproj",
    )(a, fm_lat, fm_ctx, o_f, o_b, sg, w_out, x, mods, g.reshape(1, d), wr)


def _cumsum_lanes(m, out_ref, fin):
    n_exp, n = m.shape
    tri = (lax.broadcasted_iota(jnp.int32, (LANES, LANES), 0)
           <= lax.broadcasted_iota(jnp.int32, (LANES, LANES), 1)).astype(BF16)
    run = jnp.zeros((n_exp, 1), F32)
    befores = []
    for k in range(n // LANES):
        befores.append(run)
        sl = slice(k * LANES, (k + 1) * LANES)
        cnt = _dot(m[:, sl].astype(BF16), tri) + run
        out_ref[0, :, sl] = fin(cnt, sl)
        run = cnt[:, LANES - 1:LANES]
    befores.append(run)
    return befores


def _select(seg, cap, slot_off, sel_ref, tmp_ref):
    bits = pltpu.bitcast(seg, jnp.int32)
    n_exp = seg.shape[0]

    def body(it, t):
        tt = t | lax.shift_left(jnp.int32(1), 30 - it)
        cnt = jnp.sum(jnp.where(bits >= tt, 1.0, 0.0), axis=1, keepdims=True)
        return jnp.where(cnt >= cap, tt, t)

    t = lax.fori_loop(0, 31, body, jnp.zeros((n_exp, 1), jnp.int32))
    gt = bits > t
    eq = bits == t
    need = cap - jnp.sum(jnp.where(gt, 1.0, 0.0), axis=1, keepdims=True)
    eqf = jnp.where(eq, 1.0, 0.0)
    _cumsum_lanes(eqf, tmp_ref, lambda cnt, sl: cnt)
    take = eq & (tmp_ref[0] - eqf < need)
    mask = gt | take
    maskf = jnp.where(mask, 1.0, 0.0)
    return _cumsum_lanes(
        maskf, sel_ref,
        lambda cnt, sl: jnp.where(maskf[:, sl] > 0.5, cnt - 1.0 + slot_off, -1.0))


def _routing_kernel(aff_ref, sel_l, gat_l, sel_c, gat_c, st_l, tmp_l, tmp_c, *, n_lat, cap_l, cap_c):
    b = pl.program_id(0)
    a = aff_ref[...]
    lat = a[:, :n_lat]
    ctx = a[:, n_lat:]
    gat_l[0] = lat
    gat_c[0] = ctx
    befores = _select(lat, float(cap_l), 0.0, sel_l, tmp_l)
    _select(ctx, float(cap_c), (b * cap_c).astype(F32), sel_c, tmp_c)
    lane = lax.broadcasted_iota(jnp.int32, (a.shape[0], LANES), 1)
    st = jnp.zeros((a.shape[0], LANES), F32)
    per_blk = ROW_BLK // LANES
    for j in range(n_lat // ROW_BLK + 1):
        st = jnp.where(lane == j, befores[j * per_blk], st)
    st_l[0] = st


def _routing(aff_t, bsz, seq_len, n_lat, cap_l, cap_c):
    n_exp = aff_t.shape[0]
    n_ctx = seq_len - n_lat
    shp = lambda n: jax.ShapeDtypeStruct((bsz, n_exp, n), F32)
    spec = lambda n: pl.BlockSpec((1, n_exp, n), lambda b: (b, 0, 0))
    return pl.pallas_call(
        functools.partial(_routing_kernel, n_lat=n_lat, cap_l=cap_l, cap_c=cap_c),
        grid=(bsz,),
        in_specs=[pl.BlockSpec((n_exp, seq_len), lambda b: (0, b))],
        out_specs=[spec(n_lat), spec(n_lat), spec(n_ctx), spec(n_ctx), spec(LANES)],
        out_shape=[shp(n_lat), shp(n_lat), shp(n_ctx), shp(n_ctx), shp(LANES)],
        scratch_shapes=[pltpu.VMEM((1, n_exp, n_lat), F32), pltpu.VMEM((1, n_exp, n_ctx), F32)],
        compiler_params=_cparams(("parallel",), 32),
        name="routing",
    )(aff_t)


EXPERT_GROUP = 8
SLOT_ALIGN = 16
SLOT_WINDOW = 64


def _windows(starts_ref, base, stride, n, j, cap, win):
    w, rounds = [], jnp.int32(0)
    for k in range(n):
        s0 = starts_ref[base + k * stride + j]
        s1 = starts_ref[base + k * stride + j + 1]
        wk = jnp.minimum((s0 // SLOT_ALIGN) * SLOT_ALIGN, cap - win)
        w.append(wk)
        rounds = jnp.maximum(rounds, (s1 - wk + win - 1) // win)
    return w, rounds


def _gather_kernel(starts_ref, h_ref, sel_ref, aff_ref, xg_ref, gate_ref, *, n_tb, cap, win, n_exp):
    v, g, j = pl.program_id(0), pl.program_id(1), pl.program_id(2)
    ng = sel_ref.shape[2]
    base = (v * n_exp + g * ng) * (n_tb + 1)

    @pl.when(j == 0)
    def _():
        xg_ref[...] = jnp.zeros_like(xg_ref)
        gate_ref[...] = jnp.zeros_like(gate_ref)

    w, rounds = _windows(starts_ref, base, n_tb + 1, ng, j, cap, win)
    tb = h_ref.shape[1]
    row_i = lax.broadcasted_iota(jnp.int32, (win, tb), 0).astype(F32)

    def body(r, _):
        starts, pieces, gates = [], [], []
        for k in range(ng):
            lo = w[k] + r * win
            c = pl.multiple_of(jnp.minimum(lo, cap - win), SLOT_ALIGN)
            sel = sel_ref[0, 0, k:k + 1, :]
            rel = jnp.where(sel >= lo.astype(F32), sel, -1.0) - c.astype(F32)
            hit = rel == row_i
            pieces.append(jnp.where(hit, 1.0, 0.0).astype(BF16))
            gates.append(jnp.sum(jnp.where(hit, aff_ref[0, 0, k:k + 1, :], 0.0), axis=1,
                                 keepdims=True))
            starts.append(c)
        res = _dot(jnp.concatenate(pieces, axis=0), h_ref[0])
        for k in range(ng):
            rows = pl.ds(starts[k], win)
            xg_ref[k, rows, :] = (xg_ref[k, rows, :].astype(F32)
                                  + res[k * win:(k + 1) * win, :]).astype(BF16)
            gate_ref[k, rows, :] += gates[k]
        return 0

    lax.fori_loop(0, rounds, body, 0)


def _moe_gather(starts, h, sel, aff, n_tok, cap, win):
    nb, _, d = h.shape
    n_exp = sel.shape[1]
    ng = min(EXPERT_GROUP, n_exp)
    n_tb = n_tok // ROW_BLK
    rows = pl.BlockSpec((1, 1, ng, ROW_BLK), lambda v, g, j, s: (v, g, 0, j))
    grouped = lambda t: t.reshape(nb, n_exp // ng, ng, n_tok)
    return pl.pallas_call(
        functools.partial(_gather_kernel, n_tb=n_tb, cap=cap, win=win, n_exp=n_exp),
        grid_spec=pltpu.PrefetchScalarGridSpec(
            num_scalar_prefetch=1,
            grid=(nb, n_exp // ng, n_tb),
            in_specs=[pl.BlockSpec((1, ROW_BLK, d), lambda v, g, j, s: (v, j, 0)), rows, rows],
            out_specs=[pl.BlockSpec((ng, cap, d), lambda v, g, j, s: (g, v, 0)),
                       pl.BlockSpec((ng, cap, 1), lambda v, g, j, s: (g, v, 0))],
        ),
        out_shape=[jax.ShapeDtypeStruct((n_exp, nb * cap, d), BF16),
                   jax.ShapeDtypeStruct((n_exp, nb * cap, 1), F32)],
        compiler_params=_cparams(("parallel", "parallel", "arbitrary"), 56),
        name="moe_gather",
    )(starts, h, grouped(sel), grouped(aff))


def _ffn_kernel(xl_ref, xc_ref, wg_ref, wu_ref, al_ref, ac_ref, wgb_ref, wub_ref):
    r = pl.program_id(2)

    @pl.when(r == 0)
    def _():
        wgb_ref[...] = wg_ref[0, 0].astype(BF16)
        wub_ref[...] = wu_ref[0, 0].astype(BF16)

    def swiglu(x_ref, o_ref):
        x = x_ref[0]
        o_ref[0] = (_silu(_dot(x, wgb_ref[...])) * _dot(x, wub_ref[...])).astype(BF16)

    @pl.when(r == 0)
    def _():
        swiglu(xc_ref, ac_ref)

    @pl.when(r > 0)
    def _():
        swiglu(xl_ref, al_ref)


def _moe_ffn(xg_l, xg_c, w_gate, w_up, layer):
    n_exp, rows, d = xg_l.shape
    rows_c = xg_c.shape[1]
    ff = w_gate.shape[3]
    fh = _largest_tile(ff, 512, LANES)
    rblk = _largest_tile(rows, 1024)
    nls = rows // rblk
    prev = lambda r: jnp.maximum(r - 1, 0)
    wspec = pl.BlockSpec((1, 1, d, fh), lambda e, f, r: (layer, e, 0, f))
    return pl.pallas_call(
        _ffn_kernel,
        grid=(n_exp, ff // fh, nls + 1),
        in_specs=[pl.BlockSpec((1, rblk, d), lambda e, f, r: (e, prev(r), 0)),
                  pl.BlockSpec((1, rows_c, d), lambda e, f, r: (e, 0, 0)), wspec, wspec],
        out_specs=[pl.BlockSpec((1, rblk, fh), lambda e, f, r: (e, prev(r), f)),
                   pl.BlockSpec((1, rows_c, fh), lambda e, f, r: (e, 0, f))],
        out_shape=[jax.ShapeDtypeStruct((n_exp, rows, ff), BF16),
                   jax.ShapeDtypeStruct((n_exp, rows_c, ff), BF16)],
        scratch_shapes=[pltpu.VMEM((d, fh), BF16), pltpu.VMEM((d, fh), BF16)],
        compiler_params=_cparams(("parallel", "parallel", "arbitrary"), 48),
        name="moe_ffn",
    )(xg_l, xg_c, w_gate, w_up)


def _down_kernel(al_ref, ac_ref, gl_ref, gc_ref, wd_ref, yl_ref, yc_ref, wdb_ref):
    v = pl.program_id(1)

    @pl.when(v == 0)
    def _():
        wdb_ref[...] = wd_ref[0, 0].astype(BF16)

    @pl.when(v == 0)
    def _():
        yc_ref[0] = (_dot(ac_ref[0], wdb_ref[...]) * gc_ref[0]).astype(BF16)

    @pl.when(v > 0)
    def _():
        yl_ref[0] = (_dot(al_ref[0], wdb_ref[...]) * gl_ref[0]).astype(BF16)


def _moe_down(act_l, act_c, gate_l, gate_c, w_down, layer, cap):
    n_exp, rows, ff = act_l.shape
    rows_c = act_c.shape[1]
    d = w_down.shape[3]
    nb = rows // cap
    prev = lambda v: jnp.maximum(v - 1, 0)
    return pl.pallas_call(
        _down_kernel,
        grid=(n_exp, nb + 1),
        in_specs=[
            pl.BlockSpec((1, cap, ff), lambda e, v: (e, prev(v), 0)),
            pl.BlockSpec((1, rows_c, ff), lambda e, v: (e, 0, 0)),
            pl.BlockSpec((1, cap, 1), lambda e, v: (e, prev(v), 0)),
            pl.BlockSpec((1, rows_c, 1), lambda e, v: (e, 0, 0)),
            pl.BlockSpec((1, 1, ff, d), lambda e, v: (layer, e, 0, 0)),
        ],
        out_specs=[pl.BlockSpec((1, cap, d), lambda e, v: (prev(v), e, 0)),
                   pl.BlockSpec((1, rows_c, d), lambda e, v: (0, e, 0))],
        out_shape=[jax.ShapeDtypeStruct((nb, n_exp * cap, d), BF16),
                   jax.ShapeDtypeStruct((1, n_exp * rows_c, d), BF16)],
        scratch_shapes=[pltpu.VMEM((ff, d), BF16)],
        compiler_params=_cparams(("parallel", "arbitrary"), 48),
        name="moe_down",
    )(act_l, act_c, gate_l, gate_c, w_down)


def _combine_kernel(starts_ref, y_ref, sel_ref, out_ref, st_ref, *, n_tb, cap, win, n_exp):
    v, j = pl.program_id(0), pl.program_id(2)
    base = v * n_exp * (n_tb + 1)
    w, rounds = _windows(starts_ref, base, n_tb + 1, n_exp, j, cap, win)
    cols = lax.broadcasted_iota(jnp.int32, (n_exp, n_exp * win), 1)
    spread = jnp.where(cols // win == lax.broadcasted_iota(jnp.int32, (n_exp, n_exp * win), 0),
                       1.0, 0.0).astype(BF16)
    slot_in_win = (lax.broadcasted_iota(jnp.int32, (1, n_exp * win), 1) % win + 1).astype(F32)
    exp_lane = lax.broadcasted_iota(jnp.int32, (1, n_exp), 1)

    def one_round(r):
        lo_vec = jnp.zeros((1, n_exp), F32)
        c_vec = jnp.zeros((1, n_exp), F32)
        for e in range(n_exp):
            lo = w[e] + r * win
            c = pl.multiple_of(jnp.minimum(lo, cap - win), SLOT_ALIGN)
            src = pl.multiple_of(e * cap + c, SLOT_ALIGN)
            st_ref[e * win:(e + 1) * win, :] = y_ref[0, pl.ds(src, win), :]
            lo_vec = jnp.where(exp_lane == e, lo.astype(F32), lo_vec)
            c_vec = jnp.where(exp_lane == e, c.astype(F32), c_vec)
        sel = sel_ref[0]
        rel = jnp.where(sel >= lo_vec, sel, -1.0) - c_vec
        rel1 = jnp.where((rel >= 0.0) & (rel < float(win)), rel + 1.0, 0.0)
        hit = _dot(rel1.astype(BF16), spread) == slot_in_win
        return _dot(jnp.where(hit, 1.0, 0.0).astype(BF16), st_ref[...])

    out_ref[0] = one_round(0)

    def body(r, _):
        out_ref[0] += one_round(r)
        return 0

    lax.fori_loop(1, rounds, body, 0)


def _moe_combine(starts, y, sel_t, n_tok, cap, win):
    nb, _, d = y.shape
    n_exp = sel_t.shape[2]
    n_tb = n_tok // ROW_BLK
    pw = _largest_tile(d, 1024, LANES)
    return pl.pallas_call(
        functools.partial(_combine_kernel, n_tb=n_tb, cap=cap, win=win, n_exp=n_exp),
        grid_spec=pltpu.PrefetchScalarGridSpec(
            num_scalar_prefetch=1,
            grid=(nb, d // pw, n_tb),
            in_specs=[pl.BlockSpec((1, n_exp * cap, pw), lambda v, p, j, s: (v, 0, p)),
                      pl.BlockSpec((1, ROW_BLK, n_exp), lambda v, p, j, s: (v, j, 0))],
            out_specs=pl.BlockSpec((1, ROW_BLK, pw), lambda v, p, j, s: (v, j, p)),
            scratch_shapes=[pltpu.VMEM((n_exp * win, pw), BF16)],
        ),
        out_shape=jax.ShapeDtypeStruct((nb, n_tok, d), F32),
        compiler_params=_cparams(("parallel", "parallel", "arbitrary"), 48),
        name="moe_combine",
    )(starts, y, sel_t)


def _rope_tables(pos_groups, half):
    freqs = ROPE_BASE ** (-jnp.arange(half, dtype=F32) / half)
    cos, sin = [], []
    for pos in pos_groups:
        ang = pos.astype(F32)[:, None] * freqs[None, :]
        c, s = jnp.cos(ang), jnp.sin(ang)
        cos += [c, c]
        sin += [-s, s]
    return jnp.stack([jnp.concatenate(t, axis=1) for t in (cos, sin)])


def _attn_tables(n_lat, n_ctx):
    s = jnp.arange(n_lat)
    tab = _rope_tables([s // GRID_W, s % GRID_W], HEAD_DIM // 4)
    ident = jnp.stack([jnp.ones((n_ctx, HEAD_DIM), F32), jnp.zeros((n_ctx, HEAD_DIM), F32)])
    return jnp.concatenate([tab, ident], axis=1)


def _ret_tables(n_lat, n_ctx):
    s = jnp.arange(n_lat)
    t = jnp.arange(n_ctx)
    fwd = jnp.concatenate([n_ctx + s, t])
    bwd = jnp.concatenate([n_ctx + (n_lat - 1 - s), n_ctx - 1 - t])
    return jnp.concatenate([_rope_tables([fwd], HEAD_DIM // 2),
                            _rope_tables([bwd], HEAD_DIM // 2)], axis=0)


def kernel(x, c, ctx, c_ctx, w_ada, b_ada, norm_mix, norm_ffn, w_in, q_norm, k_norm, ret_log_decay,
           w_out, w_router, w_gate, w_up, w_down):
    bsz, n_lat, d = x.shape
    n_ctx = ctx.shape[1]
    seq_len = n_lat + n_ctx
    depth = w_ada.shape[0]
    n_exp = w_router.shape[2]
    n_lat_blk = n_lat // ROW_BLK
    assert n_lat % ROW_BLK == 0 and n_ctx == ROW_BLK and bsz + 1 <= 8
    cap_l = CAPACITY_FACTOR * n_lat // n_exp
    cap_c = CAPACITY_FACTOR * n_ctx // n_exp

    cond8 = jnp.zeros((8, d), F32).at[:bsz].set(c).at[bsz].set(c_ctx)
    mods = _adaln(cond8, w_ada, b_ada).reshape(depth, 8, 6, d)

    attn_tab = _attn_tables(n_lat, n_ctx)
    ret_tab = _ret_tables(n_lat, n_ctx)
    wc = _chan_mats()
    two_stage = n_lat % (FFT_COLS * 2 * SUBLANES) == 0
    lat_mats = dict(fft=_fft_mats(n_lat)) if two_stage else dict(dft=_dft_mats(n_lat))
    dft_c = _dft_mats(n_ctx)

    o_q, o_k = 0, ATTN_WIDTH
    o_v = o_k + KV_WIDTH
    o_f = o_v + KV_WIDTH
    o_rq = o_f + FOURIER_WIDTH
    o_rk = o_rq + RET_WIDTH
    o_rv = o_rk + RET_WIDTH
    o_g = o_rv + RET_WIDTH
    o_end = o_g + 2 * RET_WIDTH
    q_scale = HEAD_DIM ** -0.5 * math.log2(math.e)

    sb_c = bsz * cap_c
    win_l = min(cap_l, SLOT_WINDOW)
    win_c = cap_c
    starts_c = jnp.tile(jnp.arange(bsz + 1, dtype=jnp.int32) * cap_c, n_exp)

    xs = jnp.concatenate([x, ctx], axis=1)
    delta_lat = delta_ctx = None
    for l in range(depth):
        xs, h = _norm(xs, mods[l], norm_mix[l], n_lat_blk, delta_lat, delta_ctx,
                      mods[l - 1] if l else None)
        h2d = h.reshape(bsz * seq_len, d)
        wl = w_in[l].astype(BF16)
        q = _proj(h2d, wl[:, o_q:o_k], seq_len, "qk", gain=q_norm[l], tab=attn_tab,
                  n_heads=ATTN_HEADS, scale=q_scale, half=HEAD_DIM // 4)
        k = _proj(h2d, wl[:, o_k:o_v], seq_len, "qk", gain=k_norm[l], tab=attn_tab,
                  n_heads=ATTN_KV_HEADS, scale=1.0, half=HEAD_DIM // 4)
        v, f = _proj(h2d, wl[:, o_v:o_rq], seq_len, "split", split_at=KV_WIDTH)
        rq = _proj(h2d, wl[:, o_rq:o_rk], seq_len, "ret", tab=ret_tab, n_heads=RET_HEADS,
                   scale=1.0, half=HEAD_DIM // 2)
        rk = _proj(h2d, wl[:, o_rk:o_rv], seq_len, "ret", tab=ret_tab, n_heads=RET_HEADS,
                   scale=HEAD_DIM ** -0.5, half=HEAD_DIM // 2)
        rv = _proj(h2d, wl[:, o_rv:o_g], seq_len, "plain")
        sg = _proj(h2d, wl[:, o_g:o_end], seq_len, "silu")

        a = _attention(q, k, v, bsz, seq_len, n_lat)
        fm_lat = _fourier(f, bsz, seq_len, 0, n_lat, wc, **lat_mats)
        fm_ctx = _fourier(f, bsz, seq_len, n_lat, n_ctx, wc, dft=dft_c)
        o_f, o_b = _retention(rq, rk, rv, ret_log_decay[l], bsz, seq_len, n_lat)

        xs, h2, aff_t = _outproj(a, fm_lat, fm_ctx, o_f, o_b, sg, w_out[l].astype(BF16), xs, mods[l],
                                 norm_ffn[l], w_router[l], n_lat_blk)

        sel_l, gat_l, sel_c, gat_c, st_l = _routing(aff_t, bsz, seq_len, n_lat, cap_l, cap_c)
        starts_l = st_l[:, :, :n_lat_blk + 1].astype(jnp.int32).reshape(-1)
        merge = lambda t: t.transpose(1, 0, 2).reshape(1, n_exp, bsz * n_ctx)
        tok_major = lambda t: t.transpose(0, 2, 1)
        sel_c, gat_c = merge(sel_c), merge(gat_c)
        h_ctx = h2[:, n_lat:, :].reshape(1, bsz * n_ctx, d)

        xg_l, gate_l = _moe_gather(starts_l, h2, sel_l, gat_l, n_lat, cap_l, win_l)
        xg_c, gate_c = _moe_gather(starts_c, h_ctx, sel_c, gat_c, bsz * n_ctx, sb_c, win_c)
        act_l, act_c = _moe_ffn(xg_l, xg_c, w_gate, w_up, l)
        y_l, y_c = _moe_down(act_l, act_c, gate_l, gate_c, w_down, l, cap_l)
        delta_lat = _moe_combine(starts_l, y_l, tok_major(sel_l), n_lat, cap_l, win_l)
        delta_ctx = _moe_combine(starts_c, y_c, tok_major(sel_c), bsz * n_ctx, sb_c,
                                 win_c).reshape(bsz * n_ctx, d)

    out, _ = _norm(xs, None, None, n_lat_blk, delta_lat, delta_ctx, mods[depth - 1],
                   want_h=False, lat_only=True)
    return out
```

```python
import functools
import math

import jax
import jax.numpy as jnp
from jax import lax
from jax.experimental import pallas as pl
from jax.experimental.pallas import tpu as pltpu

HEAD_DIM = 128
ATTN_HEADS = 8
ATTN_KV_HEADS = 2
ATTN_GROUP = ATTN_HEADS // ATTN_KV_HEADS
FOURIER_GROUPS = 4
RET_HEADS = 4
RET_CHUNK = 128
GRID_W = 64
ROPE_BASE = 10000.0
EPS = 1e-6
CAPACITY_FACTOR = 2

ATTN_WIDTH = ATTN_HEADS * HEAD_DIM
KV_WIDTH = ATTN_KV_HEADS * HEAD_DIM
FOURIER_WIDTH = FOURIER_GROUPS * HEAD_DIM
RET_WIDTH = RET_HEADS * HEAD_DIM

ROW_BLK = 256
LANES = 128
MIB = 1024 * 1024

F32 = jnp.float32
BF16 = jnp.bfloat16


def _cparams(sem, vmem_mib):
    return pltpu.CompilerParams(dimension_semantics=sem, vmem_limit_bytes=vmem_mib * MIB)


def _largest_tile(n, cap, mult=8):
    best = mult
    for t in range(mult, min(n, cap) + 1, mult):
        if n % t == 0:
            best = t
    return best


def _split(a):
    hi = a.astype(BF16)
    lo = (a - hi.astype(F32)).astype(BF16)
    return hi, lo


def _dot(a, b):
    return jnp.dot(a, b, preferred_element_type=F32)


def _dot_nt(a, b):
    return lax.dot_general(a, b, (((1,), (1,)), ((), ())), preferred_element_type=F32)


def _dot_tn(a, b):
    return lax.dot_general(a, b, (((0,), (0,)), ((), ())), preferred_element_type=F32)


def _silu(a):
    return a / (1.0 + jnp.exp(-a))


def _adaln_kernel(c_ref, w_ref, b_ref, o_ref):
    s = _silu(c_ref[...])
    sh, sl = _split(s)
    wh, wl = _split(w_ref[0])
    o_ref[0] = _dot(sh, wh) + (_dot(sh, wl) + _dot(sl, wh)) + b_ref[0]


def _adaln(cond8, w_ada, b_ada):
    depth, d, n = w_ada.shape
    tn = _largest_tile(n, 768, LANES)
    return pl.pallas_call(
        _adaln_kernel,
        grid=(depth, n // tn),
        in_specs=[
            pl.BlockSpec((8, d), lambda l, j: (0, 0)),
            pl.BlockSpec((1, d, tn), lambda l, j: (l, 0, j)),
            pl.BlockSpec((1, 1, tn), lambda l, j: (l, 0, j)),
        ],
        out_specs=pl.BlockSpec((1, 8, tn), lambda l, j: (l, 0, j)),
        out_shape=jax.ShapeDtypeStruct((depth, 8, n), F32),
        compiler_params=_cparams(("parallel", "parallel"), 48),
        name="adaln",
    )(cond8, w_ada, b_ada.reshape(depth, 1, n))


def _rms_mod(x, g, shift, scale):
    y = x * lax.rsqrt(jnp.mean(x * x, axis=-1, keepdims=True) + EPS)
    return (y * g) * (1.0 + scale) + shift


def _norm_kernel(*refs, has_delta, want_h, n_lat_blk):
    refs = list(refs)
    x_ref = refs.pop(0)
    x = x_ref[0]
    if has_delta:
        dl_ref, dc_ref, pm_ref = refs.pop(0), refs.pop(0), refs.pop(0)
        is_ctx = pl.program_id(1) >= n_lat_blk
        delta = jnp.where(is_ctx, dc_ref[...], dl_ref[0])
        x = x + pm_ref[0, 5:6, :] * delta
    if want_h:
        m_ref, g_ref = refs.pop(0), refs.pop(0)
    if has_delta:
        xo_ref = refs.pop(0)
        xo_ref[0] = x
    if want_h:
        h_ref = refs.pop(0)
        h_ref[0] = _rms_mod(x, g_ref[...], m_ref[0, 0:1, :], m_ref[0, 1:2, :]).astype(BF16)


def _norm(x, mods, g, n_lat_blk, delta_lat=None, delta_ctx=None, prev_mods=None,
          want_h=True, lat_only=False):
    bsz, l, d = x.shape
    n_blk = n_lat_blk if lat_only else l // ROW_BLK
    has_delta = delta_lat is not None

    def mod_row(b, t):
        return (jnp.where(t >= n_lat_blk, bsz, b), 0, 0)

    xspec = pl.BlockSpec((1, ROW_BLK, d), lambda b, t: (b, t, 0))
    in_specs, args = [xspec], [x]
    if has_delta:
        in_specs += [
            pl.BlockSpec((1, ROW_BLK, d), lambda b, t: (b, jnp.minimum(t, n_lat_blk - 1), 0)),
            pl.BlockSpec((ROW_BLK, d), lambda b, t: (b, 0)),
            pl.BlockSpec((1, 6, d), mod_row),
        ]
        args += [delta_lat, delta_ctx, prev_mods]
    if want_h:
        in_specs += [pl.BlockSpec((1, 6, d), mod_row), pl.BlockSpec((1, d), lambda b, t: (0, 0))]
        args += [mods, g.reshape(1, d)]
    out_specs, out_shape = [], []
    rows = n_blk * ROW_BLK
    if has_delta:
        out_specs.append(xspec)
        out_shape.append(jax.ShapeDtypeStruct((bsz, rows, d), F32))
    if want_h:
        out_specs.append(xspec)
        out_shape.append(jax.ShapeDtypeStruct((bsz, rows, d), BF16))
    outs = pl.pallas_call(
        functools.partial(_norm_kernel, has_delta=has_delta, want_h=want_h, n_lat_blk=n_lat_blk),
        grid=(bsz, n_blk),
        in_specs=in_specs,
        out_specs=out_specs,
        out_shape=out_shape,
        compiler_params=_cparams(("parallel", "parallel"), 32),
        name="norm",
    )(*args)
    outs = list(outs)
    x_new = outs.pop(0) if has_delta else x
    h = outs.pop(0) if want_h else None
    return x_new, h


def _dot_wide(a, m):
    hi, lo = _split(a)
    return _dot(hi, m) + _dot(lo, m)


def _partner_matrix(half):
    src = lax.broadcasted_iota(jnp.int32, (LANES, LANES), 0)
    dst = lax.broadcasted_iota(jnp.int32, (LANES, LANES), 1)
    mate = jnp.where(dst % (2 * half) < half, dst + half, dst - half)
    return jnp.where(src == mate, 1.0, 0.0).astype(BF16)


def _proj_kernel(*refs, flavor, n_heads, scale, half, split_at, n_sub):
    h_ref, w_ref = refs[0], refs[1]
    if n_sub > 1:
        sub = h_ref.shape[0] // n_sub
        accs = [_dot(h_ref[s * sub:(s + 1) * sub, :], w_ref[...]) for s in range(n_sub)]
        for s, acc in enumerate(accs):
            _proj_epilogue(acc, slice(s * sub, (s + 1) * sub), refs, flavor, n_heads, scale, half)
        return
    acc = _dot(h_ref[...], w_ref[...])
    if flavor in ("qk", "ret"):
        _proj_epilogue(acc, slice(None), refs, flavor, n_heads, scale, half)
    elif flavor == "plain":
        refs[2][...] = acc.astype(BF16)
    elif flavor == "split":
        refs[2][...] = acc[:, :split_at].astype(BF16)
        refs[3][...] = acc[:, split_at:].astype(BF16)
    elif flavor == "silu":
        refs[2][...] = _silu(acc).astype(BF16)
    else:
        raise ValueError(flavor)


def _proj_epilogue(acc, rows, refs, flavor, n_heads, scale, half):
    partner = _partner_matrix(half)
    if flavor == "qk":
        gain_ref, tab_ref, o_ref = refs[2], refs[3], refs[4]
        ones = jnp.ones((LANES, LANES), BF16)
        for hd in range(n_heads):
            sl = slice(hd * HEAD_DIM, (hd + 1) * HEAD_DIM)
            z = acc[:, sl]
            ms = _dot_wide(z * z, ones) * (1.0 / HEAD_DIM)
            z = (z * lax.rsqrt(ms + EPS)) * gain_ref[...]
            z = z * tab_ref[0, rows, :] + _dot_wide(z, partner) * tab_ref[1, rows, :]
            o_ref[rows, sl] = (z * scale).astype(BF16)
    else:
        tab_ref, o_ref = refs[2], refs[3]
        width = n_heads * HEAD_DIM
        for hd in range(n_heads):
            sl = slice(hd * HEAD_DIM, (hd + 1) * HEAD_DIM)
            z = acc[:, sl] * scale
            pz = _dot_wide(z, partner)
            o_ref[rows, sl] = (z * tab_ref[0, rows, :] + pz * tab_ref[1, rows, :]).astype(BF16)
            o_ref[rows, width + hd * HEAD_DIM: width + (hd + 1) * HEAD_DIM] = (
                z * tab_ref[2, rows, :] + pz * tab_ref[3, rows, :]).astype(BF16)


def _proj(h2d, w, seq_len, flavor, *, gain=None, tab=None, n_heads=0, scale=1.0, half=0,
          split_at=0):
    t_rows, d = h2d.shape
    n = w.shape[1]
    n_sub = 2 if flavor in ("qk", "ret") else 1
    tm = _largest_tile(seq_len, 1088 * n_sub, 8 * n_sub)
    per_seq = seq_len // tm
    in_specs = [pl.BlockSpec((tm, d), lambda i: (i, 0)), pl.BlockSpec((d, n), lambda i: (0, 0))]
    args = [h2d, w]
    if flavor == "qk":
        in_specs.append(pl.BlockSpec((1, HEAD_DIM), lambda i: (0, 0)))
        args.append(gain.reshape(1, HEAD_DIM))
    if flavor in ("qk", "ret"):
        ntab = tab.shape[0]
        in_specs.append(pl.BlockSpec((ntab, tm, HEAD_DIM), lambda i: (0, i % per_seq, 0)))
        args.append(tab)
    if flavor == "split":
        widths = [split_at, n - split_at]
    elif flavor == "ret":
        widths = [2 * n]
    else:
        widths = [n]
    out_specs = [pl.BlockSpec((tm, wd), lambda i: (i, 0)) for wd in widths]
    out_shape = [jax.ShapeDtypeStruct((t_rows, wd), BF16) for wd in widths]
    outs = pl.pallas_call(
        functools.partial(_proj_kernel, flavor=flavor, n_heads=n_heads, scale=scale, half=half,
                          split_at=split_at, n_sub=n_sub),
        grid=(t_rows // tm,),
        in_specs=in_specs,
        out_specs=out_specs,
        out_shape=out_shape,
        compiler_params=_cparams(("parallel",), 56),
        name="proj_" + flavor,
    )(*args)
    return outs if len(outs) > 1 else outs[0]


SUBLANES = 8


SAFE_SHIFT = 50.0


def _lane_fold(t, op):
    return functools.reduce(op, [t[:, i * LANES:(i + 1) * LANES]
                                 for i in range(t.shape[1] // LANES)])


def _fold_rows(t, op):
    return functools.reduce(op, [t[r:r + SUBLANES, :] for r in range(0, t.shape[0], SUBLANES)])


def _attn_group_bounded(q_all, m_row, k_ref, vt_ref, pt_ref, k_lo, n_keys, tk):
    lp = None
    for c in range(n_keys // tk):
        st = _dot_nt(k_ref[k_lo + c * tk:k_lo + (c + 1) * tk, :], q_all)
        p = jnp.exp2(st - m_row)
        part = _fold_rows(p, jnp.add)
        lp = part if lp is None else lp + part
        pt_ref[c * tk:(c + 1) * tk, :] = p.astype(BF16)
    l = jnp.sum(lp, axis=0, keepdims=True)
    return _dot(vt_ref[:, k_lo:k_lo + n_keys], pt_ref[:n_keys, :]) / l


def _attn_head_exact(q, k_ref, v_ref, s_ref, p_ref, k_lo, n_keys, tk):
    s_ref[:, :n_keys] = _dot_nt(q, k_ref[k_lo:k_lo + n_keys, :])
    n_chunks = n_keys // tk
    mp = None
    for c in range(n_chunks):
        part = _lane_fold(s_ref[:, c * tk:(c + 1) * tk], jnp.maximum)
        mp = part if mp is None else jnp.maximum(mp, part)
    m = jnp.max(mp, axis=-1, keepdims=True)
    lp = None
    for c in range(n_chunks):
        p = jnp.exp2(s_ref[:, c * tk:(c + 1) * tk] - m)
        part = _lane_fold(p, jnp.add)
        lp = part if lp is None else lp + part
        p_ref[:, c * tk:(c + 1) * tk] = p.astype(BF16)
    l = jnp.sum(lp, axis=-1, keepdims=True)
    return _dot(p_ref[:, :n_keys], v_ref[k_lo:k_lo + n_keys, :]) / l


def _attn_kernel(q_ref, k_ref, v_ref, o_ref, s_ref, p_ref, pt_ref, vt_ref, kmax_ref, *, tk, n_lat,
                 n_ctx, n_lat_qblk):
    i = pl.program_id(2)
    is_ctx = i >= n_lat_qblk

    @pl.when(i == 0)
    def _():
        kk = k_ref[...].astype(F32)
        kn2 = jnp.max(jnp.sum(kk * kk, axis=-1, keepdims=True), axis=0, keepdims=True)
        kmax_ref[...] = jnp.broadcast_to(jnp.sqrt(kn2), kmax_ref.shape)
        vt_ref[...] = v_ref[...].astype(F32).T.astype(BF16)

    heads = [slice(hd * HEAD_DIM, (hd + 1) * HEAD_DIM) for hd in range(ATTN_GROUP)]
    tq = q_ref.shape[0]
    q_all = jnp.concatenate([q_ref[:, sl] for sl in heads], axis=0)
    qf = q_all.astype(F32)
    qn2 = _dot_nt(jnp.ones((SUBLANES, HEAD_DIM), BF16), (qf * qf).astype(BF16))[0:1, :]
    m_row = jnp.sqrt(qn2) * kmax_ref[0:1, 0:1] * (1.0 + 2.0 ** -6)
    worst = jnp.max(m_row)
    bounded_ok = worst <= SAFE_SHIFT

    def run(k_lo, n_keys, bounded):
        if bounded:
            ot = _attn_group_bounded(q_all, m_row, k_ref, vt_ref, pt_ref, k_lo, n_keys, tk)
            outs = [ot[:, hd * tq:(hd + 1) * tq].T for hd in range(len(heads))]
        else:
            outs = [_attn_head_exact(q_ref[:, sl], k_ref, v_ref, s_ref.at[hd], p_ref.at[hd], k_lo,
                                     n_keys, tk) for hd, sl in enumerate(heads)]
        for o, sl in zip(outs, heads):
            o_ref[:, sl] = o.astype(BF16)

    for ctx_case, (k_lo, n_keys) in ((False, (0, n_lat + n_ctx)), (True, (n_lat, n_ctx))):
        for bounded in (True, False):
            @pl.when((is_ctx == ctx_case) & (bounded_ok == bounded))
            def _(k_lo=k_lo, n_keys=n_keys, bounded=bounded):
                run(k_lo, n_keys, bounded)


def _attention(q, k, v, bsz, seq_len, n_lat):
    tq = ROW_BLK
    tk = ROW_BLK
    per_seq = seq_len // tq
    gw = ATTN_GROUP * HEAD_DIM
    return pl.pallas_call(
        functools.partial(_attn_kernel, tk=tk, n_lat=n_lat, n_ctx=seq_len - n_lat,
                          n_lat_qblk=n_lat // tq),
        grid=(bsz, ATTN_KV_HEADS, per_seq),
        in_specs=[
            pl.BlockSpec((tq, gw), lambda b, g, i: (b * per_seq + i, g)),
            pl.BlockSpec((seq_len, HEAD_DIM), lambda b, g, i: (b, g)),
            pl.BlockSpec((seq_len, HEAD_DIM), lambda b, g, i: (b, g)),
        ],
        out_specs=pl.BlockSpec((tq, gw), lambda b, g, i: (b * per_seq + i, g)),
        out_shape=jax.ShapeDtypeStruct(q.shape, BF16),
        scratch_shapes=[pltpu.VMEM((ATTN_GROUP, tq, seq_len), F32),
                        pltpu.VMEM((ATTN_GROUP, tq, seq_len), BF16),
                        pltpu.VMEM((seq_len, ATTN_GROUP * tq), BF16),
                        pltpu.VMEM((HEAD_DIM, seq_len), BF16),
                        pltpu.VMEM((SUBLANES, LANES), F32)],
        compiler_params=_cparams(("parallel", "parallel", "arbitrary"), 58),
        name="attention",
    )(q, k, v)


def _fourier_a_kernel(f_ref, wc_ref, o_ref):
    g = _dot(f_ref[...], wc_ref[...])
    wdt = f_ref.shape[1]
    o_ref[0, 0] = g[:, :wdt].astype(o_ref.dtype)
    o_ref[0, 1] = g[:, wdt:].astype(o_ref.dtype)


def _fourier_b_kernel(m_ref, g_ref, o_ref):
    o_ref[0] = _dot(m_ref[...], g_ref[0]).astype(BF16)


FFT_COLS = 64
FFT_PER_STEP = SUBLANES


def _fft1_kernel(g_ref, m_ref, tc_ref, ts_ref, o_ref):
    width = o_ref.shape[4]
    n1 = g_ref.shape[2]
    for s in range(g_ref.shape[3]):
        g = jnp.concatenate([g_ref[0, 0, :, s, :], g_ref[0, 1, :, s, :]], axis=0).astype(BF16)
        a = _dot(m_ref[...], g)
        ar, ai = a[:n1], a[n1:]
        tc = tc_ref[0, :, s * LANES:(s + 1) * LANES]
        ts = ts_ref[0, :, s * LANES:(s + 1) * LANES]
        lanes = [slice(q * LANES, (q + 1) * LANES) for q in range(width // LANES)]
        o_ref[0, 0, :, s, :] = jnp.concatenate([ar[:, q] * tc + ai[:, q] * ts for q in lanes], axis=1)
        o_ref[0, 1, :, s, :] = jnp.concatenate([ai[:, q] * tc - ar[:, q] * ts for q in lanes], axis=1)


def _fft2_kernel(b_ref, m_ref, o_ref):
    for s in range(b_ref.shape[2]):
        b = jnp.concatenate([b_ref[0, 0, s], b_ref[0, 1, s]], axis=0).astype(BF16)
        o_ref[0, :, s, :] = _dot(m_ref[...], b)


def _fourier_two_stage(g, fft):
    m1, tc, ts, m3 = fft
    bsz, _, n, width = g.shape
    n2 = m3.shape[0]
    n1 = n // n2
    per = FFT_PER_STEP
    b = pl.pallas_call(
        _fft1_kernel,
        grid=(bsz, n2 // per),
        in_specs=[
            pl.BlockSpec((1, 2, n1, per, width), lambda b, k: (b, 0, 0, k, 0)),
            pl.BlockSpec(m1.shape, lambda b, k: (0, 0)),
            pl.BlockSpec((1, n1, per * LANES), lambda b, k: (k, 0, 0)),
            pl.BlockSpec((1, n1, per * LANES), lambda b, k: (k, 0, 0)),
        ],
        out_specs=pl.BlockSpec((1, 2, n1, per, width), lambda b, k: (b, 0, 0, k, 0)),
        out_shape=jax.ShapeDtypeStruct((bsz, 2, n1, n2, width), F32),
        compiler_params=_cparams(("parallel", "parallel"), 32),
        name="fourier_fft1",
    )(g.reshape(bsz, 2, n1, n2, width), m1, tc, ts)
    out = pl.pallas_call(
        _fft2_kernel,
        grid=(bsz, n1 // per),
        in_specs=[
            pl.BlockSpec((1, 2, per, n2, width), lambda b, k: (b, 0, k, 0, 0)),
            pl.BlockSpec(m3.shape, lambda b, k: (0, 0)),
        ],
        out_specs=pl.BlockSpec((1, n2, per, width), lambda b, k: (b, 0, k, 0)),
        out_shape=jax.ShapeDtypeStruct((bsz, n2, n1, width), F32),
        compiler_params=_cparams(("parallel", "parallel"), 32),
        name="fourier_fft2",
    )(b, m3)
    return out.reshape(bsz, n, width)


def _fft_mats(n):
    n2 = FFT_COLS
    n1 = n // n2
    per = FFT_PER_STEP

    def cs(rows, cols, period):
        ang = ((rows[:, None] * cols[None, :]) % period).astype(F32) * (2.0 * math.pi / period)
        return jnp.cos(ang), jnp.sin(ang)

    i1, i2 = jnp.arange(n1, dtype=jnp.int32), jnp.arange(n2, dtype=jnp.int32)
    c1, s1 = cs(i1, i1, n1)
    m1 = (jnp.block([[c1, s1], [-s1, c1]]) * n1 ** -0.5).astype(BF16)
    c3, s3 = cs(i2, i2, n2)
    m3 = (jnp.concatenate([c3, s3], axis=1) * n2 ** -0.5).astype(BF16)
    tc, ts = cs(i1, i2, n)
    expand = lambda t: jnp.repeat(t.T.reshape(n2 // per, per, n1).transpose(0, 2, 1), LANES,
                                  axis=2)
    return m1, expand(tc), expand(ts), m3


def _fourier(f2d, bsz, seq_len, row_off, n, wc, dft=None, fft=None):
    per_seq = seq_len // ROW_BLK
    off_blk = row_off // ROW_BLK
    nb = n // ROW_BLK
    width = f2d.shape[1]
    g = pl.pallas_call(
        _fourier_a_kernel,
        grid=(bsz, nb),
        in_specs=[
            pl.BlockSpec((ROW_BLK, width), lambda b, t: (b * per_seq + off_blk + t, 0)),
            pl.BlockSpec((width, 2 * width), lambda b, t: (0, 0)),
        ],
        out_specs=pl.BlockSpec((1, 2, ROW_BLK, width), lambda b, t: (b, 0, t, 0)),
        out_shape=jax.ShapeDtypeStruct((bsz, 2, n, width), BF16 if fft is None else F32),
        compiler_params=_cparams(("parallel", "parallel"), 32),
        name="fourier_chan",
    )(f2d, wc)
    if fft is not None:
        return _fourier_two_stage(g, fft)
    g = g.reshape(bsz, 2 * n, width)
    return pl.pallas_call(
        _fourier_b_kernel,
        grid=(nb, bsz),
        in_specs=[
            pl.BlockSpec((ROW_BLK, 2 * n), lambda i, b: (i, 0)),
            pl.BlockSpec((1, 2 * n, width), lambda i, b: (b, 0, 0)),
        ],
        out_specs=pl.BlockSpec((1, ROW_BLK, width), lambda i, b: (b, i, 0)),
        out_shape=jax.ShapeDtypeStruct((bsz, n, width), BF16),
        compiler_params=_cparams(("parallel", "parallel"), 48),
        name="fourier_pos",
    )(dft, g)


def _dft_mats(n):
    i = jnp.arange(n, dtype=jnp.int32)
    prod = (i[:, None] * i[None, :]) % n
    ang = prod.astype(F32) * (2.0 * math.pi / n)
    s = n ** -0.5
    return jnp.concatenate([jnp.cos(ang) * s, jnp.sin(ang) * s], axis=1).astype(BF16)


def _chan_mats():
    i = jnp.arange(HEAD_DIM, dtype=jnp.int32)
    ang = ((i[:, None] * i[None, :]) % HEAD_DIM).astype(F32) * (2.0 * math.pi / HEAD_DIM)
    s = HEAD_DIM ** -0.5
    eye = jnp.eye(FOURIER_GROUPS, dtype=F32)
    c = jnp.kron(eye, jnp.cos(ang) * s)
    sn = jnp.kron(eye, -jnp.sin(ang) * s)
    return jnp.concatenate([c, sn], axis=1).astype(BF16)


RET_HEADS_PER_STEP = 2


def _ret_kernel(ld_ref, qf_ref, qb_ref, kf_ref, kb_ref, v_ref, of_ref, ob_ref, state_ref,
                *, n_lat_chunks, n_ctx_chunks):
    c = RET_CHUNK
    n_all = n_lat_chunks + n_ctx_chunks
    h0 = pl.program_id(1) * RET_HEADS_PER_STEP
    ii = lax.broadcasted_iota(jnp.int32, (c, c), 0).astype(F32)
    jj = lax.broadcasted_iota(jnp.int32, (c, c), 1).astype(F32)
    ri = lax.broadcasted_iota(jnp.int32, (c, 1), 0).astype(F32)
    state_ref[...] = jnp.zeros_like(state_ref)

    chains = []
    for hh in range(RET_HEADS_PER_STEP):
        sl = slice(hh * HEAD_DIM, (hh + 1) * HEAD_DIM)
        for d, (q_ref, k_ref, o_ref) in enumerate(((qf_ref, kf_ref, of_ref),
                                                   (qb_ref, kb_ref, ob_ref))):
            lg = ld_ref[d, h0 + hh]
            diff = ii - jj if d == 0 else jj - ii
            intra = jnp.where(diff >= 0, jnp.exp(lg * jnp.maximum(diff, 0.0)), 0.0)
            q_dec = jnp.exp(lg * (ri + 1.0 if d == 0 else c - ri))
            k_dec = jnp.exp(lg * (c - 1.0 - ri if d == 0 else ri))
            c_dec = jnp.exp(jnp.full((1, HEAD_DIM), lg * c, F32))
            chains.append((len(chains), q_ref, k_ref, o_ref, sl, (intra, q_dec, k_dec, c_dec), d))

    def body(s, _):
        in_ctx = s < n_ctx_chunks
        for slot, q_ref, k_ref, o_ref, sl, (intra, q_dec, k_dec, c_dec), d in chains:
            if d == 0:
                chunk = jnp.where(in_ctx, n_lat_chunks + s, s - n_ctx_chunks)
            else:
                chunk = n_all - 1 - s
            off = pl.multiple_of(chunk * c, c)
            q = q_ref[pl.ds(off, c), sl]
            k = k_ref[pl.ds(off, c), sl]
            v = v_ref[pl.ds(off, c), sl]
            st = state_ref[slot]
            sc = _dot_nt(q, k) * intra
            o = _dot(sc.astype(BF16), v) + _dot(q, st.astype(BF16)) * q_dec
            kd = (k.astype(F32) * k_dec).astype(BF16)
            state_ref[slot] = st * c_dec + _dot_tn(kd, v)
            mu = jnp.mean(o, axis=-1, keepdims=True)
            var = jnp.mean(jnp.square(o - mu), axis=-1, keepdims=True)
            o_ref[pl.ds(off, c), sl] = (o - mu) * lax.rsqrt(var + EPS)
        return 0

    lax.fori_loop(0, n_all, body, 0, unroll=True)


def _retention(rq, rk, rv, log_decay, bsz, seq_len, n_lat):
    hw = RET_HEADS_PER_STEP * HEAD_DIM
    n_hb = RET_HEADS // RET_HEADS_PER_STEP
    fwd = pl.BlockSpec((seq_len, hw), lambda b, h: (b, h))
    bwd = pl.BlockSpec((seq_len, hw), lambda b, h: (b, n_hb + h))
    return pl.pallas_call(
        functools.partial(_ret_kernel, n_lat_chunks=n_lat // RET_CHUNK,
                          n_ctx_chunks=(seq_len - n_lat) // RET_CHUNK),
        grid=(bsz, n_hb),
        in_specs=[pl.BlockSpec(memory_space=pltpu.SMEM), fwd, bwd, fwd, bwd, fwd],
        out_specs=[fwd, fwd],
        out_shape=[jax.ShapeDtypeStruct((bsz * seq_len, RET_WIDTH), F32)] * 2,
        scratch_shapes=[pltpu.VMEM((2 * RET_HEADS_PER_STEP, HEAD_DIM, HEAD_DIM), F32)],
        compiler_params=_cparams(("parallel", "parallel"), 56),
        name="retention",
    )(log_decay, rq, rq, rk, rk, rv)


def _outproj_kernel(a_ref, fl_ref, fc_ref, of_ref, ob_ref, sg_ref, w_ref, x_ref, m_ref, g_ref, wr_ref,
                    xo_ref, h_ref, aff_ref, *, n_lat_blk):
    is_ctx = pl.program_id(1) >= n_lat_blk
    fm = jnp.where(is_ctx, fc_ref[0], fl_ref[0].astype(BF16))
    rw = RET_WIDTH
    r = (sg_ref[:, :rw].astype(F32) * of_ref[...]
         + sg_ref[:, rw:].astype(F32) * ob_ref[...]).astype(BF16)
    a0, f0 = ATTN_WIDTH, ATTN_WIDTH + FOURIER_WIDTH
    y = (_dot(a_ref[...], w_ref[:a0, :]) + _dot(fm, w_ref[a0:f0, :])) + _dot(r, w_ref[f0:, :])
    x = x_ref[0] + m_ref[0, 2:3, :] * y
    xo_ref[0] = x
    h = _rms_mod(x, g_ref[...], m_ref[0, 3:4, :], m_ref[0, 4:5, :])
    h_ref[0] = h.astype(BF16)
    n_exp = aff_ref.shape[0]
    hh, hl = _split(h)
    wh, wl = _split(wr_ref[...])
    lg = _dot(hh, wh) + (_dot(hh, wl) + _dot(hl, wh))
    lg = jnp.where(lax.broadcasted_iota(jnp.int32, lg.shape, 1) < n_exp, lg, -jnp.inf)
    e = jnp.exp(lg - jnp.max(lg, axis=-1, keepdims=True))
    aff = e / jnp.sum(e, axis=-1, keepdims=True)
    aff_ref[...] = aff.T[:n_exp, :]


def _outproj(a, fm_lat, fm_ctx, o_f, o_b, sg, w_out, x, mods, g, w_router, n_lat_blk):
    bsz, l, d = x.shape
    per_seq = l // ROW_BLK
    n_exp = w_router.shape[1]
    wr = jnp.pad(w_router, ((0, 0), (0, LANES - n_exp)))
    fw = fm_lat.shape[-1]

    def flat(b, t):
        return (b * per_seq + t, 0)

    return pl.pallas_call(
        functools.partial(_outproj_kernel, n_lat_blk=n_lat_blk),
        grid=(bsz, per_seq),
        in_specs=[
            pl.BlockSpec((ROW_BLK, a.shape[1]), flat),
            pl.BlockSpec((1, ROW_BLK, fw), lambda b, t: (b, jnp.minimum(t, n_lat_blk - 1), 0)),
            pl.BlockSpec((1, ROW_BLK, fw), lambda b, t: (b, 0, 0)),
            pl.BlockSpec((ROW_BLK, o_f.shape[1]), flat),
            pl.BlockSpec((ROW_BLK, o_b.shape[1]), flat),
            pl.BlockSpec((ROW_BLK, sg.shape[1]), flat),
            pl.BlockSpec(w_out.shape, lambda b, t: (0, 0)),
            pl.BlockSpec((1, ROW_BLK, d), lambda b, t: (b, t, 0)),
            pl.BlockSpec((1, 6, d), lambda b, t: (jnp.where(t >= n_lat_blk, bsz, b), 0, 0)),
            pl.BlockSpec((1, d), lambda b, t: (0, 0)),
            pl.BlockSpec(wr.shape, lambda b, t: (0, 0)),
        ],
        out_specs=[
            pl.BlockSpec((1, ROW_BLK, d), lambda b, t: (b, t, 0)),
            pl.BlockSpec((1, ROW_BLK, d), lambda b, t: (b, t, 0)),
            pl.BlockSpec((n_exp, ROW_BLK), lambda b, t: (0, b * per_seq + t)),
        ],
        out_shape=[
            jax.ShapeDtypeStruct((bsz, l, d), F32),
            jax.ShapeDtypeStruct((bsz, l, d), BF16),
            jax.ShapeDtypeStruct((n_exp, bsz * l), F32),
        ],
        compiler_params=_cparams(("parallel", "parallel"), 48),
        name="outproj",
    )(a, fm_lat, fm_ctx, o_f, o_b, sg, w_out, x, mods, g.reshape(1, d), wr)


def _cumsum_lanes(m, out_ref, fin):
    n_exp, n = m.shape
    tri = (lax.broadcasted_iota(jnp.int32, (LANES, LANES), 0)
           <= lax.broadcasted_iota(jnp.int32, (LANES, LANES), 1)).astype(BF16)
    run = jnp.zeros((n_exp, 1), F32)
    befores = []
    for k in range(n // LANES):
        befores.append(run)
        sl = slice(k * LANES, (k + 1) * LANES)
        cnt = _dot(m[:, sl].astype(BF16), tri) + run
        out_ref[0, :, sl] = fin(cnt, sl)
        run = cnt[:, LANES - 1:LANES]
    befores.append(run)
    return befores


def _select(seg, cap, slot_off, sel_ref, tmp_ref):
    bits = pltpu.bitcast(seg, jnp.int32)
    n_exp = seg.shape[0]

    def body(it, t):
        tt = t | lax.shift_left(jnp.int32(1), 30 - it)
        cnt = jnp.sum(jnp.where(bits >= tt, 1.0, 0.0), axis=1, keepdims=True)
        return jnp.where(cnt >= cap, tt, t)

    t = lax.fori_loop(0, 31, body, jnp.zeros((n_exp, 1), jnp.int32))
    gt = bits > t
    eq = bits == t
    need = cap - jnp.sum(jnp.where(gt, 1.0, 0.0), axis=1, keepdims=True)
    eqf = jnp.where(eq, 1.0, 0.0)
    _cumsum_lanes(eqf, tmp_ref, lambda cnt, sl: cnt)
    take = eq & (tmp_ref[0] - eqf < need)
    mask = gt | take
    maskf = jnp.where(mask, 1.0, 0.0)
    return _cumsum_lanes(
        maskf, sel_ref,
        lambda cnt, sl: jnp.where(maskf[:, sl] > 0.5, cnt - 1.0 + slot_off, -1.0))


def _routing_kernel(aff_ref, sel_l, gat_l, sel_c, gat_c, st_l, tmp_l, tmp_c, *, n_lat, cap_l, cap_c):
    b = pl.program_id(0)
    a = aff_ref[...]
    lat = a[:, :n_lat]
    ctx = a[:, n_lat:]
    gat_l[0] = lat
    gat_c[0] = ctx
    befores = _select(lat, float(cap_l), 0.0, sel_l, tmp_l)
    _select(ctx, float(cap_c), (b * cap_c).astype(F32), sel_c, tmp_c)
    lane = lax.broadcasted_iota(jnp.int32, (a.shape[0], LANES), 1)
    st = jnp.zeros((a.shape[0], LANES), F32)
    per_blk = ROW_BLK // LANES
    for j in range(n_lat // ROW_BLK + 1):
        st = jnp.where(lane == j, befores[j * per_blk], st)
    st_l[0] = st


def _routing(aff_t, bsz, seq_len, n_lat, cap_l, cap_c):
    n_exp = aff_t.shape[0]
    n_ctx = seq_len - n_lat
    shp = lambda n: jax.ShapeDtypeStruct((bsz, n_exp, n), F32)
    spec = lambda n: pl.BlockSpec((1, n_exp, n), lambda b: (b, 0, 0))
    return pl.pallas_call(
        functools.partial(_routing_kernel, n_lat=n_lat, cap_l=cap_l, cap_c=cap_c),
        grid=(bsz,),
        in_specs=[pl.BlockSpec((n_exp, seq_len), lambda b: (0, b))],
        out_specs=[spec(n_lat), spec(n_lat), spec(n_ctx), spec(n_ctx), spec(LANES)],
        out_shape=[shp(n_lat), shp(n_lat), shp(n_ctx), shp(n_ctx), shp(LANES)],
        scratch_shapes=[pltpu.VMEM((1, n_exp, n_lat), F32), pltpu.VMEM((1, n_exp, n_ctx), F32)],
        compiler_params=_cparams(("parallel",), 32),
        name="routing",
    )(aff_t)


EXPERT_GROUP = 8
SLOT_ALIGN = 16
SLOT_WINDOW = 64


def _windows(starts_ref, base, stride, n, j, cap, win):
    w, rounds = [], jnp.int32(0)
    for k in range(n):
        s0 = starts_ref[base + k * stride + j]
        s1 = starts_ref[base + k * stride + j + 1]
        wk = jnp.minimum((s0 // SLOT_ALIGN) * SLOT_ALIGN, cap - win)
        w.append(wk)
        rounds = jnp.maximum(rounds, (s1 - wk + win - 1) // win)
    return w, rounds


def _gather_kernel(starts_ref, h_ref, sel_ref, aff_ref, xg_ref, gate_ref, *, n_tb, cap, win, n_exp):
    v, g, j = pl.program_id(0), pl.program_id(1), pl.program_id(2)
    ng = sel_ref.shape[2]
    base = (v * n_exp + g * ng) * (n_tb + 1)

    @pl.when(j == 0)
    def _():
        xg_ref[...] = jnp.zeros_like(xg_ref)
        gate_ref[...] = jnp.zeros_like(gate_ref)

    w, rounds = _windows(starts_ref, base, n_tb + 1, ng, j, cap, win)
    tb = h_ref.shape[1]
    row_i = lax.broadcasted_iota(jnp.int32, (win, tb), 0).astype(F32)

    def body(r, _):
        starts, pieces, gates = [], [], []
        for k in range(ng):
            lo = w[k] + r * win
            c = pl.multiple_of(jnp.minimum(lo, cap - win), SLOT_ALIGN)
            sel = sel_ref[0, 0, k:k + 1, :]
            rel = jnp.where(sel >= lo.astype(F32), sel, -1.0) - c.astype(F32)
            hit = rel == row_i
            pieces.append(jnp.where(hit, 1.0, 0.0).astype(BF16))
            gates.append(jnp.sum(jnp.where(hit, aff_ref[0, 0, k:k + 1, :], 0.0), axis=1,
                                 keepdims=True))
            starts.append(c)
        res = _dot(jnp.concatenate(pieces, axis=0), h_ref[0])
        for k in range(ng):
            rows = pl.ds(starts[k], win)
            xg_ref[k, rows, :] = (xg_ref[k, rows, :].astype(F32)
                                  + res[k * win:(k + 1) * win, :]).astype(BF16)
            gate_ref[k, rows, :] += gates[k]
        return 0

    lax.fori_loop(0, rounds, body, 0)


def _moe_gather(starts, h, sel, aff, n_tok, cap, win):
    nb, _, d = h.shape
    n_exp = sel.shape[1]
    ng = min(EXPERT_GROUP, n_exp)
    n_tb = n_tok // ROW_BLK
    rows = pl.BlockSpec((1, 1, ng, ROW_BLK), lambda v, g, j, s: (v, g, 0, j))
    grouped = lambda t: t.reshape(nb, n_exp // ng, ng, n_tok)
    return pl.pallas_call(
        functools.partial(_gather_kernel, n_tb=n_tb, cap=cap, win=win, n_exp=n_exp),
        grid_spec=pltpu.PrefetchScalarGridSpec(
            num_scalar_prefetch=1,
            grid=(nb, n_exp // ng, n_tb),
            in_specs=[pl.BlockSpec((1, ROW_BLK, d), lambda v, g, j, s: (v, j, 0)), rows, rows],
            out_specs=[pl.BlockSpec((ng, cap, d), lambda v, g, j, s: (g, v, 0)),
                       pl.BlockSpec((ng, cap, 1), lambda v, g, j, s: (g, v, 0))],
        ),
        out_shape=[jax.ShapeDtypeStruct((n_exp, nb * cap, d), BF16),
                   jax.ShapeDtypeStruct((n_exp, nb * cap, 1), F32)],
        compiler_params=_cparams(("parallel", "parallel", "arbitrary"), 56),
        name="moe_gather",
    )(starts, h, grouped(sel), grouped(aff))


def _ffn_kernel(xl_ref, xc_ref, wg_ref, wu_ref, al_ref, ac_ref, wgb_ref, wub_ref):
    r = pl.program_id(2)

    @pl.when(r == 0)
    def _():
        wgb_ref[...] = wg_ref[0, 0].astype(BF16)
        wub_ref[...] = wu_ref[0, 0].astype(BF16)

    def swiglu(x_ref, o_ref):
        x = x_ref[0]
        o_ref[0] = (_silu(_dot(x, wgb_ref[...])) * _dot(x, wub_ref[...])).astype(BF16)

    @pl.when(r == 0)
    def _():
        swiglu(xc_ref, ac_ref)

    @pl.when(r > 0)
    def _():
        swiglu(xl_ref, al_ref)


def _moe_ffn(xg_l, xg_c, w_gate, w_up, layer):
    n_exp, rows, d = xg_l.shape
    rows_c = xg_c.shape[1]
    ff = w_gate.shape[3]
    fh = _largest_tile(ff, 512, LANES)
    rblk = _largest_tile(rows, 1024)
    nls = rows // rblk
    prev = lambda r: jnp.maximum(r - 1, 0)
    wspec = pl.BlockSpec((1, 1, d, fh), lambda e, f, r: (layer, e, 0, f))
    return pl.pallas_call(
        _ffn_kernel,
        grid=(n_exp, ff // fh, nls + 1),
        in_specs=[pl.BlockSpec((1, rblk, d), lambda e, f, r: (e, prev(r), 0)),
                  pl.BlockSpec((1, rows_c, d), lambda e, f, r: (e, 0, 0)), wspec, wspec],
        out_specs=[pl.BlockSpec((1, rblk, fh), lambda e, f, r: (e, prev(r), f)),
                   pl.BlockSpec((1, rows_c, fh), lambda e, f, r: (e, 0, f))],
        out_shape=[jax.ShapeDtypeStruct((n_exp, rows, ff), BF16),
                   jax.ShapeDtypeStruct((n_exp, rows_c, ff), BF16)],
        scratch_shapes=[pltpu.VMEM((d, fh), BF16), pltpu.VMEM((d, fh), BF16)],
        compiler_params=_cparams(("parallel", "parallel", "arbitrary"), 48),
        name="moe_ffn",
    )(xg_l, xg_c, w_gate, w_up)


def _down_kernel(al_ref, ac_ref, gl_ref, gc_ref, wd_ref, yl_ref, yc_ref, wdb_ref):
    v = pl.program_id(1)

    @pl.when(v == 0)
    def _():
        wdb_ref[...] = wd_ref[0, 0].astype(BF16)

    @pl.when(v == 0)
    def _():
        yc_ref[0] = (_dot(ac_ref[0], wdb_ref[...]) * gc_ref[0]).astype(BF16)

    @pl.when(v > 0)
    def _():
        yl_ref[0] = (_dot(al_ref[0], wdb_ref[...]) * gl_ref[0]).astype(BF16)


def _moe_down(act_l, act_c, gate_l, gate_c, w_down, layer, cap):
    n_exp, rows, ff = act_l.shape
    rows_c = act_c.shape[1]
    d = w_down.shape[3]
    nb = rows // cap
    prev = lambda v: jnp.maximum(v - 1, 0)
    return pl.pallas_call(
        _down_kernel,
        grid=(n_exp, nb + 1),
        in_specs=[
            pl.BlockSpec((1, cap, ff), lambda e, v: (e, prev(v), 0)),
            pl.BlockSpec((1, rows_c, ff), lambda e, v: (e, 0, 0)),
            pl.BlockSpec((1, cap, 1), lambda e, v: (e, prev(v), 0)),
            pl.BlockSpec((1, rows_c, 1), lambda e, v: (e, 0, 0)),
            pl.BlockSpec((1, 1, ff, d), lambda e, v: (layer, e, 0, 0)),
        ],
        out_specs=[pl.BlockSpec((1, cap, d), lambda e, v: (prev(v), e, 0)),
                   pl.BlockSpec((1, rows_c, d), lambda e, v: (0, e, 0))],
        out_shape=[jax.ShapeDtypeStruct((nb, n_exp * cap, d), BF16),
                   jax.ShapeDtypeStruct((1, n_exp * rows_c, d), BF16)],
        scratch_shapes=[pltpu.VMEM((ff, d), BF16)],
        compiler_params=_cparams(("parallel", "arbitrary"), 48),
        name="moe_down",
    )(act_l, act_c, gate_l, gate_c, w_down)


def _combine_kernel(starts_ref, y_ref, sel_ref, out_ref, st_ref, *, n_tb, cap, win, n_exp):
    v, j = pl.program_id(0), pl.program_id(2)
    base = v * n_exp * (n_tb + 1)
    w, rounds = _windows(starts_ref, base, n_tb + 1, n_exp, j, cap, win)
    cols = lax.broadcasted_iota(jnp.int32, (n_exp, n_exp * win), 1)
    spread = jnp.where(cols // win == lax.broadcasted_iota(jnp.int32, (n_exp, n_exp * win), 0),
                       1.0, 0.0).astype(BF16)
    slot_in_win = (lax.broadcasted_iota(jnp.int32, (1, n_exp * win), 1) % win + 1).astype(F32)
    exp_lane = lax.broadcasted_iota(jnp.int32, (1, n_exp), 1)

    def one_round(r):
        lo_vec = jnp.zeros((1, n_exp), F32)
        c_vec = jnp.zeros((1, n_exp), F32)
        for e in range(n_exp):
            lo = w[e] + r * win
            c = pl.multiple_of(jnp.minimum(lo, cap - win), SLOT_ALIGN)
            src = pl.multiple_of(e * cap + c, SLOT_ALIGN)
            st_ref[e * win:(e + 1) * win, :] = y_ref[0, pl.ds(src, win), :]
            lo_vec = jnp.where(exp_lane == e, lo.astype(F32), lo_vec)
            c_vec = jnp.where(exp_lane == e, c.astype(F32), c_vec)
        sel = sel_ref[0]
        rel = jnp.where(sel >= lo_vec, sel, -1.0) - c_vec
        rel1 = jnp.where((rel >= 0.0) & (rel < float(win)), rel + 1.0, 0.0)
        hit = _dot(rel1.astype(BF16), spread) == slot_in_win
        return _dot(jnp.where(hit, 1.0, 0.0).astype(BF16), st_ref[...])

    out_ref[0] = one_round(0)

    def body(r, _):
        out_ref[0] += one_round(r)
        return 0

    lax.fori_loop(1, rounds, body, 0)


def _moe_combine(starts, y, sel_t, n_tok, cap, win):
    nb, _, d = y.shape
    n_exp = sel_t.shape[2]
    n_tb = n_tok // ROW_BLK
    pw = _largest_tile(d, 1024, LANES)
    return pl.pallas_call(
        functools.partial(_combine_kernel, n_tb=n_tb, cap=cap, win=win, n_exp=n_exp),
        grid_spec=pltpu.PrefetchScalarGridSpec(
            num_scalar_prefetch=1,
            grid=(nb, d // pw, n_tb),
            in_specs=[pl.BlockSpec((1, n_exp * cap, pw), lambda v, p, j, s: (v, 0, p)),
                      pl.BlockSpec((1, ROW_BLK, n_exp), lambda v, p, j, s: (v, j, 0))],
            out_specs=pl.BlockSpec((1, ROW_BLK, pw), lambda v, p, j, s: (v, j, p)),
            scratch_shapes=[pltpu.VMEM((n_exp * win, pw), BF16)],
        ),
        out_shape=jax.ShapeDtypeStruct((nb, n_tok, d), F32),
        compiler_params=_cparams(("parallel", "parallel", "arbitrary"), 48),
        name="moe_combine",
    )(starts, y, sel_t)


def _rope_tables(pos_groups, half):
    freqs = ROPE_BASE ** (-jnp.arange(half, dtype=F32) / half)
    cos, sin = [], []
    for pos in pos_groups:
        ang = pos.astype(F32)[:, None] * freqs[None, :]
        c, s = jnp.cos(ang), jnp.sin(ang)
        cos += [c, c]
        sin += [-s, s]
    return jnp.stack([jnp.concatenate(t, axis=1) for t in (cos, sin)])


def _attn_tables(n_lat, n_ctx):
    s = jnp.arange(n_lat)
    tab = _rope_tables([s // GRID_W, s % GRID_W], HEAD_DIM // 4)
    ident = jnp.stack([jnp.ones((n_ctx, HEAD_DIM), F32), jnp.zeros((n_ctx, HEAD_DIM), F32)])
    return jnp.concatenate([tab, ident], axis=1)


def _ret_tables(n_lat, n_ctx):
    s = jnp.arange(n_lat)
    t = jnp.arange(n_ctx)
    fwd = jnp.concatenate([n_ctx + s, t])
    bwd = jnp.concatenate([n_ctx + (n_lat - 1 - s), n_ctx - 1 - t])
    return jnp.concatenate([_rope_tables([fwd], HEAD_DIM // 2),
                            _rope_tables([bwd], HEAD_DIM // 2)], axis=0)


def kernel(x, c, ctx, c_ctx, w_ada, b_ada, norm_mix, norm_ffn, w_in, q_norm, k_norm, ret_log_decay,
           w_out, w_router, w_gate, w_up, w_down):
    bsz, n_lat, d = x.shape
    n_ctx = ctx.shape[1]
    seq_len = n_lat + n_ctx
    depth = w_ada.shape[0]
    n_exp = w_router.shape[2]
    n_lat_blk = n_lat // ROW_BLK
    assert n_lat % ROW_BLK == 0 and n_ctx == ROW_BLK and bsz + 1 <= 8
    cap_l = CAPACITY_FACTOR * n_lat // n_exp
    cap_c = CAPACITY_FACTOR * n_ctx // n_exp

    cond8 = jnp.zeros((8, d), F32).at[:bsz].set(c).at[bsz].set(c_ctx)
    mods = _adaln(cond8, w_ada, b_ada).reshape(depth, 8, 6, d)

    attn_tab = _attn_tables(n_lat, n_ctx)
    ret_tab = _ret_tables(n_lat, n_ctx)
    wc = _chan_mats()
    two_stage = n_lat % (FFT_COLS * 2 * SUBLANES) == 0
    lat_mats = dict(fft=_fft_mats(n_lat)) if two_stage else dict(dft=_dft_mats(n_lat))
    dft_c = _dft_mats(n_ctx)

    o_q, o_k = 0, ATTN_WIDTH
    o_v = o_k + KV_WIDTH
    o_f = o_v + KV_WIDTH
    o_rq = o_f + FOURIER_WIDTH
    o_rk = o_rq + RET_WIDTH
    o_rv = o_rk + RET_WIDTH
    o_g = o_rv + RET_WIDTH
    o_end = o_g + 2 * RET_WIDTH
    q_scale = HEAD_DIM ** -0.5 * math.log2(math.e)

    sb_c = bsz * cap_c
    win_l = min(cap_l, SLOT_WINDOW)
    win_c = cap_c
    starts_c = jnp.tile(jnp.arange(bsz + 1, dtype=jnp.int32) * cap_c, n_exp)

    xs = jnp.concatenate([x, ctx], axis=1)
    delta_lat = delta_ctx = None
    for l in range(depth):
        xs, h = _norm(xs, mods[l], norm_mix[l], n_lat_blk, delta_lat, delta_ctx,
                      mods[l - 1] if l else None)
        h2d = h.reshape(bsz * seq_len, d)
        wl = w_in[l].astype(BF16)
        q = _proj(h2d, wl[:, o_q:o_k], seq_len, "qk", gain=q_norm[l], tab=attn_tab,
                  n_heads=ATTN_HEADS, scale=q_scale, half=HEAD_DIM // 4)
        k = _proj(h2d, wl[:, o_k:o_v], seq_len, "qk", gain=k_norm[l], tab=attn_tab,
                  n_heads=ATTN_KV_HEADS, scale=1.0, half=HEAD_DIM // 4)
        v, f = _proj(h2d, wl[:, o_v:o_rq], seq_len, "split", split_at=KV_WIDTH)
        rq = _proj(h2d, wl[:, o_rq:o_rk], seq_len, "ret", tab=ret_tab, n_heads=RET_HEADS,
                   scale=1.0, half=HEAD_DIM // 2)
        rk = _proj(h2d, wl[:, o_rk:o_rv], seq_len, "ret", tab=ret_tab, n_heads=RET_HEADS,
                   scale=HEAD_DIM ** -0.5, half=HEAD_DIM // 2)
        rv = _proj(h2d, wl[:, o_rv:o_g], seq_len, "plain")
        sg = _proj(h2d, wl[:, o_g:o_end], seq_len, "silu")

        a = _attention(q, k, v, bsz, seq_len, n_lat)
        fm_lat = _fourier(f, bsz, seq_len, 0, n_lat, wc, **lat_mats)
        fm_ctx = _fourier(f, bsz, seq_len, n_lat, n_ctx, wc, dft=dft_c)
        o_f, o_b = _retention(rq, rk, rv, ret_log_decay[l], bsz, seq_len, n_lat)

        xs, h2, aff_t = _outproj(a, fm_lat, fm_ctx, o_f, o_b, sg, w_out[l].astype(BF16), xs, mods[l],
                                 norm_ffn[l], w_router[l], n_lat_blk)

        sel_l, gat_l, sel_c, gat_c, st_l = _routing(aff_t, bsz, seq_len, n_lat, cap_l, cap_c)
        starts_l = st_l[:, :, :n_lat_blk + 1].astype(jnp.int32).reshape(-1)
        merge = lambda t: t.transpose(1, 0, 2).reshape(1, n_exp, bsz * n_ctx)
        tok_major = lambda t: t.transpose(0, 2, 1)
        sel_c, gat_c = merge(sel_c), merge(gat_c)
        h_ctx = h2[:, n_lat:, :].reshape(1, bsz * n_ctx, d)

        xg_l, gate_l = _moe_gather(starts_l, h2, sel_l, gat_l, n_lat, cap_l, win_l)
        xg_c, gate_c = _moe_gather(starts_c, h_ctx, sel_c, gat_c, bsz * n_ctx, sb_c, win_c)
        act_l, act_c = _moe_ffn(xg_l, xg_c, w_gate, w_up, l)
        y_l, y_c = _moe_down(act_l, act_c, gate_l, gate_c, w_down, l, cap_l)
        delta_lat = _moe_combine(starts_l, y_l, tok_major(sel_l), n_lat, cap_l, win_l)
        delta_ctx = _moe_combine(starts_c, y_c, tok_major(sel_c), bsz * n_ctx, sb_c,
                                 win_c).reshape(bsz * n_ctx, d)

    out, _ = _norm(xs, None, None, n_lat_blk, delta_lat, delta_ctx, mods[depth - 1],
                   want_h=False, lat_only=True)
    return out
```

```python
import functools
import math

import jax
import jax.numpy as jnp
from jax import lax
from jax.experimental import pallas as pl
from jax.experimental.pallas import tpu as pltpu

HEAD_DIM = 128
ATTN_HEADS = 8
ATTN_KV_HEADS = 2
ATTN_GROUP = ATTN_HEADS // ATTN_KV_HEADS
FOURIER_GROUPS = 4
RET_HEADS = 4
RET_CHUNK = 128
GRID_W = 64
ROPE_BASE = 10000.0
EPS = 1e-6
CAPACITY_FACTOR = 2

ATTN_WIDTH = ATTN_HEADS * HEAD_DIM
KV_WIDTH = ATTN_KV_HEADS * HEAD_DIM
FOURIER_WIDTH = FOURIER_GROUPS * HEAD_DIM
RET_WIDTH = RET_HEADS * HEAD_DIM

ROW_BLK = 256
LANES = 128
MIB = 1024 * 1024

F32 = jnp.float32
BF16 = jnp.bfloat16


def _cparams(sem, vmem_mib):
    return pltpu.CompilerParams(dimension_semantics=sem, vmem_limit_bytes=vmem_mib * MIB)


def _largest_tile(n, cap, mult=8):
    best = mult
    for t in range(mult, min(n, cap) + 1, mult):
        if n % t == 0:
            best = t
    return best


def _split(a):
    hi = a.astype(BF16)
    lo = (a - hi.astype(F32)).astype(BF16)
    return hi, lo


def _dot(a, b):
    return jnp.dot(a, b, preferred_element_type=F32)


def _dot_nt(a, b):
    return lax.dot_general(a, b, (((1,), (1,)), ((), ())), preferred_element_type=F32)


def _dot_tn(a, b):
    return lax.dot_general(a, b, (((0,), (0,)), ((), ())), preferred_element_type=F32)


def _silu(a):
    return a / (1.0 + jnp.exp(-a))


def _adaln_kernel(c_ref, w_ref, b_ref, o_ref):
    s = _silu(c_ref[...])
    sh, sl = _split(s)
    wh, wl = _split(w_ref[0])
    o_ref[0] = _dot(sh, wh) + (_dot(sh, wl) + _dot(sl, wh)) + b_ref[0]


def _adaln(cond8, w_ada, b_ada):
    depth, d, n = w_ada.shape
    tn = _largest_tile(n, 768, LANES)
    return pl.pallas_call(
        _adaln_kernel,
        grid=(depth, n // tn),
        in_specs=[
            pl.BlockSpec((8, d), lambda l, j: (0, 0)),
            pl.BlockSpec((1, d, tn), lambda l, j: (l, 0, j)),
            pl.BlockSpec((1, 1, tn), lambda l, j: (l, 0, j)),
        ],
        out_specs=pl.BlockSpec((1, 8, tn), lambda l, j: (l, 0, j)),
        out_shape=jax.ShapeDtypeStruct((depth, 8, n), F32),
        compiler_params=_cparams(("parallel", "parallel"), 48),
        name="adaln",
    )(cond8, w_ada, b_ada.reshape(depth, 1, n))


def _rms_mod(x, g, shift, scale):
    y = x * lax.rsqrt(jnp.mean(x * x, axis=-1, keepdims=True) + EPS)
    return (y * g) * (1.0 + scale) + shift


def _norm_kernel(xl_ref, xc_ref, m_ref, g_ref, h_ref, *, n_lat_blk):
    is_ctx = pl.program_id(1) >= n_lat_blk
    x = jnp.where(is_ctx, xc_ref[...], xl_ref[0])
    h_ref[0] = _rms_mod(x, g_ref[...], m_ref[0, 0:1, :], m_ref[0, 1:2, :]).astype(BF16)


def _norm(x_lat, x_ctx, mods, g, n_lat_blk):
    bsz, n_lat, d = x_lat.shape
    n_blk = n_lat_blk + x_ctx.shape[0] // bsz // ROW_BLK
    return pl.pallas_call(
        functools.partial(_norm_kernel, n_lat_blk=n_lat_blk),
        grid=(bsz, n_blk),
        in_specs=[
            pl.BlockSpec((1, ROW_BLK, d), lambda b, t: (b, jnp.minimum(t, n_lat_blk - 1), 0)),
            pl.BlockSpec((ROW_BLK, d), lambda b, t: (b, 0)),
            pl.BlockSpec((1, 6, d), lambda b, t: (jnp.where(t >= n_lat_blk, bsz, b), 0, 0)),
            pl.BlockSpec((1, d), lambda b, t: (0, 0)),
        ],
        out_specs=pl.BlockSpec((1, ROW_BLK, d), lambda b, t: (b, t, 0)),
        out_shape=jax.ShapeDtypeStruct((bsz, n_blk * ROW_BLK, d), BF16),
        compiler_params=_cparams(("parallel", "parallel"), 32),
        name="norm",
    )(x_lat, x_ctx, mods, g.reshape(1, d))


def _dot_wide(a, m):
    hi, lo = _split(a)
    return _dot(hi, m) + _dot(lo, m)


def _partner_matrix(half):
    src = lax.broadcasted_iota(jnp.int32, (LANES, LANES), 0)
    dst = lax.broadcasted_iota(jnp.int32, (LANES, LANES), 1)
    mate = jnp.where(dst % (2 * half) < half, dst + half, dst - half)
    return jnp.where(src == mate, 1.0, 0.0).astype(BF16)


def _proj_kernel(*refs, flavor, n_heads, scale, half, split_at, n_sub):
    h_ref, w_ref = refs[0], refs[1]
    if n_sub > 1:
        sub = h_ref.shape[0] // n_sub
        accs = [_dot(h_ref[s * sub:(s + 1) * sub, :], w_ref[...]) for s in range(n_sub)]
        for s, acc in enumerate(accs):
            _proj_epilogue(acc, slice(s * sub, (s + 1) * sub), refs, flavor, n_heads, scale, half)
        return
    acc = _dot(h_ref[...], w_ref[...])
    if flavor in ("qk", "ret"):
        _proj_epilogue(acc, slice(None), refs, flavor, n_heads, scale, half)
    elif flavor == "plain":
        refs[2][...] = acc.astype(BF16)
    elif flavor == "split":
        refs[2][...] = acc[:, :split_at].astype(BF16)
        refs[3][...] = acc[:, split_at:].astype(BF16)
    elif flavor == "silu":
        refs[2][...] = _silu(acc).astype(BF16)
    else:
        raise ValueError(flavor)


def _proj_epilogue(acc, rows, refs, flavor, n_heads, scale, half):
    partner = _partner_matrix(half)
    if flavor == "qk":
        gain_ref, tab_ref, o_ref = refs[2], refs[3], refs[4]
        ones = jnp.ones((LANES, LANES), BF16)
        for hd in range(n_heads):
            sl = slice(hd * HEAD_DIM, (hd + 1) * HEAD_DIM)
            z = acc[:, sl]
            ms = _dot_wide(z * z, ones) * (1.0 / HEAD_DIM)
            z = (z * lax.rsqrt(ms + EPS)) * gain_ref[...]
            z = z * tab_ref[0, rows, :] + _dot_wide(z, partner) * tab_ref[1, rows, :]
            o_ref[rows, sl] = (z * scale).astype(BF16)
    else:
        tab_ref, o_ref = refs[2], refs[3]
        width = n_heads * HEAD_DIM
        for hd in range(n_heads):
            sl = slice(hd * HEAD_DIM, (hd + 1) * HEAD_DIM)
            z = acc[:, sl] * scale
            pz = _dot_wide(z, partner)
            o_ref[rows, sl] = (z * tab_ref[0, rows, :] + pz * tab_ref[1, rows, :]).astype(BF16)
            o_ref[rows, width + hd * HEAD_DIM: width + (hd + 1) * HEAD_DIM] = (
                z * tab_ref[2, rows, :] + pz * tab_ref[3, rows, :]).astype(BF16)


def _proj(h2d, w, seq_len, flavor, *, gain=None, tab=None, n_heads=0, scale=1.0, half=0,
          split_at=0):
    t_rows, d = h2d.shape
    n = w.shape[1]
    n_sub = 2 if flavor in ("qk", "ret") else 1
    tm = _largest_tile(seq_len, 1088 * n_sub, 8 * n_sub)
    per_seq = seq_len // tm
    in_specs = [pl.BlockSpec((tm, d), lambda i: (i, 0)), pl.BlockSpec((d, n), lambda i: (0, 0))]
    args = [h2d, w]
    if flavor == "qk":
        in_specs.append(pl.BlockSpec((1, HEAD_DIM), lambda i: (0, 0)))
        args.append(gain.reshape(1, HEAD_DIM))
    if flavor in ("qk", "ret"):
        ntab = tab.shape[0]
        in_specs.append(pl.BlockSpec((ntab, tm, HEAD_DIM), lambda i: (0, i % per_seq, 0)))
        args.append(tab)
    if flavor == "split":
        widths = [split_at, n - split_at]
    elif flavor == "ret":
        widths = [2 * n]
    else:
        widths = [n]
    out_specs = [pl.BlockSpec((tm, wd), lambda i: (i, 0)) for wd in widths]
    out_shape = [jax.ShapeDtypeStruct((t_rows, wd), BF16) for wd in widths]
    outs = pl.pallas_call(
        functools.partial(_proj_kernel, flavor=flavor, n_heads=n_heads, scale=scale, half=half,
                          split_at=split_at, n_sub=n_sub),
        grid=(t_rows // tm,),
        in_specs=in_specs,
        out_specs=out_specs,
        out_shape=out_shape,
        compiler_params=_cparams(("parallel",), 56),
        name="proj_" + flavor,
    )(*args)
    return outs if len(outs) > 1 else outs[0]


SUBLANES = 8


SAFE_SHIFT = 50.0


def _lane_fold(t, op):
    return functools.reduce(op, [t[:, i * LANES:(i + 1) * LANES]
                                 for i in range(t.shape[1] // LANES)])


def _fold_rows(t, op):
    return functools.reduce(op, [t[r:r + SUBLANES, :] for r in range(0, t.shape[0], SUBLANES)])


def _attn_group_bounded(q_all, m_row, k_ref, vt_ref, pt_ref, k_lo, n_keys, tk):
    lp = None
    for c in range(n_keys // tk):
        st = _dot_nt(k_ref[k_lo + c * tk:k_lo + (c + 1) * tk, :], q_all)
        p = jnp.exp2(st - m_row)
        part = _fold_rows(p, jnp.add)
        lp = part if lp is None else lp + part
        pt_ref[c * tk:(c + 1) * tk, :] = p.astype(BF16)
    l = jnp.sum(lp, axis=0, keepdims=True)
    return _dot(vt_ref[:, k_lo:k_lo + n_keys], pt_ref[:n_keys, :]) / l


def _attn_head_exact(q, k_ref, v_ref, s_ref, p_ref, k_lo, n_keys, tk):
    s_ref[:, :n_keys] = _dot_nt(q, k_ref[k_lo:k_lo + n_keys, :])
    n_chunks = n_keys // tk
    mp = None
    for c in range(n_chunks):
        part = _lane_fold(s_ref[:, c * tk:(c + 1) * tk], jnp.maximum)
        mp = part if mp is None else jnp.maximum(mp, part)
    m = jnp.max(mp, axis=-1, keepdims=True)
    lp = None
    for c in range(n_chunks):
        p = jnp.exp2(s_ref[:, c * tk:(c + 1) * tk] - m)
        part = _lane_fold(p, jnp.add)
        lp = part if lp is None else lp + part
        p_ref[:, c * tk:(c + 1) * tk] = p.astype(BF16)
    l = jnp.sum(lp, axis=-1, keepdims=True)
    return _dot(p_ref[:, :n_keys], v_ref[k_lo:k_lo + n_keys, :]) / l


def _attn_kernel(q_ref, k_ref, v_ref, o_ref, s_ref, p_ref, pt_ref, vt_ref, kmax_ref, *, tk, n_lat,
                 n_ctx, n_lat_qblk):
    i = pl.program_id(2)
    is_ctx = i >= n_lat_qblk

    @pl.when(i == 0)
    def _():
        kk = k_ref[...].astype(F32)
        kn2 = jnp.max(jnp.sum(kk * kk, axis=-1, keepdims=True), axis=0, keepdims=True)
        kmax_ref[...] = jnp.broadcast_to(jnp.sqrt(kn2), kmax_ref.shape)
        vt_ref[...] = v_ref[...].astype(F32).T.astype(BF16)

    heads = [slice(hd * HEAD_DIM, (hd + 1) * HEAD_DIM) for hd in range(ATTN_GROUP)]
    tq = q_ref.shape[0]
    q_all = jnp.concatenate([q_ref[:, sl] for sl in heads], axis=0)
    qf = q_all.astype(F32)
    qn2 = _dot_nt(jnp.ones((SUBLANES, HEAD_DIM), BF16), (qf * qf).astype(BF16))[0:1, :]
    m_row = jnp.sqrt(qn2) * kmax_ref[0:1, 0:1] * (1.0 + 2.0 ** -6)
    worst = jnp.max(m_row)
    bounded_ok = worst <= SAFE_SHIFT

    def run(k_lo, n_keys, bounded):
        if bounded:
            ot = _attn_group_bounded(q_all, m_row, k_ref, vt_ref, pt_ref, k_lo, n_keys, tk)
            outs = [ot[:, hd * tq:(hd + 1) * tq].T for hd in range(len(heads))]
        else:
            outs = [_attn_head_exact(q_ref[:, sl], k_ref, v_ref, s_ref.at[hd], p_ref.at[hd], k_lo,
                                     n_keys, tk) for hd, sl in enumerate(heads)]
        for o, sl in zip(outs, heads):
            o_ref[:, sl] = o.astype(BF16)

    for ctx_case, (k_lo, n_keys) in ((False, (0, n_lat + n_ctx)), (True, (n_lat, n_ctx))):
        for bounded in (True, False):
            @pl.when((is_ctx == ctx_case) & (bounded_ok == bounded))
            def _(k_lo=k_lo, n_keys=n_keys, bounded=bounded):
                run(k_lo, n_keys, bounded)


def _attention(q, k, v, bsz, seq_len, n_lat):
    tq = ROW_BLK
    tk = ROW_BLK
    per_seq = seq_len // tq
    gw = ATTN_GROUP * HEAD_DIM
    return pl.pallas_call(
        functools.partial(_attn_kernel, tk=tk, n_lat=n_lat, n_ctx=seq_len - n_lat,
                          n_lat_qblk=n_lat // tq),
        grid=(bsz, ATTN_KV_HEADS, per_seq),
        in_specs=[
            pl.BlockSpec((tq, gw), lambda b, g, i: (b * per_seq + i, g)),
            pl.BlockSpec((seq_len, HEAD_DIM), lambda b, g, i: (b, g)),
            pl.BlockSpec((seq_len, HEAD_DIM), lambda b, g, i: (b, g)),
        ],
        out_specs=pl.BlockSpec((tq, gw), lambda b, g, i: (b * per_seq + i, g)),
        out_shape=jax.ShapeDtypeStruct(q.shape, BF16),
        scratch_shapes=[pltpu.VMEM((ATTN_GROUP, tq, seq_len), F32),
                        pltpu.VMEM((ATTN_GROUP, tq, seq_len), BF16),
                        pltpu.VMEM((seq_len, ATTN_GROUP * tq), BF16),
                        pltpu.VMEM((HEAD_DIM, seq_len), BF16),
                        pltpu.VMEM((SUBLANES, LANES), F32)],
        compiler_params=_cparams(("parallel", "parallel", "arbitrary"), 58),
        name="attention",
    )(q, k, v)


def _fourier_a_kernel(f_ref, wc_ref, o_ref):
    g = _dot(f_ref[...], wc_ref[...])
    wdt = f_ref.shape[1]
    o_ref[0, 0] = g[:, :wdt].astype(o_ref.dtype)
    o_ref[0, 1] = g[:, wdt:].astype(o_ref.dtype)


def _fourier_b_kernel(m_ref, g_ref, o_ref):
    o_ref[0] = _dot(m_ref[...], g_ref[0]).astype(BF16)


FFT_COLS = 64
FFT_PER_STEP = SUBLANES


def _fft1_kernel(g_ref, m_ref, tc_ref, ts_ref, o_ref):
    width = o_ref.shape[4]
    n1 = g_ref.shape[2]
    for s in range(g_ref.shape[3]):
        g = jnp.concatenate([g_ref[0, 0, :, s, :], g_ref[0, 1, :, s, :]], axis=0).astype(BF16)
        a = _dot(m_ref[...], g)
        ar, ai = a[:n1], a[n1:]
        tc = tc_ref[0, :, s * LANES:(s + 1) * LANES]
        ts = ts_ref[0, :, s * LANES:(s + 1) * LANES]
        lanes = [slice(q * LANES, (q + 1) * LANES) for q in range(width // LANES)]
        o_ref[0, 0, :, s, :] = jnp.concatenate([ar[:, q] * tc + ai[:, q] * ts for q in lanes], axis=1)
        o_ref[0, 1, :, s, :] = jnp.concatenate([ai[:, q] * tc - ar[:, q] * ts for q in lanes], axis=1)


def _fft2_kernel(b_ref, m_ref, o_ref):
    for s in range(b_ref.shape[2]):
        b = jnp.concatenate([b_ref[0, 0, s], b_ref[0, 1, s]], axis=0).astype(BF16)
        o_ref[0, :, s, :] = _dot(m_ref[...], b)


def _fourier_two_stage(g, fft):
    m1, tc, ts, m3 = fft
    bsz, _, n, width = g.shape
    n2 = m3.shape[0]
    n1 = n // n2
    per = FFT_PER_STEP
    b = pl.pallas_call(
        _fft1_kernel,
        grid=(bsz, n2 // per),
        in_specs=[
            pl.BlockSpec((1, 2, n1, per, width), lambda b, k: (b, 0, 0, k, 0)),
            pl.BlockSpec(m1.shape, lambda b, k: (0, 0)),
            pl.BlockSpec((1, n1, per * LANES), lambda b, k: (k, 0, 0)),
            pl.BlockSpec((1, n1, per * LANES), lambda b, k: (k, 0, 0)),
        ],
        out_specs=pl.BlockSpec((1, 2, n1, per, width), lambda b, k: (b, 0, 0, k, 0)),
        out_shape=jax.ShapeDtypeStruct((bsz, 2, n1, n2, width), F32),
        compiler_params=_cparams(("parallel", "parallel"), 32),
        name="fourier_fft1",
    )(g.reshape(bsz, 2, n1, n2, width), m1, tc, ts)
    out = pl.pallas_call(
        _fft2_kernel,
        grid=(bsz, n1 // per),
        in_specs=[
            pl.BlockSpec((1, 2, per, n2, width), lambda b, k: (b, 0, k, 0, 0)),
            pl.BlockSpec(m3.shape, lambda b, k: (0, 0)),
        ],
        out_specs=pl.BlockSpec((1, n2, per, width), lambda b, k: (b, 0, k, 0)),
        out_shape=jax.ShapeDtypeStruct((bsz, n2, n1, width), F32),
        compiler_params=_cparams(("parallel", "parallel"), 32),
        name="fourier_fft2",
    )(b, m3)
    return out.reshape(bsz, n, width)


def _fft_mats(n):
    n2 = FFT_COLS
    n1 = n // n2
    per = FFT_PER_STEP

    def cs(rows, cols, period):
        ang = ((rows[:, None] * cols[None, :]) % period).astype(F32) * (2.0 * math.pi / period)
        return jnp.cos(ang), jnp.sin(ang)

    i1, i2 = jnp.arange(n1, dtype=jnp.int32), jnp.arange(n2, dtype=jnp.int32)
    c1, s1 = cs(i1, i1, n1)
    m1 = (jnp.block([[c1, s1], [-s1, c1]]) * n1 ** -0.5).astype(BF16)
    c3, s3 = cs(i2, i2, n2)
    m3 = (jnp.concatenate([c3, s3], axis=1) * n2 ** -0.5).astype(BF16)
    tc, ts = cs(i1, i2, n)
    expand = lambda t: jnp.repeat(t.T.reshape(n2 // per, per, n1).transpose(0, 2, 1), LANES,
                                  axis=2)
    return m1, expand(tc), expand(ts), m3


def _fourier(f2d, bsz, seq_len, row_off, n, wc, dft=None, fft=None):
    per_seq = seq_len // ROW_BLK
    off_blk = row_off // ROW_BLK
    nb = n // ROW_BLK
    width = f2d.shape[1]
    g = pl.pallas_call(
        _fourier_a_kernel,
        grid=(bsz, nb),
        in_specs=[
            pl.BlockSpec((ROW_BLK, width), lambda b, t: (b * per_seq + off_blk + t, 0)),
            pl.BlockSpec((width, 2 * width), lambda b, t: (0, 0)),
        ],
        out_specs=pl.BlockSpec((1, 2, ROW_BLK, width), lambda b, t: (b, 0, t, 0)),
        out_shape=jax.ShapeDtypeStruct((bsz, 2, n, width), BF16 if fft is None else F32),
        compiler_params=_cparams(("parallel", "parallel"), 32),
        name="fourier_chan",
    )(f2d, wc)
    if fft is not None:
        return _fourier_two_stage(g, fft)
    g = g.reshape(bsz, 2 * n, width)
    return pl.pallas_call(
        _fourier_b_kernel,
        grid=(nb, bsz),
        in_specs=[
            pl.BlockSpec((ROW_BLK, 2 * n), lambda i, b: (i, 0)),
            pl.BlockSpec((1, 2 * n, width), lambda i, b: (b, 0, 0)),
        ],
        out_specs=pl.BlockSpec((1, ROW_BLK, width), lambda i, b: (b, i, 0)),
        out_shape=jax.ShapeDtypeStruct((bsz, n, width), BF16),
        compiler_params=_cparams(("parallel", "parallel"), 48),
        name="fourier_pos",
    )(dft, g)


def _dft_mats(n):
    i = jnp.arange(n, dtype=jnp.int32)
    prod = (i[:, None] * i[None, :]) % n
    ang = prod.astype(F32) * (2.0 * math.pi / n)
    s = n ** -0.5
    return jnp.concatenate([jnp.cos(ang) * s, jnp.sin(ang) * s], axis=1).astype(BF16)


def _chan_mats():
    i = jnp.arange(HEAD_DIM, dtype=jnp.int32)
    ang = ((i[:, None] * i[None, :]) % HEAD_DIM).astype(F32) * (2.0 * math.pi / HEAD_DIM)
    s = HEAD_DIM ** -0.5
    eye = jnp.eye(FOURIER_GROUPS, dtype=F32)
    c = jnp.kron(eye, jnp.cos(ang) * s)
    sn = jnp.kron(eye, -jnp.sin(ang) * s)
    return jnp.concatenate([c, sn], axis=1).astype(BF16)


RET_HEADS_PER_STEP = 2


def _ret_kernel(ld_ref, qf_ref, qb_ref, kf_ref, kb_ref, v_ref, of_ref, ob_ref, state_ref,
                *, n_lat_chunks, n_ctx_chunks):
    c = RET_CHUNK
    n_all = n_lat_chunks + n_ctx_chunks
    h0 = pl.program_id(1) * RET_HEADS_PER_STEP
    ii = lax.broadcasted_iota(jnp.int32, (c, c), 0).astype(F32)
    jj = lax.broadcasted_iota(jnp.int32, (c, c), 1).astype(F32)
    ri = lax.broadcasted_iota(jnp.int32, (c, 1), 0).astype(F32)
    state_ref[...] = jnp.zeros_like(state_ref)

    chains = []
    for hh in range(RET_HEADS_PER_STEP):
        sl = slice(hh * HEAD_DIM, (hh + 1) * HEAD_DIM)
        for d, (q_ref, k_ref, o_ref) in enumerate(((qf_ref, kf_ref, of_ref),
                                                   (qb_ref, kb_ref, ob_ref))):
            lg = ld_ref[d, h0 + hh]
            diff = ii - jj if d == 0 else jj - ii
            intra = jnp.where(diff >= 0, jnp.exp(lg * jnp.maximum(diff, 0.0)), 0.0)
            q_dec = jnp.exp(lg * (ri + 1.0 if d == 0 else c - ri))
            k_dec = jnp.exp(lg * (c - 1.0 - ri if d == 0 else ri))
            c_dec = jnp.exp(jnp.full((1, HEAD_DIM), lg * c, F32))
            chains.append((len(chains), q_ref, k_ref, o_ref, sl, (intra, q_dec, k_dec, c_dec), d))

    def body(s, _):
        in_ctx = s < n_ctx_chunks
        for slot, q_ref, k_ref, o_ref, sl, (intra, q_dec, k_dec, c_dec), d in chains:
            if d == 0:
                chunk = jnp.where(in_ctx, n_lat_chunks + s, s - n_ctx_chunks)
            else:
                chunk = n_all - 1 - s
            off = pl.multiple_of(chunk * c, c)
            q = q_ref[pl.ds(off, c), sl]
            k = k_ref[pl.ds(off, c), sl]
            v = v_ref[pl.ds(off, c), sl]
            st = state_ref[slot]
            sc = _dot_nt(q, k) * intra
            o = _dot(sc.astype(BF16), v) + _dot(q, st.astype(BF16)) * q_dec
            kd = (k.astype(F32) * k_dec).astype(BF16)
            state_ref[slot] = st * c_dec + _dot_tn(kd, v)
            mu = jnp.mean(o, axis=-1, keepdims=True)
            var = jnp.mean(jnp.square(o - mu), axis=-1, keepdims=True)
            o_ref[pl.ds(off, c), sl] = (o - mu) * lax.rsqrt(var + EPS)
        return 0

    lax.fori_loop(0, n_all, body, 0, unroll=True)


def _retention(rq, rk, rv, log_decay, bsz, seq_len, n_lat):
    hw = RET_HEADS_PER_STEP * HEAD_DIM
    n_hb = RET_HEADS // RET_HEADS_PER_STEP
    fwd = pl.BlockSpec((seq_len, hw), lambda b, h: (b, h))
    bwd = pl.BlockSpec((seq_len, hw), lambda b, h: (b, n_hb + h))
    return pl.pallas_call(
        functools.partial(_ret_kernel, n_lat_chunks=n_lat // RET_CHUNK,
                          n_ctx_chunks=(seq_len - n_lat) // RET_CHUNK),
        grid=(bsz, n_hb),
        in_specs=[pl.BlockSpec(memory_space=pltpu.SMEM), fwd, bwd, fwd, bwd, fwd],
        out_specs=[fwd, fwd],
        out_shape=[jax.ShapeDtypeStruct((bsz * seq_len, RET_WIDTH), F32)] * 2,
        scratch_shapes=[pltpu.VMEM((2 * RET_HEADS_PER_STEP, HEAD_DIM, HEAD_DIM), F32)],
        compiler_params=_cparams(("parallel", "parallel"), 56),
        name="retention",
    )(log_decay, rq, rq, rk, rk, rv)


def _outproj_kernel(a_ref, fl_ref, fc_ref, of_ref, ob_ref, sg_ref, w_ref, xl_ref, xc_ref, m_ref, g_ref,
                    wr_ref,
                    xo_ref, h_ref, aff_ref, *, n_lat_blk):
    is_ctx = pl.program_id(1) >= n_lat_blk
    fm = jnp.where(is_ctx, fc_ref[0], fl_ref[0].astype(BF16))
    rw = RET_WIDTH
    r = (sg_ref[:, :rw].astype(F32) * of_ref[...]
         + sg_ref[:, rw:].astype(F32) * ob_ref[...]).astype(BF16)
    a0, f0 = ATTN_WIDTH, ATTN_WIDTH + FOURIER_WIDTH
    y = (_dot(a_ref[...], w_ref[:a0, :]) + _dot(fm, w_ref[a0:f0, :])) + _dot(r, w_ref[f0:, :])
    x = jnp.where(is_ctx, xc_ref[...], xl_ref[0]) + m_ref[0, 2:3, :] * y
    xo_ref[0] = x
    h = _rms_mod(x, g_ref[...], m_ref[0, 3:4, :], m_ref[0, 4:5, :])
    h_ref[0] = h.astype(BF16)
    n_exp = aff_ref.shape[0]
    hh, hl = _split(h)
    wh, wl = _split(wr_ref[...])
    lg = _dot(hh, wh) + (_dot(hh, wl) + _dot(hl, wh))
    lg = jnp.where(lax.broadcasted_iota(jnp.int32, lg.shape, 1) < n_exp, lg, -jnp.inf)
    e = jnp.exp(lg - jnp.max(lg, axis=-1, keepdims=True))
    aff = e / jnp.sum(e, axis=-1, keepdims=True)
    aff_ref[...] = aff.T[:n_exp, :]


def _outproj(a, fm_lat, fm_ctx, o_f, o_b, sg, w_out, x_lat, x_ctx, mods, g, w_router, n_lat_blk):
    bsz, n_lat, d = x_lat.shape
    l = n_lat + x_ctx.shape[0] // bsz
    per_seq = l // ROW_BLK
    n_exp = w_router.shape[1]
    wr = jnp.pad(w_router, ((0, 0), (0, LANES - n_exp)))
    fw = fm_lat.shape[-1]

    def flat(b, t):
        return (b * per_seq + t, 0)

    return pl.pallas_call(
        functools.partial(_outproj_kernel, n_lat_blk=n_lat_blk),
        grid=(bsz, per_seq),
        in_specs=[
            pl.BlockSpec((ROW_BLK, a.shape[1]), flat),
            pl.BlockSpec((1, ROW_BLK, fw), lambda b, t: (b, jnp.minimum(t, n_lat_blk - 1), 0)),
            pl.BlockSpec((1, ROW_BLK, fw), lambda b, t: (b, 0, 0)),
            pl.BlockSpec((ROW_BLK, o_f.shape[1]), flat),
            pl.BlockSpec((ROW_BLK, o_b.shape[1]), flat),
            pl.BlockSpec((ROW_BLK, sg.shape[1]), flat),
            pl.BlockSpec(w_out.shape, lambda b, t: (0, 0)),
            pl.BlockSpec((1, ROW_BLK, d), lambda b, t: (b, jnp.minimum(t, n_lat_blk - 1), 0)),
            pl.BlockSpec((ROW_BLK, d), lambda b, t: (b, 0)),
            pl.BlockSpec((1, 6, d), lambda b, t: (jnp.where(t >= n_lat_blk, bsz, b), 0, 0)),
            pl.BlockSpec((1, d), lambda b, t: (0, 0)),
            pl.BlockSpec(wr.shape, lambda b, t: (0, 0)),
        ],
        out_specs=[
            pl.BlockSpec((1, ROW_BLK, d), lambda b, t: (b, t, 0)),
            pl.BlockSpec((1, ROW_BLK, d), lambda b, t: (b, t, 0)),
            pl.BlockSpec((n_exp, ROW_BLK), lambda b, t: (0, b * per_seq + t)),
        ],
        out_shape=[
            jax.ShapeDtypeStruct((bsz, l, d), F32),
            jax.ShapeDtypeStruct((bsz, l, d), BF16),
            jax.ShapeDtypeStruct((n_exp, bsz * l), F32),
        ],
        compiler_params=_cparams(("parallel", "parallel"), 48),
        name="outproj",
    )(a, fm_lat, fm_ctx, o_f, o_b, sg, w_out, x_lat, x_ctx, mods, g.reshape(1, d), wr)


def _cumsum_lanes(m, out_ref, fin):
    n_exp, n = m.shape
    tri = (lax.broadcasted_iota(jnp.int32, (LANES, LANES), 0)
           <= lax.broadcasted_iota(jnp.int32, (LANES, LANES), 1)).astype(BF16)
    run = jnp.zeros((n_exp, 1), F32)
    befores = []
    for k in range(n // LANES):
        befores.append(run)
        sl = slice(k * LANES, (k + 1) * LANES)
        cnt = _dot(m[:, sl].astype(BF16), tri) + run
        out_ref[0, :, sl] = fin(cnt, sl)
        run = cnt[:, LANES - 1:LANES]
    befores.append(run)
    return befores


def _select(seg, cap, slot_off, sel_ref, tmp_ref):
    bits = pltpu.bitcast(seg, jnp.int32)
    n_exp = seg.shape[0]

    def body(it, t):
        tt = t | lax.shift_left(jnp.int32(1), 30 - it)
        cnt = jnp.sum(jnp.where(bits >= tt, 1.0, 0.0), axis=1, keepdims=True)
        return jnp.where(cnt >= cap, tt, t)

    t = lax.fori_loop(0, 31, body, jnp.zeros((n_exp, 1), jnp.int32))
    gt = bits > t
    eq = bits == t
    need = cap - jnp.sum(jnp.where(gt, 1.0, 0.0), axis=1, keepdims=True)
    eqf = jnp.where(eq, 1.0, 0.0)
    _cumsum_lanes(eqf, tmp_ref, lambda cnt, sl: cnt)
    take = eq & (tmp_ref[0] - eqf < need)
    mask = gt | take
    maskf = jnp.where(mask, 1.0, 0.0)
    return _cumsum_lanes(
        maskf, sel_ref,
        lambda cnt, sl: jnp.where(maskf[:, sl] > 0.5, cnt - 1.0 + slot_off, -1.0))


def _routing_kernel(aff_ref, sel_l, gat_l, sel_c, gat_c, st_l, tmp_l, tmp_c, *, n_lat, cap_l, cap_c):
    b = pl.program_id(0)
    a = aff_ref[...]
    lat = a[:, :n_lat]
    ctx = a[:, n_lat:]
    gat_l[0] = lat
    gat_c[0] = ctx
    befores = _select(lat, float(cap_l), 0.0, sel_l, tmp_l)
    _select(ctx, float(cap_c), (b * cap_c).astype(F32), sel_c, tmp_c)
    lane = lax.broadcasted_iota(jnp.int32, (a.shape[0], LANES), 1)
    st = jnp.zeros((a.shape[0], LANES), F32)
    per_blk = ROW_BLK // LANES
    for j in range(n_lat // ROW_BLK + 1):
        st = jnp.where(lane == j, befores[j * per_blk], st)
    st_l[0] = st


def _routing(aff_t, bsz, seq_len, n_lat, cap_l, cap_c):
    n_exp = aff_t.shape[0]
    n_ctx = seq_len - n_lat
    shp = lambda n: jax.ShapeDtypeStruct((bsz, n_exp, n), F32)
    spec = lambda n: pl.BlockSpec((1, n_exp, n), lambda b: (b, 0, 0))
    return pl.pallas_call(
        functools.partial(_routing_kernel, n_lat=n_lat, cap_l=cap_l, cap_c=cap_c),
        grid=(bsz,),
        in_specs=[pl.BlockSpec((n_exp, seq_len), lambda b: (0, b))],
        out_specs=[spec(n_lat), spec(n_lat), spec(n_ctx), spec(n_ctx), spec(LANES)],
        out_shape=[shp(n_lat), shp(n_lat), shp(n_ctx), shp(n_ctx), shp(LANES)],
        scratch_shapes=[pltpu.VMEM((1, n_exp, n_lat), F32), pltpu.VMEM((1, n_exp, n_ctx), F32)],
        compiler_params=_cparams(("parallel",), 32),
        name="routing",
    )(aff_t)


EXPERT_GROUP = 8
SLOT_ALIGN = 16
SLOT_WINDOW = 64


def _windows(starts_ref, base, stride, n, j, cap, win):
    w, rounds = [], jnp.int32(0)
    for k in range(n):
        s0 = starts_ref[base + k * stride + j]
        s1 = starts_ref[base + k * stride + j + 1]
        wk = jnp.minimum((s0 // SLOT_ALIGN) * SLOT_ALIGN, cap - win)
        w.append(wk)
        rounds = jnp.maximum(rounds, (s1 - wk + win - 1) // win)
    return w, rounds


def _gather_kernel(starts_ref, h_ref, sel_ref, aff_ref, xg_ref, gate_ref, *, n_tb, cap, win, n_exp):
    v, g, j = pl.program_id(0), pl.program_id(1), pl.program_id(2)
    ng = sel_ref.shape[2]
    base = (v * n_exp + g * ng) * (n_tb + 1)

    @pl.when(j == 0)
    def _():
        xg_ref[...] = jnp.zeros_like(xg_ref)
        gate_ref[...] = jnp.zeros_like(gate_ref)

    w, rounds = _windows(starts_ref, base, n_tb + 1, ng, j, cap, win)
    tb = h_ref.shape[1]
    row_i = lax.broadcasted_iota(jnp.int32, (win, tb), 0).astype(F32)

    def body(r, _):
        starts, pieces, gates = [], [], []
        for k in range(ng):
            lo = w[k] + r * win
            c = pl.multiple_of(jnp.minimum(lo, cap - win), SLOT_ALIGN)
            sel = sel_ref[0, 0, k:k + 1, :]
            rel = jnp.where(sel >= lo.astype(F32), sel, -1.0) - c.astype(F32)
            hit = rel == row_i
            pieces.append(jnp.where(hit, 1.0, 0.0).astype(BF16))
            gates.append(jnp.sum(jnp.where(hit, aff_ref[0, 0, k:k + 1, :], 0.0), axis=1,
                                 keepdims=True))
            starts.append(c)
        res = _dot(jnp.concatenate(pieces, axis=0), h_ref[0])
        for k in range(ng):
            rows = pl.ds(starts[k], win)
            xg_ref[k, rows, :] = (xg_ref[k, rows, :].astype(F32)
                                  + res[k * win:(k + 1) * win, :]).astype(BF16)
            gate_ref[k, rows, :] += gates[k]
        return 0

    lax.fori_loop(0, rounds, body, 0)


def _moe_gather(starts, h, sel, aff, n_tok, cap, win):
    nb, _, d = h.shape
    n_exp = sel.shape[1]
    ng = min(EXPERT_GROUP, n_exp)
    n_tb = n_tok // ROW_BLK
    rows = pl.BlockSpec((1, 1, ng, ROW_BLK), lambda v, g, j, s: (v, g, 0, j))
    grouped = lambda t: t.reshape(nb, n_exp // ng, ng, n_tok)
    return pl.pallas_call(
        functools.partial(_gather_kernel, n_tb=n_tb, cap=cap, win=win, n_exp=n_exp),
        grid_spec=pltpu.PrefetchScalarGridSpec(
            num_scalar_prefetch=1,
            grid=(nb, n_exp // ng, n_tb),
            in_specs=[pl.BlockSpec((1, ROW_BLK, d), lambda v, g, j, s: (v, j, 0)), rows, rows],
            out_specs=[pl.BlockSpec((ng, cap, d), lambda v, g, j, s: (g, v, 0)),
                       pl.BlockSpec((ng, cap, 1), lambda v, g, j, s: (g, v, 0))],
        ),
        out_shape=[jax.ShapeDtypeStruct((n_exp, nb * cap, d), BF16),
                   jax.ShapeDtypeStruct((n_exp, nb * cap, 1), F32)],
        compiler_params=_cparams(("parallel", "parallel", "arbitrary"), 56),
        name="moe_gather",
    )(starts, h, grouped(sel), grouped(aff))


def _ffn_kernel(xl_ref, xc_ref, wg_ref, wu_ref, al_ref, ac_ref, wgb_ref, wub_ref):
    r = pl.program_id(2)

    @pl.when(r == 0)
    def _():
        wgb_ref[...] = wg_ref[0, 0].astype(BF16)
        wub_ref[...] = wu_ref[0, 0].astype(BF16)

    def swiglu(x_ref, o_ref):
        x = x_ref[0]
        o_ref[0] = (_silu(_dot(x, wgb_ref[...])) * _dot(x, wub_ref[...])).astype(BF16)

    @pl.when(r == 0)
    def _():
        swiglu(xc_ref, ac_ref)

    @pl.when(r > 0)
    def _():
        swiglu(xl_ref, al_ref)


def _moe_ffn(xg_l, xg_c, w_gate, w_up, layer):
    n_exp, rows, d = xg_l.shape
    rows_c = xg_c.shape[1]
    ff = w_gate.shape[3]
    fh = _largest_tile(ff, 512, LANES)
    rblk = _largest_tile(rows, 1024)
    nls = rows // rblk
    prev = lambda r: jnp.maximum(r - 1, 0)
    wspec = pl.BlockSpec((1, 1, d, fh), lambda e, f, r: (layer, e, 0, f))
    return pl.pallas_call(
        _ffn_kernel,
        grid=(n_exp, ff // fh, nls + 1),
        in_specs=[pl.BlockSpec((1, rblk, d), lambda e, f, r: (e, prev(r), 0)),
                  pl.BlockSpec((1, rows_c, d), lambda e, f, r: (e, 0, 0)), wspec, wspec],
        out_specs=[pl.BlockSpec((1, rblk, fh), lambda e, f, r: (e, prev(r), f)),
                   pl.BlockSpec((1, rows_c, fh), lambda e, f, r: (e, 0, f))],
        out_shape=[jax.ShapeDtypeStruct((n_exp, rows, ff), BF16),
                   jax.ShapeDtypeStruct((n_exp, rows_c, ff), BF16)],
        scratch_shapes=[pltpu.VMEM((d, fh), BF16), pltpu.VMEM((d, fh), BF16)],
        compiler_params=_cparams(("parallel", "parallel", "arbitrary"), 48),
        name="moe_ffn",
    )(xg_l, xg_c, w_gate, w_up)


def _down_kernel(al_ref, ac_ref, gl_ref, gc_ref, wd_ref, yl_ref, yc_ref, wdb_ref):
    v = pl.program_id(1)

    @pl.when(v == 0)
    def _():
        wdb_ref[...] = wd_ref[0, 0].astype(BF16)

    @pl.when(v == 0)
    def _():
        yc_ref[0] = (_dot(ac_ref[0], wdb_ref[...]) * gc_ref[0]).astype(BF16)

    @pl.when(v > 0)
    def _():
        yl_ref[0] = (_dot(al_ref[0], wdb_ref[...]) * gl_ref[0]).astype(BF16)


def _moe_down(act_l, act_c, gate_l, gate_c, w_down, layer, cap):
    n_exp, rows, ff = act_l.shape
    rows_c = act_c.shape[1]
    d = w_down.shape[3]
    nb = rows // cap
    prev = lambda v: jnp.maximum(v - 1, 0)
    return pl.pallas_call(
        _down_kernel,
        grid=(n_exp, nb + 1),
        in_specs=[
            pl.BlockSpec((1, cap, ff), lambda e, v: (e, prev(v), 0)),
            pl.BlockSpec((1, rows_c, ff), lambda e, v: (e, 0, 0)),
            pl.BlockSpec((1, cap, 1), lambda e, v: (e, prev(v), 0)),
            pl.BlockSpec((1, rows_c, 1), lambda e, v: (e, 0, 0)),
            pl.BlockSpec((1, 1, ff, d), lambda e, v: (layer, e, 0, 0)),
        ],
        out_specs=[pl.BlockSpec((1, cap, d), lambda e, v: (prev(v), e, 0)),
                   pl.BlockSpec((1, rows_c, d), lambda e, v: (0, e, 0))],
        out_shape=[jax.ShapeDtypeStruct((nb, n_exp * cap, d), BF16),
                   jax.ShapeDtypeStruct((1, n_exp * rows_c, d), BF16)],
        scratch_shapes=[pltpu.VMEM((ff, d), BF16)],
        compiler_params=_cparams(("parallel", "arbitrary"), 48),
        name="moe_down",
    )(act_l, act_c, gate_l, gate_c, w_down)


def _combine_kernel(starts_ref, y_ref, sel_ref, x_ref, m_ref, out_ref, st_ref, *, n_tb, cap, win, n_exp):
    v, j = pl.program_id(0), pl.program_id(2)
    base = v * n_exp * (n_tb + 1)
    w, rounds = _windows(starts_ref, base, n_tb + 1, n_exp, j, cap, win)
    cols = lax.broadcasted_iota(jnp.int32, (n_exp, n_exp * win), 1)
    spread = jnp.where(cols // win == lax.broadcasted_iota(jnp.int32, (n_exp, n_exp * win), 0),
                       1.0, 0.0).astype(BF16)
    slot_in_win = (lax.broadcasted_iota(jnp.int32, (1, n_exp * win), 1) % win + 1).astype(F32)
    exp_lane = lax.broadcasted_iota(jnp.int32, (1, n_exp), 1)

    def one_round(r):
        lo_vec = jnp.zeros((1, n_exp), F32)
        c_vec = jnp.zeros((1, n_exp), F32)
        for e in range(n_exp):
            lo = w[e] + r * win
            c = pl.multiple_of(jnp.minimum(lo, cap - win), SLOT_ALIGN)
            src = pl.multiple_of(e * cap + c, SLOT_ALIGN)
            st_ref[e * win:(e + 1) * win, :] = y_ref[0, pl.ds(src, win), :]
            lo_vec = jnp.where(exp_lane == e, lo.astype(F32), lo_vec)
            c_vec = jnp.where(exp_lane == e, c.astype(F32), c_vec)
        sel = sel_ref[0]
        rel = jnp.where(sel >= lo_vec, sel, -1.0) - c_vec
        rel1 = jnp.where((rel >= 0.0) & (rel < float(win)), rel + 1.0, 0.0)
        hit = _dot(rel1.astype(BF16), spread) == slot_in_win
        return _dot(jnp.where(hit, 1.0, 0.0).astype(BF16), st_ref[...])

    gate2 = m_ref[0, 5:6, :]
    out_ref[0] = x_ref[0] + gate2 * one_round(0)

    def body(r, _):
        out_ref[0] += gate2 * one_round(r)
        return 0

    lax.fori_loop(1, rounds, body, 0)


def _moe_combine(starts, y, sel_t, x, mods, x_block, mod_row, n_tok, cap, win):
    nb, _, d = y.shape
    n_exp = sel_t.shape[2]
    n_tb = n_tok // ROW_BLK
    pw = _largest_tile(d, 1024, LANES)
    return pl.pallas_call(
        functools.partial(_combine_kernel, n_tb=n_tb, cap=cap, win=win, n_exp=n_exp),
        grid_spec=pltpu.PrefetchScalarGridSpec(
            num_scalar_prefetch=1,
            grid=(nb, d // pw, n_tb),
            in_specs=[pl.BlockSpec((1, n_exp * cap, pw), lambda v, p, j, s: (v, 0, p)),
                      pl.BlockSpec((1, ROW_BLK, n_exp), lambda v, p, j, s: (v, j, 0)),
                      pl.BlockSpec((1, ROW_BLK, pw), lambda v, p, j, s: (*x_block(v, j), p)),
                      pl.BlockSpec((1, 6, pw), lambda v, p, j, s: (mod_row(v, j), 0, p))],
            out_specs=pl.BlockSpec((1, ROW_BLK, pw), lambda v, p, j, s: (v, j, p)),
            scratch_shapes=[pltpu.VMEM((n_exp * win, pw), BF16)],
        ),
        out_shape=jax.ShapeDtypeStruct((nb, n_tok, d), F32),
        compiler_params=_cparams(("parallel", "parallel", "arbitrary"), 48),
        name="moe_combine",
    )(starts, y, sel_t, x, mods)


def _rope_tables(pos_groups, half):
    freqs = ROPE_BASE ** (-jnp.arange(half, dtype=F32) / half)
    cos, sin = [], []
    for pos in pos_groups:
        ang = pos.astype(F32)[:, None] * freqs[None, :]
        c, s = jnp.cos(ang), jnp.sin(ang)
        cos += [c, c]
        sin += [-s, s]
    return jnp.stack([jnp.concatenate(t, axis=1) for t in (cos, sin)])


def _attn_tables(n_lat, n_ctx):
    s = jnp.arange(n_lat)
    tab = _rope_tables([s // GRID_W, s % GRID_W], HEAD_DIM // 4)
    ident = jnp.stack([jnp.ones((n_ctx, HEAD_DIM), F32), jnp.zeros((n_ctx, HEAD_DIM), F32)])
    return jnp.concatenate([tab, ident], axis=1)


def _ret_tables(n_lat, n_ctx):
    s = jnp.arange(n_lat)
    t = jnp.arange(n_ctx)
    fwd = jnp.concatenate([n_ctx + s, t])
    bwd = jnp.concatenate([n_ctx + (n_lat - 1 - s), n_ctx - 1 - t])
    return jnp.concatenate([_rope_tables([fwd], HEAD_DIM // 2),
                            _rope_tables([bwd], HEAD_DIM // 2)], axis=0)


def kernel(x, c, ctx, c_ctx, w_ada, b_ada, norm_mix, norm_ffn, w_in, q_norm, k_norm, ret_log_decay,
           w_out, w_router, w_gate, w_up, w_down):
    bsz, n_lat, d = x.shape
    n_ctx = ctx.shape[1]
    seq_len = n_lat + n_ctx
    depth = w_ada.shape[0]
    n_exp = w_router.shape[2]
    n_lat_blk = n_lat // ROW_BLK
    assert n_lat % ROW_BLK == 0 and n_ctx == ROW_BLK and bsz + 1 <= 8
    cap_l = CAPACITY_FACTOR * n_lat // n_exp
    cap_c = CAPACITY_FACTOR * n_ctx // n_exp

    cond8 = jnp.zeros((8, d), F32).at[:bsz].set(c).at[bsz].set(c_ctx)
    mods = _adaln(cond8, w_ada, b_ada).reshape(depth, 8, 6, d)

    attn_tab = _attn_tables(n_lat, n_ctx)
    ret_tab = _ret_tables(n_lat, n_ctx)
    wc = _chan_mats()
    two_stage = n_lat % (FFT_COLS * 2 * SUBLANES) == 0
    lat_mats = dict(fft=_fft_mats(n_lat)) if two_stage else dict(dft=_dft_mats(n_lat))
    dft_c = _dft_mats(n_ctx)

    o_q, o_k = 0, ATTN_WIDTH
    o_v = o_k + KV_WIDTH
    o_f = o_v + KV_WIDTH
    o_rq = o_f + FOURIER_WIDTH
    o_rk = o_rq + RET_WIDTH
    o_rv = o_rk + RET_WIDTH
    o_g = o_rv + RET_WIDTH
    o_end = o_g + 2 * RET_WIDTH
    q_scale = HEAD_DIM ** -0.5 * math.log2(math.e)

    sb_c = bsz * cap_c
    win_l = min(cap_l, SLOT_WINDOW)
    win_c = cap_c
    starts_c = jnp.tile(jnp.arange(bsz + 1, dtype=jnp.int32) * cap_c, n_exp)

    x_lat, x_ctx = x, ctx.reshape(bsz * n_ctx, d)
    for l in range(depth):
        h = _norm(x_lat, x_ctx, mods[l], norm_mix[l], n_lat_blk)
        h2d = h.reshape(bsz * seq_len, d)
        wl = w_in[l].astype(BF16)
        q = _proj(h2d, wl[:, o_q:o_k], seq_len, "qk", gain=q_norm[l], tab=attn_tab,
                  n_heads=ATTN_HEADS, scale=q_scale, half=HEAD_DIM // 4)
        k = _proj(h2d, wl[:, o_k:o_v], seq_len, "qk", gain=k_norm[l], tab=attn_tab,
                  n_heads=ATTN_KV_HEADS, scale=1.0, half=HEAD_DIM // 4)
        v, f = _proj(h2d, wl[:, o_v:o_rq], seq_len, "split", split_at=KV_WIDTH)
        rq = _proj(h2d, wl[:, o_rq:o_rk], seq_len, "ret", tab=ret_tab, n_heads=RET_HEADS,
                   scale=1.0, half=HEAD_DIM // 2)
        rk = _proj(h2d, wl[:, o_rk:o_rv], seq_len, "ret", tab=ret_tab, n_heads=RET_HEADS,
                   scale=HEAD_DIM ** -0.5, half=HEAD_DIM // 2)
        rv = _proj(h2d, wl[:, o_rv:o_g], seq_len, "plain")
        sg = _proj(h2d, wl[:, o_g:o_end], seq_len, "silu")

        a = _attention(q, k, v, bsz, seq_len, n_lat)
        fm_lat = _fourier(f, bsz, seq_len, 0, n_lat, wc, **lat_mats)
        fm_ctx = _fourier(f, bsz, seq_len, n_lat, n_ctx, wc, dft=dft_c)
        o_f, o_b = _retention(rq, rk, rv, ret_log_decay[l], bsz, seq_len, n_lat)

        xs, h2, aff_t = _outproj(a, fm_lat, fm_ctx, o_f, o_b, sg, w_out[l].astype(BF16), x_lat, x_ctx,
                                 mods[l],
                                 norm_ffn[l], w_router[l], n_lat_blk)

        sel_l, gat_l, sel_c, gat_c, st_l = _routing(aff_t, bsz, seq_len, n_lat, cap_l, cap_c)
        starts_l = st_l[:, :, :n_lat_blk + 1].astype(jnp.int32).reshape(-1)
        merge = lambda t: t.transpose(1, 0, 2).reshape(1, n_exp, bsz * n_ctx)
        tok_major = lambda t: t.transpose(0, 2, 1)
        sel_c, gat_c = merge(sel_c), merge(gat_c)
        h_ctx = h2[:, n_lat:, :].reshape(1, bsz * n_ctx, d)

        xg_l, gate_l = _moe_gather(starts_l, h2, sel_l, gat_l, n_lat, cap_l, win_l)
        xg_c, gate_c = _moe_gather(starts_c, h_ctx, sel_c, gat_c, bsz * n_ctx, sb_c, win_c)
        act_l, act_c = _moe_ffn(xg_l, xg_c, w_gate, w_up, l)
        y_l, y_c = _moe_down(act_l, act_c, gate_l, gate_c, w_down, l, cap_l)
        x_lat = _moe_combine(starts_l, y_l, tok_major(sel_l), xs, mods[l], lambda v, j: (v, j),
                             lambda v, j: v, n_lat, cap_l, win_l)
        if l + 1 < depth:
            x_ctx = _moe_combine(starts_c, y_c, tok_major(sel_c), xs, mods[l],
                                 lambda v, j: (j, n_lat_blk), lambda v, j: bsz, bsz * n_ctx, sb_c,
                                 win_c).reshape(bsz * n_ctx, d)

    return x_lat
```
